```python
import jax, jax.numpy as jnp
from jax import lax
import numpy as np

D_MODEL = 1024
BATCH = 8
SEQ = 4096
DEPTH = 1
DEC_BATCH = 128
DEC_SEQ = 1
PAST_LEN = 16384
PAGE_SIZE = 128

N_META = 16
MIX_WIDTH = D_MODEL
ATTN_WIDTH = MIX_WIDTH // 2
POOL_WIDTH = MIX_WIDTH - ATTN_WIDTH
HEAD_DIM = 64
N_HEADS = ATTN_WIDTH // HEAD_DIM
N_KV_HEADS = max(1, N_HEADS // 4)
GQA_GROUP = N_HEADS // N_KV_HEADS
WINDOW = 128
BLOCK = 128
SM_SCALE = HEAD_DIM ** -0.5
POOL_WINDOWS = (2, 4, 8, 16)
N_POOL_GROUPS = len(POOL_WINDOWS)
POOL_GROUP_WIDTH = POOL_WIDTH // N_POOL_GROUPS
POOL_STATE = max(POOL_WINDOWS) - 1
D_FF = ((8 * D_MODEL // 3 + 127) // 128) * 128
CONV_WIDTH = 3
CONV_STATE = CONV_WIDTH - 1
QKV_WIDTH = (N_HEADS + 2 * N_KV_HEADS) * HEAD_DIM
IN_WIDTH = QKV_WIDTH + POOL_WIDTH
RMS_EPS = 1e-6

kernel_name = 'hymba_pool_swa_sink_convffn_step'

F32 = jnp.float32


def rms_norm(x, g):
    xf = x.astype(F32)
    y = xf * lax.rsqrt(jnp.mean(xf * xf, axis=-1, keepdims=True) + RMS_EPS)
    return (y * g.astype(F32)).astype(x.dtype)


def alibi_slopes():
    exps = jnp.arange(1, N_HEADS + 1, dtype=F32) * (8.0 / N_HEADS)
    return jnp.exp2(-exps).reshape(N_KV_HEADS, GQA_GROUP)


def mixer_projection(h, w_in, b_in):
    z = h @ w_in + b_in
    lead = z.shape[:-1]
    nq = N_HEADS * HEAD_DIM
    nk = N_KV_HEADS * HEAD_DIM
    q = z[..., :nq].reshape(*lead, N_KV_HEADS, GQA_GROUP, HEAD_DIM)
    k = z[..., nq:nq + nk].reshape(*lead, N_KV_HEADS, HEAD_DIM)
    v = z[..., nq + nk:nq + 2 * nk].reshape(*lead, N_KV_HEADS, HEAD_DIM)
    u = z[..., nq + 2 * nk:]
    return q, k, v, u


def sink_softmax_attend(q, k, v, dist, valid, sinks, slopes):
    s = jnp.einsum('...qkgd,...skd->...kgqs', q.astype(F32), k.astype(F32)) * SM_SCALE
    s = s - slopes[:, :, None, None] * dist.astype(F32)
    s = jnp.where(valid, s, -jnp.inf)
    sink = sinks.astype(F32).reshape(N_KV_HEADS, GQA_GROUP, 1, 1)
    m = jnp.maximum(jnp.max(s, axis=-1, keepdims=True), sink)
    p = jnp.exp(s - m)
    w = p / (jnp.sum(p, axis=-1, keepdims=True) + jnp.exp(sink - m))
    return jnp.einsum('...kgqs,...skd->...qkgd', w, v.astype(F32))


def window_attention_prompt(q, k, v, sinks, slopes):
    b, L = q.shape[:2]
    front = (-L) % BLOCK
    n_blk = (L + front) // BLOCK
    qb = jnp.pad(q, ((0, 0), (front, 0), (0, 0), (0, 0), (0, 0))).reshape(
        b, n_blk, BLOCK, N_KV_HEADS, GQA_GROUP, HEAD_DIM)

    def band(t):
        tp = jnp.pad(t, ((0, 0), (front + BLOCK, 0), (0, 0), (0, 0))).reshape(
            b, n_blk + 1, BLOCK, N_KV_HEADS, HEAD_DIM)
        return jnp.concatenate([tp[:, :-1], tp[:, 1:]], axis=2)

    qi = jnp.arange(BLOCK)[:, None]
    kj = jnp.arange(2 * BLOCK)[None, :]
    dist = BLOCK + qi - kj
    kpos = (jnp.arange(n_blk)[:, None, None] - 1) * BLOCK + kj[None] - front
    valid = (dist >= 0) & (dist <= WINDOW) & (kpos >= 0)
    out = sink_softmax_attend(qb, band(k), band(v), dist, valid[:, None, None], sinks, slopes)
    return out.reshape(b, n_blk * BLOCK, ATTN_WIDTH)[:, front:].astype(q.dtype)


def window_attention_step(q, k_ext, v_ext, sinks, slopes):
    b, S = q.shape[:2]
    qi = jnp.arange(S)[:, None]
    kj = jnp.arange(WINDOW + S)[None, :]
    dist = WINDOW + qi - kj
    valid = (dist >= 0) & (dist <= WINDOW)
    out = sink_softmax_attend(q, k_ext, v_ext, dist, valid, sinks, slopes)
    return out.reshape(b, S, ATTN_WIDTH).astype(q.dtype)


def multiscale_pool(u, first_pos, w_pool, pool_scale):
    b, R, _ = u.shape
    max_w = max(POOL_WINDOWS)
    uf = u.astype(F32)
    cs = jnp.pad(jnp.cumsum(uf, axis=1), ((0, 0), (max_w, 0), (0, 0)))
    pos = first_pos + jnp.arange(R)
    means = []
    for g, w in enumerate(POOL_WINDOWS):
        sl = slice(g * POOL_GROUP_WIDTH, (g + 1) * POOL_GROUP_WIDTH)
        win_sum = cs[:, max_w:, sl] - cs[:, max_w - w:max_w - w + R, sl]
        cnt = jnp.minimum(pos + 1, w).astype(F32)[None, :, None]
        means.append(win_sum / cnt)
    mean = jnp.stack(means, axis=2)
    ug = uf.reshape(b, R, N_POOL_GROUPS, POOL_GROUP_WIDTH)
    mixed = jnp.einsum('brgc,gcd->brgd', mean - ug, w_pool.astype(F32))
    return (mixed.reshape(b, R, POOL_WIDTH) * pool_scale.astype(F32)).astype(u.dtype)


def conv_ffn(up_ext, conv_w, conv_b, w_down):
    L = up_ext.shape[1] - CONV_STATE
    c = conv_b.astype(F32) + up_ext[:, 0:L].astype(F32) * conv_w[0].astype(F32)
    for i in range(1, CONV_WIDTH):
        c = c + up_ext[:, i:i + L].astype(F32) * conv_w[i].astype(F32)
    a = jax.nn.gelu(c[..., :D_FF], approximate=True) * c[..., D_FF:]
    return a.astype(up_ext.dtype) @ w_down


def decoder_layer(x, k_buf, v_buf, pool_buf, conv_buf, lw):
    (w_in, b_in, sinks, w_pool, pool_scale, g_attn_out, g_pool_out, w_o,
     g_pre_mix, g_post_mix, g_pre_ffn, g_post_ffn, w_up, conv_w, conv_b, w_down) = lw
    slopes = alibi_slopes()
    b = x.shape[0]
    h = rms_norm(x, g_pre_mix)
    q, k, v, u = mixer_projection(h, w_in, b_in)
    if k_buf is None:
        k_all, v_all, u_all = k, v, u
        attn = window_attention_prompt(q, k, v, sinks, slopes)
        pool = multiscale_pool(u, 0, w_pool, pool_scale)
    else:
        k_all = jnp.concatenate([k_buf.astype(k.dtype), k], axis=1)
        v_all = jnp.concatenate([v_buf.astype(v.dtype), v], axis=1)
        u_all = jnp.concatenate([pool_buf.astype(u.dtype), u], axis=1)
        attn = window_attention_step(q, k_all, v_all, sinks, slopes)
        pool = multiscale_pool(u_all, PAST_LEN - POOL_STATE, w_pool, pool_scale)[:, POOL_STATE:]
    mix = jnp.concatenate([rms_norm(attn, g_attn_out), rms_norm(pool, g_pool_out)], axis=-1) @ w_o
    x = x + rms_norm(mix, g_post_mix)
    up = rms_norm(x, g_pre_ffn) @ w_up
    if conv_buf is None:
        up_ext = jnp.concatenate([jnp.zeros((b, CONV_STATE, up.shape[-1]), up.dtype), up], axis=1)
    else:
        up_ext = jnp.concatenate([conv_buf.astype(up.dtype), up], axis=1)
    x = x + rms_norm(conv_ffn(up_ext, conv_w, conv_b, w_down), g_post_ffn)
    return (x, k_all[:, -WINDOW:], v_all[:, -WINDOW:], u_all[:, -POOL_STATE:], up_ext[:, -CONV_STATE:])


def setup_inputs(seed: int = 0) -> dict:
    key = jax.random.key(seed)
    ks = jax.random.split(key, 24)

    def nrm(k, shape, scale):
        return jax.random.normal(k, shape, F32) * scale

    def gain(k, shape):
        return 1.0 + nrm(k, shape, 0.05)

    return {
        'x_prompt': nrm(ks[0], (BATCH, SEQ, D_MODEL), 1.0),
        'x_sample': nrm(ks[1], (DEC_BATCH, DEC_SEQ, D_MODEL), 1.0),
        'cache_k': nrm(ks[2], (DEPTH, DEC_BATCH, WINDOW, N_KV_HEADS, HEAD_DIM), 1.0),
        'cache_v': nrm(ks[3], (DEPTH, DEC_BATCH, WINDOW, N_KV_HEADS, HEAD_DIM), 1.0),
        'state_pool': nrm(ks[4], (DEPTH, DEC_BATCH, POOL_STATE, POOL_WIDTH), 1.0),
        'state_conv': nrm(ks[5], (DEPTH, DEC_BATCH, CONV_STATE, 2 * D_FF), 1.0),
        'meta': nrm(ks[6], (N_META, D_MODEL), 1.0),
        'w_in': nrm(ks[7], (DEPTH, D_MODEL, IN_WIDTH), D_MODEL ** -0.5),
        'b_in': nrm(ks[8], (DEPTH, IN_WIDTH), 0.02),
        'sinks': nrm(ks[9], (DEPTH, N_HEADS), 1.0),
        'w_pool': nrm(ks[10], (DEPTH, N_POOL_GROUPS, POOL_GROUP_WIDTH, POOL_GROUP_WIDTH), POOL_GROUP_WIDTH ** -0.5),
        'pool_scale': 1.0 + nrm(ks[11], (DEPTH, POOL_WIDTH), 0.1),
        'g_attn_out': gain(ks[12], (DEPTH, ATTN_WIDTH)),
        'g_pool_out': gain(ks[13], (DEPTH, POOL_WIDTH)),
        'w_o': nrm(ks[14], (DEPTH, MIX_WIDTH, D_MODEL), MIX_WIDTH ** -0.5),
        'g_pre_mix': gain(ks[15], (DEPTH, D_MODEL)),
        'g_post_mix': gain(ks[16], (DEPTH, D_MODEL)),
        'g_pre_ffn': gain(ks[17], (DEPTH, D_MODEL)),
        'g_post_ffn': gain(ks[18], (DEPTH, D_MODEL)),
        'w_up': nrm(ks[19], (DEPTH, D_MODEL, 2 * D_FF), D_MODEL ** -0.5),
        'conv_w': nrm(ks[20], (DEPTH, CONV_WIDTH, 2 * D_FF), CONV_WIDTH ** -0.5),
        'conv_b': nrm(ks[21], (DEPTH, 2 * D_FF), 0.02),
        'w_down': nrm(ks[22], (DEPTH, D_FF, D_MODEL), D_FF ** -0.5),
    }


def reference(x_prompt, x_sample, cache_k, cache_v, state_pool, state_conv, meta,
              w_in, b_in, sinks, w_pool, pool_scale, g_attn_out, g_pool_out, w_o,
              g_pre_mix, g_post_mix, g_pre_ffn, g_post_ffn, w_up, conv_w, conv_b, w_down):
    b = x_prompt.shape[0]
    xp = jnp.concatenate(
        [jnp.broadcast_to(meta.astype(x_prompt.dtype)[None], (b, N_META, D_MODEL)), x_prompt], axis=1)
    xs = x_sample
    kp_l, vp_l, pp_l, cp_l = [], [], [], []
    ks_l, vs_l, ps_l, cs_l = [], [], [], []
    for l in range(DEPTH):
        lw = (w_in[l], b_in[l], sinks[l], w_pool[l], pool_scale[l], g_attn_out[l], g_pool_out[l], w_o[l],
              g_pre_mix[l], g_post_mix[l], g_pre_ffn[l], g_post_ffn[l], w_up[l], conv_w[l], conv_b[l], w_down[l])
        xp, kp, vp, pp, cp = decoder_layer(xp, None, None, None, None, lw)
        xs, k_s, v_s, p_s, c_s = decoder_layer(xs, cache_k[l], cache_v[l], state_pool[l], state_conv[l], lw)
        kp_l.append(kp); vp_l.append(vp); pp_l.append(pp); cp_l.append(cp)
        ks_l.append(k_s); vs_l.append(v_s); ps_l.append(p_s); cs_l.append(c_s)
    y_prompt = xp[:, N_META:]
    return (y_prompt, xs,
            jnp.stack(kp_l), jnp.stack(vp_l), jnp.stack(pp_l), jnp.stack(cp_l),
            jnp.stack(ks_l), jnp.stack(vs_l), jnp.stack(ps_l), jnp.stack(cs_l))
```

```python
import functools

import numpy as np
import jax
import jax.numpy as jnp
from jax import lax
from jax.experimental import pallas as pl
from jax.experimental.pallas import tpu as pltpu

F32 = jnp.float32
BF16 = jnp.bfloat16

N_META = 16
HEAD_DIM = 64
N_HEADS = 8
N_KV_HEADS = 2
WINDOW = 128
POOL_WINDOWS = (2, 4, 8, 16)
POOL_STATE = 15
CONV_STATE = 2
RMS_EPS = 1e-6
SM_SCALE = HEAD_DIM ** -0.5
SLOPES = tuple(2.0 ** (-(h + 1) * (8.0 / N_HEADS)) for h in range(N_HEADS))

LANES = 128
SUBLANES = 8
KV_W = N_KV_HEADS * HEAD_DIM
Q_W = N_HEADS * HEAD_DIM
POOL_W = 512
POOL_GW = POOL_W // len(POOL_WINDOWS)
U_HIST = 16

SEQ_TILE = 256
FF_CHUNK = 256
SAMPLE_TB = 32
VMEM_LIMIT = 48 * 1024 * 1024


def _rms(x, g):
    ms = jnp.mean(x * x, axis=-1, keepdims=True)
    return x * lax.rsqrt(ms + RMS_EPS) * g


def _dot(a, b):
    return jnp.dot(a, b, preferred_element_type=F32)


def _dot_nt(a, b):
    return lax.dot_general(a, b, (((1,), (1,)), ((), ())), preferred_element_type=F32)


def _lane_lo(shape):
    return lax.broadcasted_iota(jnp.int32, shape, len(shape) - 1) < HEAD_DIM


def _attend_block(qs, k2, v2, pos_start, sink_of):
    nq, nk = qs.shape[0], k2.shape[0]
    qi = lax.broadcasted_iota(jnp.int32, (nq, nk), 0)
    kj = lax.broadcasted_iota(jnp.int32, (nq, nk), 1)
    valid = (kj >= qi) & (kj <= qi + WINDOW) & (kj >= WINDOW - pos_start)
    neg_dist = jnp.where(valid, (kj - qi - WINDOW).astype(F32), -jnp.inf)
    lo = _lane_lo((nq, LANES))
    pieces = []
    for p in range(N_HEADS // 2):
        g = (2 * p) // (N_HEADS // N_KV_HEADS)
        blk = qs[:, p * LANES:(p + 1) * LANES]
        rolled = pltpu.roll(blk, HEAD_DIM, axis=1)
        keep = lo if g == 0 else jnp.logical_not(lo)
        outs = []
        for e in range(2):
            h = 2 * p + e
            src = blk if e == g else rolled
            qf = jnp.where(keep, src, 0.0).astype(BF16)
            s = _dot_nt(qf, k2) + SLOPES[h] * neg_dist
            sink = sink_of(h)
            m = jnp.maximum(jnp.max(s, axis=-1, keepdims=True), sink)
            pe = jnp.exp(s - m)
            den = jnp.sum(pe, axis=-1, keepdims=True) + jnp.exp(sink - m)
            outs.append(_dot(pe.astype(BF16), v2) / den)
        if g == 0:
            pieces.append(jnp.where(lo, outs[0], pltpu.roll(outs[1], HEAD_DIM, axis=1)))
        else:
            pieces.append(jnp.where(lo, pltpu.roll(outs[0], HEAD_DIM, axis=1), outs[1]))
    return jnp.concatenate(pieces, axis=1)


def _pool_rows(ubuf, rows, w_pool_ref, pool_scale, cnt_of):
    outs = []
    for g, w in enumerate(POOL_WINDOWS):
        cs = slice(g * POOL_GW, (g + 1) * POOL_GW)
        cur = ubuf[U_HIST:U_HIST + rows, cs]
        acc = cur
        for j in range(1, w):
            acc = acc + ubuf[U_HIST - j:U_HIST - j + rows, cs]
        mean = acc / cnt_of(w)
        outs.append(_dot((mean - cur).astype(BF16), w_pool_ref[g]))
    return jnp.concatenate(outs, axis=1) * pool_scale


def _mix_residual(x, attn, pool, g_attn, g_pool, w_o_ref, g_post_mix):
    mixin = jnp.concatenate([_rms(attn, g_attn), _rms(pool, g_pool)], axis=1).astype(BF16)
    return x + _rms(_dot(mixin, w_o_ref[...]), g_post_mix)


def _gated(conv, tc):
    return (jax.nn.gelu(conv[:, :tc], approximate=True) * conv[:, tc:]).astype(BF16)


def _meta_kernel(x_ref, w_in_ref, b_in_ref, sink_ref, w_pool_ref, pscale_ref, g_attn_ref,
                 g_pool_ref, w_o_ref, g_pre_mix_ref, g_post_mix_ref, g_pre_ffn_ref, wup_ref,
                 k0_ref, v0_ref, u0_ref, up0_ref, ubuf):
    rows = x_ref.shape[0]
    pos0 = N_META - rows
    x = x_ref[...]
    z = _dot(_rms(x, g_pre_mix_ref[...]).astype(BF16), w_in_ref[...]) + b_in_ref[...]
    k = z[:, Q_W:Q_W + KV_W]
    v = z[:, Q_W + KV_W:Q_W + 2 * KV_W]
    u = z[:, Q_W + 2 * KV_W:]
    pos = pos0 + lax.broadcasted_iota(jnp.int32, (rows, 1), 0)
    zeros_kv = jnp.zeros((WINDOW, KV_W), BF16)
    k2 = jnp.concatenate([zeros_kv, k.astype(BF16)], axis=0)
    v2 = jnp.concatenate([zeros_kv, v.astype(BF16)], axis=0)
    attn = _attend_block(z[:, :Q_W] * SM_SCALE, k2, v2, pos0, lambda h: sink_ref[h])

    ubuf[0:U_HIST, :] = jnp.zeros((U_HIST, POOL_W), F32)
    ubuf[U_HIST:, :] = jnp.where(pos >= 0, u, 0.0)
    cnt_of = lambda w: jnp.clip(pos + 1, 1, w).astype(F32)
    pool = _pool_rows(ubuf, rows, w_pool_ref, pscale_ref[...], cnt_of)

    x1 = _mix_residual(x, attn, pool, g_attn_ref[...], g_pool_ref[...], w_o_ref,
                       g_post_mix_ref[...])
    h2 = _rms(x1, g_pre_ffn_ref[...]).astype(BF16)
    k0_ref[...] = k
    v0_ref[...] = v
    u0_ref[...] = u[rows - U_HIST:, :]
    tail = 2 * SUBLANES
    for c in range(wup_ref.shape[0]):
        up_tail = _dot(h2[rows - tail:, :], wup_ref[c])
        up0_ref[c] = up_tail[tail - CONV_STATE:, :]


def _prompt_kernel(x_ref, k0_ref, v0_ref, u0_ref, up0_ref, w_in_ref, b_in_ref, sink_ref,
                   w_pool_ref, pscale_ref, g_attn_ref, g_pool_ref, w_o_ref, g_pre_mix_ref,
                   g_post_mix_ref, g_pre_ffn_ref, g_post_ffn_ref, wup_ref, cw_ref, cb_ref,
                   wdown_ref,
                   y_ref, kout_ref, vout_ref, uout_ref, cout_ref,
                   kbuf, vbuf, ubuf, upst, upbuf, acc, attn_buf):
    t = pl.program_id(1)
    rows = x_ref.shape[1]
    n_chunks, _, tc2 = wup_ref.shape
    tc = tc2 // 2
    up_rows = slice(SUBLANES - CONV_STATE, SUBLANES)

    @pl.when(t == 0)
    def _():
        kbuf[0:WINDOW, :] = k0_ref[...]
        vbuf[0:WINDOW, :] = v0_ref[...]
        ubuf[0:U_HIST, :] = u0_ref[...]
        upst[:, up_rows, :] = up0_ref[...]

    x = x_ref[0]
    z = _dot(_rms(x, g_pre_mix_ref[...]).astype(BF16), w_in_ref[...]) + b_in_ref[...]
    kbuf[WINDOW:, :] = z[:, Q_W:Q_W + KV_W]
    vbuf[WINDOW:, :] = z[:, Q_W + KV_W:Q_W + 2 * KV_W]
    ubuf[U_HIST:, :] = z[:, Q_W + 2 * KV_W:]
    qs = z[:, :Q_W] * SM_SCALE

    pos_tile = N_META + t * rows
    for i in range(rows // WINDOW):
        r0 = i * WINDOW
        k2 = kbuf[r0:r0 + 2 * WINDOW, :].astype(BF16)
        v2 = vbuf[r0:r0 + 2 * WINDOW, :].astype(BF16)
        attn_buf[r0:r0 + WINDOW, :] = _attend_block(
            qs[r0:r0 + WINDOW, :], k2, v2, pos_tile + r0, lambda h: sink_ref[h])

    pool = _pool_rows(ubuf, rows, w_pool_ref, pscale_ref[...], lambda w: float(w))
    x1 = _mix_residual(x, attn_buf[...], pool, g_attn_ref[...], g_pool_ref[...], w_o_ref,
                       g_post_mix_ref[...])
    h2 = _rms(x1, g_pre_ffn_ref[...]).astype(BF16)

    acc[...] = jnp.zeros(acc.shape, F32)

    def ffn_chunk(c, carry):
        up = _dot(h2, wup_ref[c])
        upbuf[up_rows, :] = upst[c, up_rows, :]
        upbuf[SUBLANES:, :] = up
        upst[c, up_rows, :] = up[rows - CONV_STATE:, :]
        cw = cw_ref[c]
        conv = cb_ref[c] + upbuf[SUBLANES - 2:SUBLANES - 2 + rows, :] * cw[0:1, :]
        conv = conv + upbuf[SUBLANES - 1:SUBLANES - 1 + rows, :] * cw[1:2, :]
        conv = conv + up * cw[2:3, :]
        acc[...] += _dot(_gated(conv, tc), wdown_ref[c])
        return carry

    lax.fori_loop(0, n_chunks, ffn_chunk, 0)
    y_ref[0] = x1 + _rms(acc[...], g_post_ffn_ref[...])

    kbuf[0:WINDOW, :] = kbuf[rows:rows + WINDOW, :]
    vbuf[0:WINDOW, :] = vbuf[rows:rows + WINDOW, :]
    ubuf[0:U_HIST, :] = ubuf[rows:rows + U_HIST, :]

    @pl.when(t == pl.num_programs(1) - 1)
    def _():
        kout_ref[0] = kbuf[0:WINDOW, :]
        vout_ref[0] = vbuf[0:WINDOW, :]
        uout_ref[0] = ubuf[0:U_HIST, :]
        cout_ref[0] = upst[:, up_rows, :]


def _sample_a_kernel(x_ref, ck_ref, cv_ref, w_in_ref, b_in_ref, g_pre_mix_ref, sinkcol_ref,
                     slopecol_ref,
                     attn_ref, u_ref, ko_ref, vo_ref,
                     qs_buf, kn_buf, vn_buf):
    n_tok = x_ref.shape[0]
    z = _dot(_rms(x_ref[...], g_pre_mix_ref[...]).astype(BF16), w_in_ref[...]) + b_in_ref[...]
    qs_buf[...] = z[:, :Q_W] * SM_SCALE
    kn_buf[...] = z[:, Q_W:Q_W + KV_W]
    vn_buf[...] = z[:, Q_W + KV_W:Q_W + 2 * KV_W]
    u_ref[...] = z[:, Q_W + 2 * KV_W:]

    lo1 = _lane_lo((1, LANES))
    kj = lax.broadcasted_iota(jnp.int32, (N_HEADS, WINDOW), 1)
    neg_dist = (kj - WINDOW).astype(F32)
    bias = slopecol_ref[...] * neg_dist
    sink = sinkcol_ref[...]

    def token(b, carry):
        qrow = qs_buf[pl.ds(b, 1), :]
        kn = kn_buf[pl.ds(b, 1), :]
        vn = vn_buf[pl.ds(b, 1), :]
        heads = []
        for h in range(N_HEADS):
            p, e, g = h // 2, h % 2, h // (N_HEADS // N_KV_HEADS)
            blk = qrow[:, p * LANES:(p + 1) * LANES]
            src = blk if e == g else pltpu.roll(blk, HEAD_DIM, axis=1)
            keep = lo1 if g == 0 else jnp.logical_not(lo1)
            heads.append(jnp.where(keep, src, 0.0))
        qf = jnp.concatenate(heads, axis=0).astype(BF16)
        s = _dot_nt(qf, ck_ref[b].astype(BF16)) + bias
        s_self = jnp.sum(qf.astype(F32) * kn.astype(BF16).astype(F32), axis=-1, keepdims=True)
        m = jnp.maximum(jnp.maximum(jnp.max(s, axis=-1, keepdims=True), s_self), sink)
        pe = jnp.exp(s - m)
        pe_self = jnp.exp(s_self - m)
        den = jnp.sum(pe, axis=-1, keepdims=True) + pe_self + jnp.exp(sink - m)
        o = _dot(pe.astype(BF16), cv_ref[b].astype(BF16))
        o = o + pe_self.astype(BF16).astype(F32) * vn.astype(BF16).astype(F32)
        o = o / den
        pieces = []
        for p in range(N_HEADS // 2):
            g = (2 * p) // (N_HEADS // N_KV_HEADS)
            a, c = o[2 * p:2 * p + 1, :], o[2 * p + 1:2 * p + 2, :]
            if g == 0:
                pieces.append(jnp.where(lo1, a, pltpu.roll(c, HEAD_DIM, axis=1)))
            else:
                pieces.append(jnp.where(lo1, pltpu.roll(a, HEAD_DIM, axis=1), c))
        attn_ref[pl.ds(b, 1), :] = jnp.concatenate(pieces, axis=1)
        ko_ref[b, 0:WINDOW - 1, :] = ck_ref[b, 1:WINDOW, :]
        ko_ref[b, WINDOW - 1:WINDOW, :] = kn
        vo_ref[b, 0:WINDOW - 1, :] = cv_ref[b, 1:WINDOW, :]
        vo_ref[b, WINDOW - 1:WINDOW, :] = vn
        return carry

    lax.fori_loop(0, n_tok, token, 0)


def _sample_b_kernel(x_ref, attn_ref, u_ref, sp_ref, sc_ref, w_pool_ref, pscale_ref,
                     g_attn_ref, g_pool_ref, w_o_ref, g_post_mix_ref, g_pre_ffn_ref,
                     g_post_ffn_ref, wup_ref, cw_ref, cb_ref, wdown_ref,
                     y_ref, po_ref, co_ref):
    n_chunks, _, tc2 = wup_ref.shape
    tc = tc2 // 2
    d_ff = n_chunks * tc
    row_w = 2 * d_ff
    u = u_ref[...]

    outs = []
    for g, w in enumerate(POOL_WINDOWS):
        c0 = g * POOL_GW
        cur = u[:, c0:c0 + POOL_GW]
        acc_u = cur
        for j in range(1, w):
            base = (POOL_STATE - j) * POOL_W + c0
            acc_u = acc_u + sp_ref[:, base:base + POOL_GW]
        outs.append(_dot((acc_u / float(w) - cur).astype(BF16), w_pool_ref[g]))
    pool = jnp.concatenate(outs, axis=1) * pscale_ref[...]
    po_ref[:, 0:(POOL_STATE - 1) * POOL_W] = sp_ref[:, POOL_W:]
    po_ref[:, (POOL_STATE - 1) * POOL_W:] = u

    x1 = _mix_residual(x_ref[...], attn_ref[...], pool, g_attn_ref[...], g_pool_ref[...],
                       w_o_ref, g_post_mix_ref[...])
    h2 = _rms(x1, g_pre_ffn_ref[...]).astype(BF16)

    co_ref[:, 0:row_w] = sc_ref[:, row_w:]
    ffn = jnp.zeros(x1.shape, F32)
    for c in range(n_chunks):
        up = _dot(h2, wup_ref[c])
        gcols = slice(c * tc, (c + 1) * tc)
        vcols = slice(d_ff + c * tc, d_ff + (c + 1) * tc)
        co_ref[:, row_w + gcols.start:row_w + gcols.stop] = up[:, :tc]
        co_ref[:, row_w + vcols.start:row_w + vcols.stop] = up[:, tc:]
        old0 = jnp.concatenate([sc_ref[:, gcols], sc_ref[:, vcols]], axis=1)
        old1 = jnp.concatenate([sc_ref[:, row_w + gcols.start:row_w + gcols.stop],
                                sc_ref[:, row_w + vcols.start:row_w + vcols.stop]], axis=1)
        cw = cw_ref[c]
        conv = cb_ref[c] + old0 * cw[0:1, :]
        conv = conv + old1 * cw[1:2, :]
        conv = conv + up * cw[2:3, :]
        ffn = ffn + _dot(_gated(conv, tc), wdown_ref[c])
    y_ref[...] = x1 + _rms(ffn, g_post_ffn_ref[...])


def _vmem():
    return pl.BlockSpec(memory_space=pltpu.VMEM)


def _smem():
    return pl.BlockSpec(memory_space=pltpu.SMEM)


def _resident(shape):
    nd = len(shape)
    return pl.BlockSpec(shape, lambda *_: (0,) * nd, pipeline_mode=pl.Buffered(1))


def _chunk_cols(a, tc):
    lead = a.shape[:-1]
    n_chunks = a.shape[-1] // (2 * tc)
    a = a.reshape(*lead, 2, n_chunks, tc)
    a = jnp.moveaxis(a, -2, 0)
    return a.reshape(n_chunks, *lead, 2 * tc)


def _unchunk_cols(a):
    n_chunks, r, tc2 = a.shape
    a = a.reshape(n_chunks, r, 2, tc2 // 2)
    return jnp.transpose(a, (1, 2, 0, 3)).reshape(r, n_chunks * tc2)


def kernel(x_prompt, x_sample, cache_k, cache_v, state_pool, state_conv, meta, w_in, b_in, sinks,
           w_pool, pool_scale, g_attn_out, g_pool_out, w_o, g_pre_mix, g_post_mix, g_pre_ffn,
           g_post_ffn, w_up, conv_w, conv_b, w_down):
    assert w_in.shape[0] == 1, "single layer"
    batch, seq, d_model = x_prompt.shape
    dec_batch = x_sample.shape[0]
    d_ff = w_down.shape[1]
    tc = FF_CHUNK
    n_chunks = d_ff // tc
    assert n_chunks * tc == d_ff and seq % SEQ_TILE == 0 and dec_batch % SAMPLE_TB == 0
    assert meta.shape[0] == N_META and N_META > POOL_STATE

    row = lambda a: a[0].reshape(1, -1)
    w_in_b = w_in[0].astype(BF16)
    b_in_r = row(b_in)
    w_pool_b = w_pool[0].astype(BF16)
    w_o_b = w_o[0].astype(BF16)
    wup_c = _chunk_cols(w_up[0].astype(BF16), tc)
    cw_c = _chunk_cols(conv_w[0], tc)
    cb_c = _chunk_cols(conv_b[0].reshape(1, -1), tc)
    wdown_c = w_down[0].astype(BF16).reshape(n_chunks, tc, d_model)
    sink_s = sinks[0]
    sink_col = sinks[0].reshape(N_HEADS, 1)
    slope_col = jnp.asarray(np.array(SLOPES, np.float32).reshape(N_HEADS, 1))
    gains = dict(pscale=row(pool_scale), g_attn=row(g_attn_out), g_pool=row(g_pool_out),
                 g_pre_mix=row(g_pre_mix), g_post_mix=row(g_post_mix),
                 g_pre_ffn=row(g_pre_ffn), g_post_ffn=row(g_post_ffn))

    x_meta = jnp.concatenate([jnp.zeros((WINDOW - N_META, d_model), F32), meta.astype(F32)], 0)
    k0, v0, u0, up0 = pl.pallas_call(
        _meta_kernel,
        out_shape=(jax.ShapeDtypeStruct((WINDOW, KV_W), F32),
                   jax.ShapeDtypeStruct((WINDOW, KV_W), F32),
                   jax.ShapeDtypeStruct((U_HIST, POOL_W), F32),
                   jax.ShapeDtypeStruct((n_chunks, CONV_STATE, 2 * tc), F32)),
        in_specs=[_vmem(), _vmem(), _vmem(), _smem()] + [_vmem()] * 9,
        out_specs=(_vmem(),) * 4,
        scratch_shapes=[pltpu.VMEM((U_HIST + WINDOW, POOL_W), F32)],
        compiler_params=pltpu.CompilerParams(vmem_limit_bytes=VMEM_LIMIT),
        name="meta",
    )(x_meta, w_in_b, b_in_r, sink_s, w_pool_b, gains["pscale"], gains["g_attn"],
      gains["g_pool"], w_o_b, gains["g_pre_mix"], gains["g_post_mix"], gains["g_pre_ffn"], wup_c)

    n_tiles = seq // SEQ_TILE
    per_batch = lambda shape: pl.BlockSpec((1,) + shape, lambda b, t: (b,) + (0,) * len(shape))
    prompt_inputs = (
        x_prompt, k0, v0, u0, up0, w_in_b, b_in_r, sink_s, w_pool_b, gains["pscale"],
        gains["g_attn"], gains["g_pool"], w_o_b, gains["g_pre_mix"], gains["g_post_mix"],
        gains["g_pre_ffn"], gains["g_post_ffn"], wup_c, cw_c, cb_c, wdown_c)
    in_specs = [pl.BlockSpec((1, SEQ_TILE, d_model), lambda b, t: (b, t, 0))]
    in_specs += [_smem() if a is sink_s else _resident(a.shape) for a in prompt_inputs[1:]]
    y_prompt, k_p, v_p, u_p, c_p = pl.pallas_call(
        _prompt_kernel,
        grid=(batch, n_tiles),
        out_shape=(jax.ShapeDtypeStruct((batch, seq, d_model), F32),
                   jax.ShapeDtypeStruct((batch, WINDOW, KV_W), F32),
                   jax.ShapeDtypeStruct((batch, WINDOW, KV_W), F32),
                   jax.ShapeDtypeStruct((batch, U_HIST, POOL_W), F32),
                   jax.ShapeDtypeStruct((batch, n_chunks, CONV_STATE, 2 * tc), F32)),
        in_specs=in_specs,
        out_specs=(pl.BlockSpec((1, SEQ_TILE, d_model), lambda b, t: (b, t, 0)),
                   per_batch((WINDOW, KV_W)), per_batch((WINDOW, KV_W)),
                   per_batch((U_HIST, POOL_W)), per_batch((n_chunks, CONV_STATE, 2 * tc))),
        scratch_shapes=[
            pltpu.VMEM((WINDOW + SEQ_TILE, KV_W), F32),
            pltpu.VMEM((WINDOW + SEQ_TILE, KV_W), F32),
            pltpu.VMEM((U_HIST + SEQ_TILE, POOL_W), F32),
            pltpu.VMEM((n_chunks, SUBLANES, 2 * tc), F32),
            pltpu.VMEM((SUBLANES + SEQ_TILE, 2 * tc), F32),
            pltpu.VMEM((SEQ_TILE, d_model), F32),
            pltpu.VMEM((SEQ_TILE, Q_W), F32),
        ],
        compiler_params=pltpu.CompilerParams(
            dimension_semantics=("arbitrary", "arbitrary"), vmem_limit_bytes=VMEM_LIMIT),
        name="prompt",
    )(*prompt_inputs)

    ck = cache_k[0].reshape(dec_batch, WINDOW, KV_W)
    cv = cache_v[0].reshape(dec_batch, WINDOW, KV_W)
    xs = x_sample.reshape(dec_batch, d_model)
    tb = SAMPLE_TB
    tok = lambda w: pl.BlockSpec((tb, w), lambda i: (i, 0))
    cache_spec = pl.BlockSpec((tb, WINDOW, KV_W), lambda i: (i, 0, 0))
    attn_s, u_s, k_s, v_s = pl.pallas_call(
        _sample_a_kernel,
        grid=(dec_batch // tb,),
        out_shape=(jax.ShapeDtypeStruct((dec_batch, Q_W), F32),
                   jax.ShapeDtypeStruct((dec_batch, POOL_W), F32),
                   jax.ShapeDtypeStruct((dec_batch, WINDOW, KV_W), F32),
                   jax.ShapeDtypeStruct((dec_batch, WINDOW, KV_W), F32)),
        in_specs=[tok(d_model), cache_spec, cache_spec, _resident(w_in_b.shape),
                  _resident(b_in_r.shape), _resident(gains["g_pre_mix"].shape),
                  _resident(sink_col.shape), _resident(slope_col.shape)],
        out_specs=(tok(Q_W), tok(POOL_W), cache_spec, cache_spec),
        scratch_shapes=[pltpu.VMEM((tb, Q_W), F32), pltpu.VMEM((tb, KV_W), F32),
                        pltpu.VMEM((tb, KV_W), F32)],
        compiler_params=pltpu.CompilerParams(
            dimension_semantics=("arbitrary",), vmem_limit_bytes=VMEM_LIMIT),
        name="sample_a",
    )(xs, ck, cv, w_in_b, b_in_r, gains["g_pre_mix"], sink_col, slope_col)

    sp = state_pool[0].reshape(dec_batch, POOL_STATE * POOL_W)
    sc = state_conv[0].reshape(dec_batch, CONV_STATE * 2 * d_ff)
    y_s, pool_s, conv_s = pl.pallas_call(
        _sample_b_kernel,
        out_shape=(jax.ShapeDtypeStruct((dec_batch, d_model), F32),
                   jax.ShapeDtypeStruct(sp.shape, F32),
                   jax.ShapeDtypeStruct(sc.shape, F32)),
        in_specs=[_vmem()] * 17,
        out_specs=(_vmem(),) * 3,
        compiler_params=pltpu.CompilerParams(vmem_limit_bytes=VMEM_LIMIT),
        name="sample_b",
    )(xs, attn_s, u_s, sp, sc, w_pool_b, gains["pscale"], gains["g_attn"], gains["g_pool"],
      w_o_b, gains["g_post_mix"], gains["g_pre_ffn"], gains["g_post_ffn"], wup_c, cw_c, cb_c,
      wdown_c)

    kv_shape = (1, batch, WINDOW, N_KV_HEADS, HEAD_DIM)
    kv_s_shape = (1, dec_batch, WINDOW, N_KV_HEADS, HEAD_DIM)
    conv_p = jax.vmap(_unchunk_cols)(c_p)
    return (y_prompt,
            y_s.reshape(dec_batch, 1, d_model),
            k_p.reshape(kv_shape), v_p.reshape(kv_shape),
            u_p[:, U_HIST - POOL_STATE:, :][None],
            conv_p[None],
            k_s.reshape(kv_s_shape), v_s.reshape(kv_s_shape),
            pool_s.reshape(1, dec_batch, POOL_STATE, POOL_W),
            conv_s.reshape(1, dec_batch, CONV_STATE, 2 * d_ff))
```

```python
import numpy as np
import jax
import jax.numpy as jnp
from jax import lax
from jax.experimental import pallas as pl
from jax.experimental.pallas import tpu as pltpu

F32 = jnp.float32
BF16 = jnp.bfloat16

N_META = 16
HEAD_DIM = 64
N_HEADS = 8
N_KV_HEADS = 2
WINDOW = 128
POOL_WINDOWS = (2, 4, 8, 16)
POOL_STATE = 15
CONV_STATE = 2
RMS_EPS = 1e-6
SM_SCALE = HEAD_DIM ** -0.5
SLOPES = tuple(2.0 ** (-(h + 1) * (8.0 / N_HEADS)) for h in range(N_HEADS))

LANES = 128
SUBLANES = 8
KV_W = N_KV_HEADS * HEAD_DIM
Q_W = N_HEADS * HEAD_DIM
POOL_W = 512
POOL_GW = POOL_W // len(POOL_WINDOWS)
U_HIST = 16

SEQ_TILE = 512
FF_CHUNK = 256
SAMPLE_TB = 32
VMEM_LIMIT = 56 * 1024 * 1024
TAIL_SLABS = 4
GELU_C1 = float(np.sqrt(2.0 / np.pi))
GELU_C2 = GELU_C1 * 0.044715


def _rms(x, g):
    ms = jnp.mean(x * x, axis=-1, keepdims=True)
    return x * lax.rsqrt(ms + RMS_EPS) * g


def _dot(a, b):
    return jnp.dot(a, b, preferred_element_type=F32)


def _dot_nt(a, b):
    return lax.dot_general(a, b, (((1,), (1,)), ((), ())), preferred_element_type=F32)


def _lane_lo(shape):
    return lax.broadcasted_iota(jnp.int32, shape, len(shape) - 1) < HEAD_DIM


def _attend_block(qs, k2, v2, pos_start, sink_of):
    nq, nk = qs.shape[0], k2.shape[0]
    qi = lax.broadcasted_iota(jnp.int32, (nq, nk), 0)
    kj = lax.broadcasted_iota(jnp.int32, (nq, nk), 1)
    valid = (kj >= qi) & (kj <= qi + WINDOW) & (kj >= WINDOW - pos_start)
    neg_dist = jnp.where(valid, (kj - qi - WINDOW).astype(F32), -jnp.inf)
    lo = _lane_lo((nq, LANES))
    pieces = []
    for p in range(N_HEADS // 2):
        g = (2 * p) // (N_HEADS // N_KV_HEADS)
        blk = qs[:, p * LANES:(p + 1) * LANES]
        rolled = pltpu.roll(blk, HEAD_DIM, axis=1)
        keep = lo if g == 0 else jnp.logical_not(lo)
        outs = []
        for e in range(2):
            h = 2 * p + e
            src = blk if e == g else rolled
            qf = jnp.where(keep, src, 0.0).astype(BF16)
            s = _dot_nt(qf, k2) + SLOPES[h] * neg_dist
            sink = sink_of(h)
            m = jnp.maximum(jnp.max(s, axis=-1, keepdims=True), sink)
            pe = jnp.exp(s - m)
            den = jnp.sum(pe, axis=-1, keepdims=True) + jnp.exp(sink - m)
            outs.append(_dot(pe.astype(BF16), v2) / den)
        if g == 0:
            pieces.append(jnp.where(lo, outs[0], pltpu.roll(outs[1], HEAD_DIM, axis=1)))
        else:
            pieces.append(jnp.where(lo, pltpu.roll(outs[0], HEAD_DIM, axis=1), outs[1]))
    return jnp.concatenate(pieces, axis=1)


def _pool_rows(ubuf, rows, w_pool_ref, pool_scale, cnt_of):
    outs = []
    for g, w in enumerate(POOL_WINDOWS):
        cs = slice(g * POOL_GW, (g + 1) * POOL_GW)
        cur = ubuf[U_HIST:U_HIST + rows, cs]
        acc = cur
        for j in range(1, w):
            acc = acc + ubuf[U_HIST - j:U_HIST - j + rows, cs]
        mean = acc / cnt_of(w)
        outs.append(_dot((mean - cur).astype(BF16), w_pool_ref[g]))
    return jnp.concatenate(outs, axis=1) * pool_scale


def _mix_residual(x, attn, pool, g_attn, g_pool, w_o_ref, g_post_mix):
    mixin = jnp.concatenate([_rms(attn, g_attn), _rms(pool, g_pool)], axis=1).astype(BF16)
    return x + _rms(_dot(mixin, w_o_ref[...]), g_post_mix)


def _gated(conv, tc):
    g, hv = conv[:, :tc], conv[:, tc:]
    inner = g * (GELU_C1 + GELU_C2 * (g * g))
    return (g * (1.0 + jnp.tanh(inner)) * hv).astype(BF16)


def _row_slabs(rows):
    na = rows // SUBLANES - TAIL_SLABS
    assert na % 4 == 0 and (na // 4) % 2 == 1
    slabs = [(v, na) for v in range(na)]
    slabs += [(SUBLANES * na + v, TAIL_SLABS) for v in range(TAIL_SLABS)]
    return slabs, na


def _shifted_rows(ub, prev, na):
    S = SUBLANES
    width = ub.shape[1]
    sub = lax.broadcasted_iota(jnp.int32, (S, width), 0)
    slab = lambda i: ub[i * S:(i + 1) * S, :]

    def wrap(x, y):
        return pltpu.roll(jnp.where(sub == S - 1, y, x), 1, axis=0)

    a1 = wrap(slab(na - 1), prev[S:2 * S, :])
    a2 = wrap(slab(na - 2), prev[0:S, :])
    b1 = wrap(slab(na + 3), slab(na - 1))
    b2 = wrap(slab(na + 2), slab(na - 2))
    s1 = jnp.concatenate([a1, ub[0:(na - 1) * S, :], b1, ub[na * S:(na + 3) * S, :]], axis=0)
    s2 = jnp.concatenate([a2, a1, ub[0:(na - 2) * S, :], b2, b1, ub[na * S:(na + 2) * S, :]],
                         axis=0)
    return s1, s2


def _meta_kernel(x_ref, w_in_ref, b_in_ref, sink_ref, w_pool_ref, pscale_ref, g_attn_ref,
                 g_pool_ref, w_o_ref, g_pre_mix_ref, g_post_mix_ref, g_pre_ffn_ref, wup_ref,
                 k0_ref, v0_ref, u0_ref, up0_ref, ubuf):
    rows = x_ref.shape[0]
    pos0 = N_META - rows
    x = x_ref[...]
    z = _dot(_rms(x, g_pre_mix_ref[...]).astype(BF16), w_in_ref[...]) + b_in_ref[...]
    k = z[:, Q_W:Q_W + KV_W]
    v = z[:, Q_W + KV_W:Q_W + 2 * KV_W]
    u = z[:, Q_W + 2 * KV_W:]
    pos = pos0 + lax.broadcasted_iota(jnp.int32, (rows, 1), 0)
    zeros_kv = jnp.zeros((WINDOW, KV_W), BF16)
    k2 = jnp.concatenate([zeros_kv, k.astype(BF16)], axis=0)
    v2 = jnp.concatenate([zeros_kv, v.astype(BF16)], axis=0)
    attn = _attend_block(z[:, :Q_W] * SM_SCALE, k2, v2, pos0, lambda h: sink_ref[h])

    ubuf[0:U_HIST, :] = jnp.zeros((U_HIST, POOL_W), F32)
    ubuf[U_HIST:, :] = jnp.where(pos >= 0, u, 0.0)
    cnt_of = lambda w: jnp.clip(pos + 1, 1, w).astype(F32)
    pool = _pool_rows(ubuf, rows, w_pool_ref, pscale_ref[...], cnt_of)

    x1 = _mix_residual(x, attn, pool, g_attn_ref[...], g_pool_ref[...], w_o_ref,
                       g_post_mix_ref[...])
    h2 = _rms(x1, g_pre_ffn_ref[...]).astype(BF16)
    k0_ref[...] = k
    v0_ref[...] = v
    u0_ref[...] = u[rows - U_HIST:, :]
    tail = 2 * SUBLANES
    for c in range(wup_ref.shape[0]):
        up_tail = _dot(h2[rows - tail:, :], wup_ref[c])
        up0_ref[c] = up_tail[tail - CONV_STATE:, :]


def _prompt_kernel(x_ref, k0_ref, v0_ref, u0_ref, up0_ref, w_in_ref, b_in_ref, sink_ref,
                   w_pool_ref, pscale_ref, g_attn_ref, g_pool_ref, w_o_ref, g_pre_mix_ref,
                   g_post_mix_ref, g_pre_ffn_ref, g_post_ffn_ref, wup_ref, cw_ref, cb_ref,
                   wdown_ref,
                   y_ref, kout_ref, vout_ref, uout_ref, cout_ref,
                   kbuf, vbuf, ubuf, upst, upbuf, abuf, acc, attn_buf, xbuf, obuf):
    t = pl.program_id(1)
    rows = x_ref.shape[1]
    n_chunks, d_model, tc2 = wup_ref.shape
    tc = tc2 // 2
    n_col = d_model // LANES
    S = SUBLANES
    slabs, na = _row_slabs(rows)
    st2_row, st1_row = S - 1, 2 * S - 1

    @pl.when(t == 0)
    def _():
        kbuf[0:WINDOW, :] = k0_ref[...]
        vbuf[0:WINDOW, :] = v0_ref[...]
        ubuf[0:U_HIST, :] = u0_ref[...]
        upst[...] = jnp.zeros(upst.shape, F32)
        upst[:, st2_row:st2_row + 1, :] = up0_ref[:, 0:1, :]
        upst[:, st1_row:st1_row + 1, :] = up0_ref[:, 1:2, :]

    x = x_ref[0]
    z = _dot(_rms(x, g_pre_mix_ref[...]).astype(BF16), w_in_ref[...]) + b_in_ref[...]
    kbuf[WINDOW:, :] = z[:, Q_W:Q_W + KV_W]
    vbuf[WINDOW:, :] = z[:, Q_W + KV_W:Q_W + 2 * KV_W]
    ubuf[U_HIST:, :] = z[:, Q_W + 2 * KV_W:]
    qs = z[:, :Q_W] * SM_SCALE

    pos_tile = N_META + t * rows
    for i in range(rows // WINDOW):
        r0 = i * WINDOW
        k2 = kbuf[r0:r0 + 2 * WINDOW, :].astype(BF16)
        v2 = vbuf[r0:r0 + 2 * WINDOW, :].astype(BF16)
        attn_buf[r0:r0 + WINDOW, :] = _attend_block(
            qs[r0:r0 + WINDOW, :], k2, v2, pos_tile + r0, lambda h: sink_ref[h])

    pool = _pool_rows(ubuf, rows, w_pool_ref, pscale_ref[...], lambda w: float(w))
    x1 = _mix_residual(x, attn_buf[...], pool, g_attn_ref[...], g_pool_ref[...], w_o_ref,
                       g_post_mix_ref[...])

    for j in range(n_col):
        xbuf[j] = x1[:, j * LANES:(j + 1) * LANES]

    def x1_slab_order():
        return jnp.concatenate(
            [jnp.concatenate([xbuf[j, pl.ds(start, S, stride=stride), :] for j in range(n_col)],
                             axis=1) for start, stride in slabs], axis=0)

    h2 = _rms(x1_slab_order(), g_pre_ffn_ref[...]).astype(BF16)

    def up_project(c):
        upbuf[c % 2] = _dot(h2, wup_ref[c])

    def activate(c):
        ub = upbuf.at[c % 2]
        s1, s2 = _shifted_rows(ub, upst[c], na)
        upst[c] = ub[rows - 2 * S:rows, :]
        cw = cw_ref[c]
        conv = cb_ref[c] + s2 * cw[0:1, :]
        conv = conv + s1 * cw[1:2, :]
        conv = conv + ub[...] * cw[2:3, :]
        abuf[c % 2] = _gated(conv, tc)

    def down_project(c):
        part = _dot(abuf[c % 2], wdown_ref[c])
        if c == 0:
            acc[...] = part
        else:
            acc[...] += part

    up_project(0)
    for c in range(n_chunks + 1):
        if c + 1 < n_chunks:
            up_project(c + 1)
        if c < n_chunks:
            activate(c)
        if c >= 1:
            down_project(c - 1)

    y = x1_slab_order() + _rms(acc[...], g_post_ffn_ref[...])
    for i, (start, stride) in enumerate(slabs):
        for j in range(n_col):
            obuf[j, pl.ds(start, S, stride=stride), :] = y[i * S:(i + 1) * S,
                                                          j * LANES:(j + 1) * LANES]
    y_ref[0] = jnp.concatenate([obuf[j] for j in range(n_col)], axis=1)

    kbuf[0:WINDOW, :] = kbuf[rows:rows + WINDOW, :]
    vbuf[0:WINDOW, :] = vbuf[rows:rows + WINDOW, :]
    ubuf[0:U_HIST, :] = ubuf[rows:rows + U_HIST, :]

    @pl.when(t == pl.num_programs(1) - 1)
    def _():
        kout_ref[0] = kbuf[0:WINDOW, :]
        vout_ref[0] = vbuf[0:WINDOW, :]
        uout_ref[0] = ubuf[0:U_HIST, :]
        cout_ref[0, :, 0:1, :] = upst[:, st2_row:st2_row + 1, :]
        cout_ref[0, :, 1:2, :] = upst[:, st1_row:st1_row + 1, :]


def _sample_a_kernel(x_ref, ck_ref, cv_ref, w_in_ref, b_in_ref, g_pre_mix_ref, sinkcol_ref,
                     slopecol_ref,
                     attn_ref, u_ref, ko_ref, vo_ref,
                     qs_buf, kn_buf, vn_buf):
    n_tok = x_ref.shape[0]
    z = _dot(_rms(x_ref[...], g_pre_mix_ref[...]).astype(BF16), w_in_ref[...]) + b_in_ref[...]
    qs_buf[...] = z[:, :Q_W] * SM_SCALE
    kn_buf[...] = z[:, Q_W:Q_W + KV_W]
    vn_buf[...] = z[:, Q_W + KV_W:Q_W + 2 * KV_W]
    u_ref[...] = z[:, Q_W + 2 * KV_W:]

    lo1 = _lane_lo((1, LANES))
    kj = lax.broadcasted_iota(jnp.int32, (N_HEADS, WINDOW), 1)
    neg_dist = (kj - WINDOW).astype(F32)
    bias = slopecol_ref[...] * neg_dist
    sink = sinkcol_ref[...]

    def token(b, carry):
        qrow = qs_buf[pl.ds(b, 1), :]
        kn = kn_buf[pl.ds(b, 1), :]
        vn = vn_buf[pl.ds(b, 1), :]
        heads = []
        for h in range(N_HEADS):
            p, e, g = h // 2, h % 2, h // (N_HEADS // N_KV_HEADS)
            blk = qrow[:, p * LANES:(p + 1) * LANES]
            src = blk if e == g else pltpu.roll(blk, HEAD_DIM, axis=1)
            keep = lo1 if g == 0 else jnp.logical_not(lo1)
            heads.append(jnp.where(keep, src, 0.0))
        qf = jnp.concatenate(heads, axis=0).astype(BF16)
        s = _dot_nt(qf, ck_ref[b].astype(BF16)) + bias
        s_self = jnp.sum(qf.astype(F32) * kn.astype(BF16).astype(F32), axis=-1, keepdims=True)
        m = jnp.maximum(jnp.maximum(jnp.max(s, axis=-1, keepdims=True), s_self), sink)
        pe = jnp.exp(s - m)
        pe_self = jnp.exp(s_self - m)
        den = jnp.sum(pe, axis=-1, keepdims=True) + pe_self + jnp.exp(sink - m)
        o = _dot(pe.astype(BF16), cv_ref[b].astype(BF16))
        o = o + pe_self.astype(BF16).astype(F32) * vn.astype(BF16).astype(F32)
        o = o / den
        pieces = []
        for p in range(N_HEADS // 2):
            g = (2 * p) // (N_HEADS // N_KV_HEADS)
            a, c = o[2 * p:2 * p + 1, :], o[2 * p + 1:2 * p + 2, :]
            if g == 0:
                pieces.append(jnp.where(lo1, a, pltpu.roll(c, HEAD_DIM, axis=1)))
            else:
                pieces.append(jnp.where(lo1, pltpu.roll(a, HEAD_DIM, axis=1), c))
        attn_ref[pl.ds(b, 1), :] = jnp.concatenate(pieces, axis=1)
        ko_ref[b, 0:WINDOW - 1, :] = ck_ref[b, 1:WINDOW, :]
        ko_ref[b, WINDOW - 1:WINDOW, :] = kn
        vo_ref[b, 0:WINDOW - 1, :] = cv_ref[b, 1:WINDOW, :]
        vo_ref[b, WINDOW - 1:WINDOW, :] = vn
        return carry

    lax.fori_loop(0, n_tok, token, 0)


def _sample_b_kernel(x_ref, attn_ref, u_ref, sp_ref, sc_ref, w_pool_ref, pscale_ref,
                     g_attn_ref, g_pool_ref, w_o_ref, g_post_mix_ref, g_pre_ffn_ref,
                     g_post_ffn_ref, wup_ref, cw_ref, cb_ref, wdown_ref,
                     y_ref, po_ref, co_ref):
    n_chunks, _, tc2 = wup_ref.shape
    tc = tc2 // 2
    d_ff = n_chunks * tc
    row_w = 2 * d_ff
    u = u_ref[...]

    outs = []
    for g, w in enumerate(POOL_WINDOWS):
        c0 = g * POOL_GW
        cur = u[:, c0:c0 + POOL_GW]
        acc_u = cur
        for j in range(1, w):
            base = (POOL_STATE - j) * POOL_W + c0
            acc_u = acc_u + sp_ref[:, base:base + POOL_GW]
        outs.append(_dot((acc_u / float(w) - cur).astype(BF16), w_pool_ref[g]))
    pool = jnp.concatenate(outs, axis=1) * pscale_ref[...]
    po_ref[:, 0:(POOL_STATE - 1) * POOL_W] = sp_ref[:, POOL_W:]
    po_ref[:, (POOL_STATE - 1) * POOL_W:] = u

    x1 = _mix_residual(x_ref[...], attn_ref[...], pool, g_attn_ref[...], g_pool_ref[...],
                       w_o_ref, g_post_mix_ref[...])
    h2 = _rms(x1, g_pre_ffn_ref[...]).astype(BF16)

    co_ref[:, 0:row_w] = sc_ref[:, row_w:]
    ffn = jnp.zeros(x1.shape, F32)
    for c in range(n_chunks):
        up = _dot(h2, wup_ref[c])
        gcols = slice(c * tc, (c + 1) * tc)
        vcols = slice(d_ff + c * tc, d_ff + (c + 1) * tc)
        co_ref[:, row_w + gcols.start:row_w + gcols.stop] = up[:, :tc]
        co_ref[:, row_w + vcols.start:row_w + vcols.stop] = up[:, tc:]
        old0 = jnp.concatenate([sc_ref[:, gcols], sc_ref[:, vcols]], axis=1)
        old1 = jnp.concatenate([sc_ref[:, row_w + gcols.start:row_w + gcols.stop],
                                sc_ref[:, row_w + vcols.start:row_w + vcols.stop]], axis=1)
        cw = cw_ref[c]
        conv = cb_ref[c] + old0 * cw[0:1, :]
        conv = conv + old1 * cw[1:2, :]
        conv = conv + up * cw[2:3, :]
        ffn = ffn + _dot(_gated(conv, tc), wdown_ref[c])
    y_ref[...] = x1 + _rms(ffn, g_post_ffn_ref[...])


def _vmem():
    return pl.BlockSpec(memory_space=pltpu.VMEM)


def _smem():
    return pl.BlockSpec(memory_space=pltpu.SMEM)


def _resident(shape):
    nd = len(shape)
    return pl.BlockSpec(shape, lambda *_: (0,) * nd, pipeline_mode=pl.Buffered(1))


def _chunk_cols(a, tc):
    lead = a.shape[:-1]
    n_chunks = a.shape[-1] // (2 * tc)
    a = a.reshape(*lead, 2, n_chunks, tc)
    a = jnp.moveaxis(a, -2, 0)
    return a.reshape(n_chunks, *lead, 2 * tc)


def _unchunk_cols(a):
    n_chunks, r, tc2 = a.shape
    a = a.reshape(n_chunks, r, 2, tc2 // 2)
    return jnp.transpose(a, (1, 2, 0, 3)).reshape(r, n_chunks * tc2)


def kernel(x_prompt, x_sample, cache_k, cache_v, state_pool, state_conv, meta, w_in, b_in, sinks,
           w_pool, pool_scale, g_attn_out, g_pool_out, w_o, g_pre_mix, g_post_mix, g_pre_ffn,
           g_post_ffn, w_up, conv_w, conv_b, w_down):
    assert w_in.shape[0] == 1, "single layer"
    batch, seq, d_model = x_prompt.shape
    dec_batch = x_sample.shape[0]
    d_ff = w_down.shape[1]
    tc = FF_CHUNK
    n_chunks = d_ff // tc
    assert n_chunks * tc == d_ff and seq % SEQ_TILE == 0 and dec_batch % SAMPLE_TB == 0
    assert meta.shape[0] == N_META and N_META > POOL_STATE

    row = lambda a: a[0].reshape(1, -1)
    w_in_b = w_in[0].astype(BF16)
    b_in_r = row(b_in)
    w_pool_b = w_pool[0].astype(BF16)
    w_o_b = w_o[0].astype(BF16)
    wup_c = _chunk_cols(w_up[0].astype(BF16), tc)
    half_value = jnp.concatenate([jnp.ones((d_ff,), F32), jnp.full((d_ff,), 0.5, F32)])
    cw_c = _chunk_cols(conv_w[0] * half_value, tc)
    cb_c = _chunk_cols((conv_b[0] * half_value).reshape(1, -1), tc)
    wdown_c = w_down[0].astype(BF16).reshape(n_chunks, tc, d_model)
    sink_s = sinks[0]
    sink_col = sinks[0].reshape(N_HEADS, 1)
    slope_col = jnp.asarray(np.array(SLOPES, np.float32).reshape(N_HEADS, 1))
    gains = dict(pscale=row(pool_scale), g_attn=row(g_attn_out), g_pool=row(g_pool_out),
                 g_pre_mix=row(g_pre_mix), g_post_mix=row(g_post_mix),
                 g_pre_ffn=row(g_pre_ffn), g_post_ffn=row(g_post_ffn))

    x_meta = jnp.concatenate([jnp.zeros((WINDOW - N_META, d_model), F32), meta.astype(F32)], 0)
    k0, v0, u0, up0 = pl.pallas_call(
        _meta_kernel,
        out_shape=(jax.ShapeDtypeStruct((WINDOW, KV_W), F32),
                   jax.ShapeDtypeStruct((WINDOW, KV_W), F32),
                   jax.ShapeDtypeStruct((U_HIST, POOL_W), F32),
                   jax.ShapeDtypeStruct((n_chunks, CONV_STATE, 2 * tc), F32)),
        in_specs=[_vmem(), _vmem(), _vmem(), _smem()] + [_vmem()] * 9,
        out_specs=(_vmem(),) * 4,
        scratch_shapes=[pltpu.VMEM((U_HIST + WINDOW, POOL_W), F32)],
        compiler_params=pltpu.CompilerParams(vmem_limit_bytes=VMEM_LIMIT),
        name="meta",
    )(x_meta, w_in_b, b_in_r, sink_s, w_pool_b, gains["pscale"], gains["g_attn"],
      gains["g_pool"], w_o_b, gains["g_pre_mix"], gains["g_post_mix"], gains["g_pre_ffn"], wup_c)

    n_tiles = seq // SEQ_TILE
    per_batch = lambda shape: pl.BlockSpec((1,) + shape, lambda b, t: (b,) + (0,) * len(shape))
    prompt_inputs = (
        x_prompt, k0, v0, u0, up0, w_in_b, b_in_r, sink_s, w_pool_b, gains["pscale"],
        gains["g_attn"], gains["g_pool"], w_o_b, gains["g_pre_mix"], gains["g_post_mix"],
        gains["g_pre_ffn"], gains["g_post_ffn"], wup_c, cw_c, cb_c, wdown_c)
    in_specs = [pl.BlockSpec((1, SEQ_TILE, d_model), lambda b, t: (b, t, 0))]
    in_specs += [_smem() if a is sink_s else _resident(a.shape) for a in prompt_inputs[1:]]
    y_prompt, k_p, v_p, u_p, c_p = pl.pallas_call(
        _prompt_kernel,
        grid=(batch, n_tiles),
        out_shape=(jax.ShapeDtypeStruct((batch, seq, d_model), F32),
                   jax.ShapeDtypeStruct((batch, WINDOW, KV_W), F32),
                   jax.ShapeDtypeStruct((batch, WINDOW, KV_W), F32),
                   jax.ShapeDtypeStruct((batch, U_HIST, POOL_W), F32),
                   jax.ShapeDtypeStruct((batch, n_chunks, CONV_STATE, 2 * tc), F32)),
        in_specs=in_specs,
        out_specs=(pl.BlockSpec((1, SEQ_TILE, d_model), lambda b, t: (b, t, 0)),
                   per_batch((WINDOW, KV_W)), per_batch((WINDOW, KV_W)),
                   per_batch((U_HIST, POOL_W)), per_batch((n_chunks, CONV_STATE, 2 * tc))),
        scratch_shapes=[
            pltpu.VMEM((WINDOW + SEQ_TILE, KV_W), F32),
            pltpu.VMEM((WINDOW + SEQ_TILE, KV_W), F32),
            pltpu.VMEM((U_HIST + SEQ_TILE, POOL_W), F32),
            pltpu.VMEM((n_chunks, 2 * SUBLANES, 2 * tc), F32),
            pltpu.VMEM((2, SEQ_TILE, 2 * tc), F32),
            pltpu.VMEM((2, SEQ_TILE, tc), BF16),
            pltpu.VMEM((SEQ_TILE, d_model), F32),
            pltpu.VMEM((SEQ_TILE, Q_W), F32),
            pltpu.VMEM((d_model // LANES, SEQ_TILE, LANES), F32),
            pltpu.VMEM((d_model // LANES, SEQ_TILE, LANES), F32),
        ],
        compiler_params=pltpu.CompilerParams(
            dimension_semantics=("arbitrary", "arbitrary"), vmem_limit_bytes=VMEM_LIMIT),
        name="prompt",
    )(*prompt_inputs)

    ck = cache_k[0].reshape(dec_batch, WINDOW, KV_W)
    cv = cache_v[0].reshape(dec_batch, WINDOW, KV_W)
    xs = x_sample.reshape(dec_batch, d_model)
    tb = SAMPLE_TB
    tok = lambda w: pl.BlockSpec((tb, w), lambda i: (i, 0))
    cache_spec = pl.BlockSpec((tb, WINDOW, KV_W), lambda i: (i, 0, 0))
    attn_s, u_s, k_s, v_s = pl.pallas_call(
        _sample_a_kernel,
        grid=(dec_batch // tb,),
        out_shape=(jax.ShapeDtypeStruct((dec_batch, Q_W), F32),
                   jax.ShapeDtypeStruct((dec_batch, POOL_W), F32),
                   jax.ShapeDtypeStruct((dec_batch, WINDOW, KV_W), F32),
                   jax.ShapeDtypeStruct((dec_batch, WINDOW, KV_W), F32)),
        in_specs=[tok(d_model), cache_spec, cache_spec, _resident(w_in_b.shape),
                  _resident(b_in_r.shape), _resident(gains["g_pre_mix"].shape),
                  _resident(sink_col.shape), _resident(slope_col.shape)],
        out_specs=(tok(Q_W), tok(POOL_W), cache_spec, cache_spec),
        scratch_shapes=[pltpu.VMEM((tb, Q_W), F32), pltpu.VMEM((tb, KV_W), F32),
                        pltpu.VMEM((tb, KV_W), F32)],
        compiler_params=pltpu.CompilerParams(
            dimension_semantics=("arbitrary",), vmem_limit_bytes=VMEM_LIMIT),
        name="sample_a",
    )(xs, ck, cv, w_in_b, b_in_r, gains["g_pre_mix"], sink_col, slope_col)

    sp = state_pool[0].reshape(dec_batch, POOL_STATE * POOL_W)
    sc = state_conv[0].reshape(dec_batch, CONV_STATE * 2 * d_ff)
    y_s, pool_s, conv_s = pl.pallas_call(
        _sample_b_kernel,
        out_shape=(jax.ShapeDtypeStruct((dec_batch, d_model), F32),
                   jax.ShapeDtypeStruct(sp.shape, F32),
                   jax.ShapeDtypeStruct(sc.shape, F32)),
        in_specs=[_vmem()] * 17,
        out_specs=(_vmem(),) * 3,
        compiler_params=pltpu.CompilerParams(vmem_limit_bytes=VMEM_LIMIT),
        name="sample_b",
    )(xs, attn_s, u_s, sp, sc, w_pool_b, gains["pscale"], gains["g_attn"], gains["g_pool"],
      w_o_b, gains["g_post_mix"], gains["g_pre_ffn"], gains["g_post_ffn"], wup_c, cw_c, cb_c,
      wdown_c)

    kv_shape = (1, batch, WINDOW, N_KV_HEADS, HEAD_DIM)
    kv_s_shape = (1, dec_batch, WINDOW, N_KV_HEADS, HEAD_DIM)
    conv_p = jax.vmap(_unchunk_cols)(c_p)
    return (y_prompt,
            y_s.reshape(dec_batch, 1, d_model),
            k_p.reshape(kv_shape), v_p.reshape(kv_shape),
            u_p[:, U_HIST - POOL_STATE:, :][None],
            conv_p[None],
            k_s.reshape(kv_s_shape), v_s.reshape(kv_s_shape),
            pool_s.reshape(1, dec_batch, POOL_STATE, POOL_W),
            conv_s.reshape(1, dec_batch, CONV_STATE, 2 * d_ff))
```

```python
import numpy as np
import jax
import jax.numpy as jnp
from jax import lax
from jax.experimental import pallas as pl
from jax.experimental.pallas import tpu as pltpu

F32 = jnp.float32
BF16 = jnp.bfloat16

N_META = 16
HEAD_DIM = 64
N_HEADS = 8
N_KV_HEADS = 2
WINDOW = 128
POOL_WINDOWS = (2, 4, 8, 16)
POOL_STATE = 15
CONV_STATE = 2
RMS_EPS = 1e-6
SM_SCALE = HEAD_DIM ** -0.5
SLOPES = tuple(2.0 ** (-(h + 1) * (8.0 / N_HEADS)) for h in range(N_HEADS))

LANES = 128
SUBLANES = 8
KV_W = N_KV_HEADS * HEAD_DIM
Q_W = N_HEADS * HEAD_DIM
POOL_W = 512
POOL_GW = POOL_W // len(POOL_WINDOWS)
U_HIST = 16

SEQ_TILE = 512
FF_CHUNK = 256
SAMPLE_TB = 32
SAMPLE_UNROLL = 8
VMEM_LIMIT = 56 * 1024 * 1024
TAIL_SLABS = 4
GELU_C1 = float(np.sqrt(2.0 / np.pi))
GELU_C2 = GELU_C1 * 0.044715


def _rms(x, g):
    ms = jnp.mean(x * x, axis=-1, keepdims=True)
    return x * lax.rsqrt(ms + RMS_EPS) * g


def _dot(a, b):
    return jnp.dot(a, b, preferred_element_type=F32)


def _dot_nt(a, b):
    return lax.dot_general(a, b, (((1,), (1,)), ((), ())), preferred_element_type=F32)


def _lane_lo(shape):
    return lax.broadcasted_iota(jnp.int32, shape, len(shape) - 1) < HEAD_DIM


def _attend_block(qs, k2, v2, pos_start, sink_of):
    nq, nk = qs.shape[0], k2.shape[0]
    qi = lax.broadcasted_iota(jnp.int32, (nq, nk), 0)
    kj = lax.broadcasted_iota(jnp.int32, (nq, nk), 1)
    valid = (kj >= qi) & (kj <= qi + WINDOW) & (kj >= WINDOW - pos_start)
    neg_dist = jnp.where(valid, (kj - qi - WINDOW).astype(F32), -jnp.inf)
    lo = _lane_lo((nq, LANES))
    pieces = []
    for p in range(N_HEADS // 2):
        g = (2 * p) // (N_HEADS // N_KV_HEADS)
        blk = qs[:, p * LANES:(p + 1) * LANES]
        rolled = pltpu.roll(blk, HEAD_DIM, axis=1)
        keep = lo if g == 0 else jnp.logical_not(lo)
        outs = []
        for e in range(2):
            h = 2 * p + e
            src = blk if e == g else rolled
            qf = jnp.where(keep, src, 0.0).astype(BF16)
            s = _dot_nt(qf, k2) + SLOPES[h] * neg_dist
            sink = sink_of(h)
            m = jnp.maximum(jnp.max(s, axis=-1, keepdims=True), sink)
            pe = jnp.exp(s - m)
            den = jnp.sum(pe, axis=-1, keepdims=True) + jnp.exp(sink - m)
            outs.append(_dot(pe.astype(BF16), v2) / den)
        if g == 0:
            pieces.append(jnp.where(lo, outs[0], pltpu.roll(outs[1], HEAD_DIM, axis=1)))
        else:
            pieces.append(jnp.where(lo, pltpu.roll(outs[0], HEAD_DIM, axis=1), outs[1]))
    return jnp.concatenate(pieces, axis=1)


def _pool_rows(ubuf, rows, w_pool_ref, pool_scale, cnt_of):
    outs = []
    for g, w in enumerate(POOL_WINDOWS):
        cs = slice(g * POOL_GW, (g + 1) * POOL_GW)
        cur = ubuf[U_HIST:U_HIST + rows, cs]
        acc = cur
        for j in range(1, w):
            acc = acc + ubuf[U_HIST - j:U_HIST - j + rows, cs]
        mean = acc / cnt_of(w)
        outs.append(_dot((mean - cur).astype(BF16), w_pool_ref[g]))
    return jnp.concatenate(outs, axis=1) * pool_scale


def _mix_residual(x, attn, pool, g_attn, g_pool, w_o_ref, g_post_mix):
    mixin = jnp.concatenate([_rms(attn, g_attn), _rms(pool, g_pool)], axis=1).astype(BF16)
    return x + _rms(_dot(mixin, w_o_ref[...]), g_post_mix)


def _gated(conv, tc):
    g, hv = conv[:, :tc], conv[:, tc:]
    inner = g * (GELU_C1 + GELU_C2 * (g * g))
    return (g * (1.0 + jnp.tanh(inner)) * hv).astype(BF16)


def _ff_cols(ref, c, tc):
    d_ff = ref.shape[-1] // 2
    return jnp.concatenate([ref[:, c * tc:(c + 1) * tc],
                            ref[:, d_ff + c * tc:d_ff + (c + 1) * tc]], axis=1)


def _row_slabs(rows):
    na = rows // SUBLANES - TAIL_SLABS
    assert na % 4 == 0 and (na // 4) % 2 == 1
    slabs = [(v, na) for v in range(na)]
    slabs += [(SUBLANES * na + v, TAIL_SLABS) for v in range(TAIL_SLABS)]
    return slabs, na


def _shifted_rows(ub, prev, na):
    S = SUBLANES
    width = ub.shape[1]
    sub = lax.broadcasted_iota(jnp.int32, (S, width), 0)
    slab = lambda i: ub[i * S:(i + 1) * S, :]

    def wrap(x, y):
        return pltpu.roll(jnp.where(sub == S - 1, y, x), 1, axis=0)

    a1 = wrap(slab(na - 1), prev[S:2 * S, :])
    a2 = wrap(slab(na - 2), prev[0:S, :])
    b1 = wrap(slab(na + 3), slab(na - 1))
    b2 = wrap(slab(na + 2), slab(na - 2))
    s1 = jnp.concatenate([a1, ub[0:(na - 1) * S, :], b1, ub[na * S:(na + 3) * S, :]], axis=0)
    s2 = jnp.concatenate([a2, a1, ub[0:(na - 2) * S, :], b2, b1, ub[na * S:(na + 2) * S, :]],
                         axis=0)
    return s1, s2


def _meta_kernel(x_ref, w_in_ref, b_in_ref, sink_ref, w_pool_ref, pscale_ref, g_attn_ref,
                 g_pool_ref, w_o_ref, g_pre_mix_ref, g_post_mix_ref, g_pre_ffn_ref, wup_ref,
                 k0_ref, v0_ref, u0_ref, up0_ref, ubuf):
    rows = x_ref.shape[0]
    pos0 = N_META - rows
    x = x_ref[...]
    z = _dot(_rms(x, g_pre_mix_ref[...]).astype(BF16), w_in_ref[...]) + b_in_ref[...]
    k = z[:, Q_W:Q_W + KV_W]
    v = z[:, Q_W + KV_W:Q_W + 2 * KV_W]
    u = z[:, Q_W + 2 * KV_W:]
    pos = pos0 + lax.broadcasted_iota(jnp.int32, (rows, 1), 0)
    zeros_kv = jnp.zeros((WINDOW, KV_W), BF16)
    k2 = jnp.concatenate([zeros_kv, k.astype(BF16)], axis=0)
    v2 = jnp.concatenate([zeros_kv, v.astype(BF16)], axis=0)
    attn = _attend_block(z[:, :Q_W] * SM_SCALE, k2, v2, pos0, lambda h: sink_ref[h])

    ubuf[0:U_HIST, :] = jnp.zeros((U_HIST, POOL_W), F32)
    ubuf[U_HIST:, :] = jnp.where(pos >= 0, u, 0.0)
    cnt_of = lambda w: jnp.clip(pos + 1, 1, w).astype(F32)
    pool = _pool_rows(ubuf, rows, w_pool_ref, pscale_ref[...], cnt_of)

    x1 = _mix_residual(x, attn, pool, g_attn_ref[...], g_pool_ref[...], w_o_ref,
                       g_post_mix_ref[...])
    h2 = _rms(x1, g_pre_ffn_ref[...]).astype(BF16)
    k0_ref[...] = k
    v0_ref[...] = v
    u0_ref[...] = u[rows - U_HIST:, :]
    tail = 2 * SUBLANES
    tc = up0_ref.shape[2] // 2
    for c in range(up0_ref.shape[0]):
        up_tail = _dot(h2[rows - tail:, :], _ff_cols(wup_ref, c, tc))
        up0_ref[c] = up_tail[tail - CONV_STATE:, :]


def _prompt_kernel(x_ref, k0_ref, v0_ref, u0_ref, up0_ref, w_in_ref, b_in_ref, sink_ref,
                   w_pool_ref, pscale_ref, g_attn_ref, g_pool_ref, w_o_ref, g_pre_mix_ref,
                   g_post_mix_ref, g_pre_ffn_ref, g_post_ffn_ref, wup_ref, cw_ref, cb_ref,
                   wdown_ref,
                   y_ref, kout_ref, vout_ref, uout_ref, cout_ref,
                   kbuf, vbuf, ubuf, upst, upbuf, abuf, acc, attn_buf, xbuf, obuf):
    t = pl.program_id(1)
    rows = x_ref.shape[1]
    n_chunks, _, tc2 = upst.shape
    tc = tc2 // 2
    n_col = x_ref.shape[2] // LANES
    S = SUBLANES
    slabs, na = _row_slabs(rows)
    st2_row, st1_row = S - 1, 2 * S - 1

    @pl.when(t == 0)
    def _():
        kbuf[0:WINDOW, :] = k0_ref[...]
        vbuf[0:WINDOW, :] = v0_ref[...]
        ubuf[0:U_HIST, :] = u0_ref[...]
        upst[...] = jnp.zeros(upst.shape, F32)
        upst[:, st2_row:st2_row + 1, :] = up0_ref[:, 0:1, :]
        upst[:, st1_row:st1_row + 1, :] = up0_ref[:, 1:2, :]

    x = x_ref[0]
    z = _dot(_rms(x, g_pre_mix_ref[...]).astype(BF16), w_in_ref[...]) + b_in_ref[...]
    kbuf[WINDOW:, :] = z[:, Q_W:Q_W + KV_W]
    vbuf[WINDOW:, :] = z[:, Q_W + KV_W:Q_W + 2 * KV_W]
    ubuf[U_HIST:, :] = z[:, Q_W + 2 * KV_W:]
    qs = z[:, :Q_W] * SM_SCALE

    pos_tile = N_META + t * rows
    for i in range(rows // WINDOW):
        r0 = i * WINDOW
        k2 = kbuf[r0:r0 + 2 * WINDOW, :].astype(BF16)
        v2 = vbuf[r0:r0 + 2 * WINDOW, :].astype(BF16)
        attn_buf[r0:r0 + WINDOW, :] = _attend_block(
            qs[r0:r0 + WINDOW, :], k2, v2, pos_tile + r0, lambda h: sink_ref[h])

    pool = _pool_rows(ubuf, rows, w_pool_ref, pscale_ref[...], lambda w: float(w))
    x1 = _mix_residual(x, attn_buf[...], pool, g_attn_ref[...], g_pool_ref[...], w_o_ref,
                       g_post_mix_ref[...])

    for j in range(n_col):
        xbuf[j] = x1[:, j * LANES:(j + 1) * LANES]

    def x1_slab_order():
        return jnp.concatenate(
            [jnp.concatenate([xbuf[j, pl.ds(start, S, stride=stride), :] for j in range(n_col)],
                             axis=1) for start, stride in slabs], axis=0)

    h2 = _rms(x1_slab_order(), g_pre_ffn_ref[...]).astype(BF16)

    def up_project(c):
        upbuf[c % 2] = _dot(h2, _ff_cols(wup_ref, c, tc))

    def activate(c):
        ub = upbuf.at[c % 2]
        s1, s2 = _shifted_rows(ub, upst[c], na)
        upst[c] = ub[rows - 2 * S:rows, :]
        cw = _ff_cols(cw_ref, c, tc)
        conv = _ff_cols(cb_ref, c, tc) + s2 * cw[0:1, :]
        conv = conv + s1 * cw[1:2, :]
        conv = conv + ub[...] * cw[2:3, :]
        abuf[c % 2] = _gated(conv, tc)

    def down_project(c):
        part = _dot(abuf[c % 2], wdown_ref[c * tc:(c + 1) * tc, :])
        if c == 0:
            acc[...] = part
        else:
            acc[...] += part

    up_project(0)
    for c in range(n_chunks + 1):
        if c + 1 < n_chunks:
            up_project(c + 1)
        if c < n_chunks:
            activate(c)
        if c >= 1:
            down_project(c - 1)

    y = x1_slab_order() + _rms(acc[...], g_post_ffn_ref[...])
    for i, (start, stride) in enumerate(slabs):
        for j in range(n_col):
            obuf[j, pl.ds(start, S, stride=stride), :] = y[i * S:(i + 1) * S,
                                                          j * LANES:(j + 1) * LANES]
    y_ref[0] = jnp.concatenate([obuf[j] for j in range(n_col)], axis=1)

    kbuf[0:WINDOW, :] = kbuf[rows:rows + WINDOW, :]
    vbuf[0:WINDOW, :] = vbuf[rows:rows + WINDOW, :]
    ubuf[0:U_HIST, :] = ubuf[rows:rows + U_HIST, :]

    @pl.when(t == pl.num_programs(1) - 1)
    def _():
        kout_ref[0] = kbuf[0:WINDOW, :].T
        vout_ref[0] = vbuf[0:WINDOW, :].T
        uout_ref[0] = ubuf[0:U_HIST, :]
        cout_ref[0, :, 0:1, :] = upst[:, st2_row:st2_row + 1, :]
        cout_ref[0, :, 1:2, :] = upst[:, st1_row:st1_row + 1, :]


def _sample_a_kernel(x_ref, ck_ref, cv_ref, w_in_ref, b_in_ref, g_pre_mix_ref, sinkcol_ref,
                     slopecol_ref,
                     attn_ref, u_ref, ko_ref, vo_ref,
                     qs_buf, kn_buf, vn_buf, knt_buf, vnt_buf):
    step = pl.program_id(0)
    n_tok = ck_ref.shape[0]

    @pl.when(step == 0)
    def _():
        z = _dot(_rms(x_ref[...], g_pre_mix_ref[...]).astype(BF16), w_in_ref[...]) + b_in_ref[...]
        k = z[:, Q_W:Q_W + KV_W]
        v = z[:, Q_W + KV_W:Q_W + 2 * KV_W]
        qs_buf[...] = z[:, :Q_W] * SM_SCALE
        kn_buf[...] = k
        vn_buf[...] = v
        knt_buf[...] = k.T
        vnt_buf[...] = v.T
        u_ref[...] = z[:, Q_W + 2 * KV_W:]

    lo1 = _lane_lo((1, LANES))
    kj = lax.broadcasted_iota(jnp.int32, (N_HEADS, WINDOW), 1)
    neg_dist = (kj - WINDOW).astype(F32)
    bias = slopecol_ref[...] * neg_dist
    sink = sinkcol_ref[...]
    last_lane = lax.broadcasted_iota(jnp.int32, (KV_W, WINDOW), 1) == WINDOW - 1

    def token(j, carry):
        b = step * n_tok + j
        qrow = qs_buf[pl.ds(b, 1), :]
        kn = kn_buf[pl.ds(b, 1), :]
        vn = vn_buf[pl.ds(b, 1), :]
        kt = ck_ref[j]
        vt = cv_ref[j]
        heads = []
        for h in range(N_HEADS):
            p, e, g = h // 2, h % 2, h // (N_HEADS // N_KV_HEADS)
            blk = qrow[:, p * LANES:(p + 1) * LANES]
            src = blk if e == g else pltpu.roll(blk, HEAD_DIM, axis=1)
            keep = lo1 if g == 0 else jnp.logical_not(lo1)
            heads.append(jnp.where(keep, src, 0.0))
        qf = jnp.concatenate(heads, axis=0).astype(BF16)
        s = _dot(qf, kt.astype(BF16)) + bias
        s_self = jnp.sum(qf.astype(F32) * kn.astype(BF16).astype(F32), axis=-1, keepdims=True)
        m = jnp.maximum(jnp.maximum(jnp.max(s, axis=-1, keepdims=True), s_self), sink)
        pe = jnp.exp(s - m)
        pe_self = jnp.exp(s_self - m)
        den = jnp.sum(pe, axis=-1, keepdims=True) + pe_self + jnp.exp(sink - m)
        o = _dot_nt(pe.astype(BF16), vt.astype(BF16))
        o = o + pe_self.astype(BF16).astype(F32) * vn.astype(BF16).astype(F32)
        o = o / den
        pieces = []
        for p in range(N_HEADS // 2):
            g = (2 * p) // (N_HEADS // N_KV_HEADS)
            a, c = o[2 * p:2 * p + 1, :], o[2 * p + 1:2 * p + 2, :]
            if g == 0:
                pieces.append(jnp.where(lo1, a, pltpu.roll(c, HEAD_DIM, axis=1)))
            else:
                pieces.append(jnp.where(lo1, pltpu.roll(a, HEAD_DIM, axis=1), c))
        attn_ref[pl.ds(b, 1), :] = jnp.concatenate(pieces, axis=1)
        bring = WINDOW - 1 - b
        ko_ref[j] = jnp.where(last_lane, pltpu.roll(knt_buf[...], bring, axis=1),
                              pltpu.roll(kt, WINDOW - 1, axis=1))
        vo_ref[j] = jnp.where(last_lane, pltpu.roll(vnt_buf[...], bring, axis=1),
                              pltpu.roll(vt, WINDOW - 1, axis=1))
        return carry

    lax.fori_loop(0, n_tok, token, 0, unroll=SAMPLE_UNROLL)


def _sample_b_kernel(x_ref, attn_ref, u_ref, sp_ref, sc_ref, w_pool_ref, pscale_ref,
                     g_attn_ref, g_pool_ref, w_o_ref, g_post_mix_ref, g_pre_ffn_ref,
                     g_post_ffn_ref, wup_ref, cw_ref, cb_ref, wdown_ref,
                     y_ref, po_ref, co_ref):
    tc = FF_CHUNK
    d_ff = wdown_ref.shape[0]
    n_chunks = d_ff // tc
    u = u_ref[...]

    outs = []
    for g, w in enumerate(POOL_WINDOWS):
        cs = slice(g * POOL_GW, (g + 1) * POOL_GW)
        cur = u[:, cs]
        acc_u = cur
        for j in range(1, w):
            acc_u = acc_u + sp_ref[POOL_STATE - j, :, cs]
        outs.append(_dot((acc_u / float(w) - cur).astype(BF16), w_pool_ref[g]))
    pool = jnp.concatenate(outs, axis=1) * pscale_ref[...]
    for r in range(POOL_STATE - 1):
        po_ref[r] = sp_ref[r + 1]
    po_ref[POOL_STATE - 1] = u

    x1 = _mix_residual(x_ref[...], attn_ref[...], pool, g_attn_ref[...], g_pool_ref[...],
                       w_o_ref, g_post_mix_ref[...])
    h2 = _rms(x1, g_pre_ffn_ref[...]).astype(BF16)

    co_ref[:, 0, :] = sc_ref[:, 1, :]
    ffn = jnp.zeros(x1.shape, F32)
    for c in range(n_chunks):
        up = _dot(h2, _ff_cols(wup_ref, c, tc))
        gcols = slice(c * tc, (c + 1) * tc)
        vcols = slice(d_ff + c * tc, d_ff + (c + 1) * tc)
        co_ref[:, 1, gcols] = up[:, :tc]
        co_ref[:, 1, vcols] = up[:, tc:]
        old0 = jnp.concatenate([sc_ref[:, 0, gcols], sc_ref[:, 0, vcols]], axis=1)
        old1 = jnp.concatenate([sc_ref[:, 1, gcols], sc_ref[:, 1, vcols]], axis=1)
        cw = _ff_cols(cw_ref, c, tc)
        conv = _ff_cols(cb_ref, c, tc) + old0 * cw[0:1, :]
        conv = conv + old1 * cw[1:2, :]
        conv = conv + up * cw[2:3, :]
        ffn = ffn + _dot(_gated(conv, tc), wdown_ref[c * tc:(c + 1) * tc, :])
    y_ref[...] = x1 + _rms(ffn, g_post_ffn_ref[...])


def _vmem():
    return pl.BlockSpec(memory_space=pltpu.VMEM)


def _smem():
    return pl.BlockSpec(memory_space=pltpu.SMEM)


def _resident(shape):
    nd = len(shape)
    return pl.BlockSpec(shape, lambda *_: (0,) * nd, pipeline_mode=pl.Buffered(1))


def _unchunk_cols(a):
    n_chunks, r, tc2 = a.shape
    a = a.reshape(n_chunks, r, 2, tc2 // 2)
    return jnp.transpose(a, (1, 2, 0, 3)).reshape(r, n_chunks * tc2)


def kernel(x_prompt, x_sample, cache_k, cache_v, state_pool, state_conv, meta, w_in, b_in, sinks,
           w_pool, pool_scale, g_attn_out, g_pool_out, w_o, g_pre_mix, g_post_mix, g_pre_ffn,
           g_post_ffn, w_up, conv_w, conv_b, w_down):
    assert w_in.shape[0] == 1, "single layer"
    batch, seq, d_model = x_prompt.shape
    dec_batch = x_sample.shape[0]
    d_ff = w_down.shape[1]
    tc = FF_CHUNK
    n_chunks = d_ff // tc
    assert n_chunks * tc == d_ff and seq % SEQ_TILE == 0 and dec_batch % SAMPLE_TB == 0
    assert meta.shape[0] == N_META and N_META > POOL_STATE

    row = lambda a: a[0].reshape(1, -1)
    w_in_b = w_in[0].astype(BF16)
    b_in_r = row(b_in)
    w_pool_b = w_pool[0].astype(BF16)
    w_o_b = w_o[0].astype(BF16)
    wup_c = w_up[0].astype(BF16)
    half_value = jnp.concatenate([jnp.ones((d_ff,), F32), jnp.full((d_ff,), 0.5, F32)])
    cw_c = conv_w[0] * half_value
    cb_c = (conv_b[0] * half_value).reshape(1, -1)
    wdown_c = w_down[0].astype(BF16)
    sink_s = sinks[0]
    sink_col = sinks[0].reshape(N_HEADS, 1)
    slope_col = jnp.asarray(np.array(SLOPES, np.float32).reshape(N_HEADS, 1))
    gains = dict(pscale=row(pool_scale), g_attn=row(g_attn_out), g_pool=row(g_pool_out),
                 g_pre_mix=row(g_pre_mix), g_post_mix=row(g_post_mix),
                 g_pre_ffn=row(g_pre_ffn), g_post_ffn=row(g_post_ffn))

    x_meta = jnp.concatenate([jnp.zeros((WINDOW - N_META, d_model), F32), meta.astype(F32)], 0)
    k0, v0, u0, up0 = pl.pallas_call(
        _meta_kernel,
        out_shape=(jax.ShapeDtypeStruct((WINDOW, KV_W), F32),
                   jax.ShapeDtypeStruct((WINDOW, KV_W), F32),
                   jax.ShapeDtypeStruct((U_HIST, POOL_W), F32),
                   jax.ShapeDtypeStruct((n_chunks, CONV_STATE, 2 * tc), F32)),
        in_specs=[_vmem(), _vmem(), _vmem(), _smem()] + [_vmem()] * 9,
        out_specs=(_vmem(),) * 4,
        scratch_shapes=[pltpu.VMEM((U_HIST + WINDOW, POOL_W), F32)],
        compiler_params=pltpu.CompilerParams(vmem_limit_bytes=VMEM_LIMIT),
        name="meta",
    )(x_meta, w_in_b, b_in_r, sink_s, w_pool_b, gains["pscale"], gains["g_attn"],
      gains["g_pool"], w_o_b, gains["g_pre_mix"], gains["g_post_mix"], gains["g_pre_ffn"], wup_c)

    n_tiles = seq // SEQ_TILE
    per_batch = lambda shape: pl.BlockSpec((1,) + shape, lambda b, t: (b,) + (0,) * len(shape))
    prompt_inputs = (
        x_prompt, k0, v0, u0, up0, w_in_b, b_in_r, sink_s, w_pool_b, gains["pscale"],
        gains["g_attn"], gains["g_pool"], w_o_b, gains["g_pre_mix"], gains["g_post_mix"],
        gains["g_pre_ffn"], gains["g_post_ffn"], wup_c, cw_c, cb_c, wdown_c)
    in_specs = [pl.BlockSpec((1, SEQ_TILE, d_model), lambda b, t: (b, t, 0))]
    in_specs += [_smem() if a is sink_s else _resident(a.shape) for a in prompt_inputs[1:]]
    y_prompt, k_p, v_p, u_p, c_p = pl.pallas_call(
        _prompt_kernel,
        grid=(batch, n_tiles),
        out_shape=(jax.ShapeDtypeStruct((batch, seq, d_model), F32),
                   jax.ShapeDtypeStruct((batch, WINDOW, KV_W), F32),
                   jax.ShapeDtypeStruct((batch, WINDOW, KV_W), F32),
                   jax.ShapeDtypeStruct((batch, U_HIST, POOL_W), F32),
                   jax.ShapeDtypeStruct((batch, n_chunks, CONV_STATE, 2 * tc), F32)),
        in_specs=in_specs,
        out_specs=(pl.BlockSpec((1, SEQ_TILE, d_model), lambda b, t: (b, t, 0)),
                   per_batch((WINDOW, KV_W)), per_batch((WINDOW, KV_W)),
                   per_batch((U_HIST, POOL_W)), per_batch((n_chunks, CONV_STATE, 2 * tc))),
        scratch_shapes=[
            pltpu.VMEM((WINDOW + SEQ_TILE, KV_W), F32),
            pltpu.VMEM((WINDOW + SEQ_TILE, KV_W), F32),
            pltpu.VMEM((U_HIST + SEQ_TILE, POOL_W), F32),
            pltpu.VMEM((n_chunks, 2 * SUBLANES, 2 * tc), F32),
            pltpu.VMEM((2, SEQ_TILE, 2 * tc), F32),
            pltpu.VMEM((2, SEQ_TILE, tc), BF16),
            pltpu.VMEM((SEQ_TILE, d_model), F32),
            pltpu.VMEM((SEQ_TILE, Q_W), F32),
            pltpu.VMEM((d_model // LANES, SEQ_TILE, LANES), F32),
            pltpu.VMEM((d_model // LANES, SEQ_TILE, LANES), F32),
        ],
        compiler_params=pltpu.CompilerParams(
            dimension_semantics=("arbitrary", "arbitrary"), vmem_limit_bytes=VMEM_LIMIT),
        name="prompt",
    )(*prompt_inputs)

    feat_pos = lambda c: jnp.transpose(c[0].reshape(dec_batch, WINDOW, KV_W), (0, 2, 1))
    ck, cv = feat_pos(cache_k), feat_pos(cache_v)
    xs = x_sample.reshape(dec_batch, d_model)
    tb = SAMPLE_TB
    whole = lambda shape: pl.BlockSpec(shape, lambda i: (0,) * len(shape))
    cache_spec = pl.BlockSpec((tb, KV_W, WINDOW), lambda i: (i, 0, 0))
    attn_s, u_s, k_s, v_s = pl.pallas_call(
        _sample_a_kernel,
        grid=(dec_batch // tb,),
        out_shape=(jax.ShapeDtypeStruct((dec_batch, Q_W), F32),
                   jax.ShapeDtypeStruct((dec_batch, POOL_W), F32),
                   jax.ShapeDtypeStruct((dec_batch, KV_W, WINDOW), F32),
                   jax.ShapeDtypeStruct((dec_batch, KV_W, WINDOW), F32)),
        in_specs=[whole(xs.shape), cache_spec, cache_spec, _resident(w_in_b.shape),
                  _resident(b_in_r.shape), _resident(gains["g_pre_mix"].shape),
                  _resident(sink_col.shape), _resident(slope_col.shape)],
        out_specs=(whole((dec_batch, Q_W)), whole((dec_batch, POOL_W)), cache_spec, cache_spec),
        scratch_shapes=[pltpu.VMEM((dec_batch, Q_W), F32), pltpu.VMEM((dec_batch, KV_W), F32),
                        pltpu.VMEM((dec_batch, KV_W), F32), pltpu.VMEM((KV_W, dec_batch), F32),
                        pltpu.VMEM((KV_W, dec_batch), F32)],
        compiler_params=pltpu.CompilerParams(
            dimension_semantics=("arbitrary",), vmem_limit_bytes=VMEM_LIMIT),
        name="sample_a",
    )(xs, ck, cv, w_in_b, b_in_r, gains["g_pre_mix"], sink_col, slope_col)

    sp = jnp.transpose(state_pool[0], (1, 0, 2))
    sc = state_conv[0]
    y_s, pool_s, conv_s = pl.pallas_call(
        _sample_b_kernel,
        out_shape=(jax.ShapeDtypeStruct((dec_batch, d_model), F32),
                   jax.ShapeDtypeStruct(sp.shape, F32),
                   jax.ShapeDtypeStruct(sc.shape, F32)),
        in_specs=[_vmem()] * 17,
        out_specs=(_vmem(),) * 3,
        compiler_params=pltpu.CompilerParams(vmem_limit_bytes=VMEM_LIMIT),
        name="sample_b",
    )(xs, attn_s, u_s, sp, sc, w_pool_b, gains["pscale"], gains["g_attn"], gains["g_pool"],
      w_o_b, gains["g_post_mix"], gains["g_pre_ffn"], gains["g_post_ffn"], wup_c, cw_c, cb_c,
      wdown_c)

    def pos_feat(c):
        n = c.shape[0]
        return jnp.transpose(c, (0, 2, 1)).reshape(1, n, WINDOW, N_KV_HEADS, HEAD_DIM)

    conv_p = jax.vmap(_unchunk_cols)(c_p)
    return (y_prompt,
            y_s.reshape(dec_batch, 1, d_model),
            pos_feat(k_p), pos_feat(v_p),
            u_p[:, U_HIST - POOL_STATE:, :][None],
            conv_p[None],
            pos_feat(k_s), pos_feat(v_s),
            jnp.transpose(pool_s, (1, 0, 2))[None],
            conv_s[None])
```

```python
import functools

import numpy as np
import jax
import jax.numpy as jnp
from jax import lax
from jax.experimental import pallas as pl
from jax.experimental.pallas import tpu as pltpu

F32 = jnp.float32
BF16 = jnp.bfloat16

N_META = 16
HEAD_DIM = 64
N_HEADS = 8
N_KV_HEADS = 2
WINDOW = 128
POOL_WINDOWS = (2, 4, 8, 16)
POOL_STATE = 15
CONV_STATE = 2
RMS_EPS = 1e-6
SM_SCALE = HEAD_DIM ** -0.5
SLOPES = tuple(2.0 ** (-(h + 1) * (8.0 / N_HEADS)) for h in range(N_HEADS))

LANES = 128
SUBLANES = 8
KV_W = N_KV_HEADS * HEAD_DIM
Q_W = N_HEADS * HEAD_DIM
POOL_W = 512
POOL_GW = POOL_W // len(POOL_WINDOWS)
U_HIST = 16

SEQ_TILE = 512
FF_CHUNK = 256
SAMPLE_TB = 32
SAMPLE_UNROLL = 8
VMEM_LIMIT = 56 * 1024 * 1024
TAIL_SLABS = 4
GELU_C1 = float(np.sqrt(2.0 / np.pi))
GELU_C2 = GELU_C1 * 0.044715


def _rms(x, g):
    ms = jnp.mean(x * x, axis=-1, keepdims=True)
    return x * lax.rsqrt(ms + RMS_EPS) * g


def _dot(a, b):
    return jnp.dot(a, b, preferred_element_type=F32)


def _dot_nt(a, b):
    return lax.dot_general(a, b, (((1,), (1,)), ((), ())), preferred_element_type=F32)


def _lane_lo(shape):
    return lax.broadcasted_iota(jnp.int32, shape, len(shape) - 1) < HEAD_DIM


def _attend_block(qs, k2, v2, pos_start, sink_of):
    nq, nk = qs.shape[0], k2.shape[0]
    qi = lax.broadcasted_iota(jnp.int32, (nq, nk), 0)
    kj = lax.broadcasted_iota(jnp.int32, (nq, nk), 1)
    valid = (kj >= qi) & (kj <= qi + WINDOW) & (kj >= WINDOW - pos_start)
    neg_dist = jnp.where(valid, (kj - qi - WINDOW).astype(F32), -jnp.inf)
    lo = _lane_lo((nq, LANES))
    group = N_HEADS // N_KV_HEADS
    q_heads = []
    for p in range(N_HEADS // 2):
        g = (2 * p) // group
        blk = qs[:, p * LANES:(p + 1) * LANES]
        rolled = pltpu.roll(blk, HEAD_DIM, axis=1)
        keep = lo if g == 0 else jnp.logical_not(lo)
        for e in range(2):
            q_heads.append(jnp.where(keep, blk if e == g else rolled, 0.0).astype(BF16))
    s_all = _dot_nt(jnp.concatenate(q_heads, axis=0), k2)
    probs, dens = [], []
    for h in range(N_HEADS):
        s = s_all[h * nq:(h + 1) * nq, :] + SLOPES[h] * neg_dist
        sink = sink_of(h)
        m = jnp.maximum(jnp.max(s, axis=-1, keepdims=True), sink)
        pe = jnp.exp(s - m)
        dens.append(jnp.sum(pe, axis=-1, keepdims=True) + jnp.exp(sink - m))
        probs.append(pe.astype(BF16))
    o_all = _dot(jnp.concatenate(probs, axis=0), v2)
    outs = [o_all[h * nq:(h + 1) * nq, :] / dens[h] for h in range(N_HEADS)]
    pieces = []
    for p in range(N_HEADS // 2):
        a, b = outs[2 * p], outs[2 * p + 1]
        if (2 * p) // group == 0:
            pieces.append(jnp.where(lo, a, pltpu.roll(b, HEAD_DIM, axis=1)))
        else:
            pieces.append(jnp.where(lo, pltpu.roll(a, HEAD_DIM, axis=1), b))
    return jnp.concatenate(pieces, axis=1)


def _pool_rows(ubuf, rows, w_pool_ref, pool_scale, cnt_of):
    outs = []
    for g, w in enumerate(POOL_WINDOWS):
        cs = slice(g * POOL_GW, (g + 1) * POOL_GW)
        cur = ubuf[U_HIST:U_HIST + rows, cs]
        acc = cur
        for j in range(1, w):
            acc = acc + ubuf[U_HIST - j:U_HIST - j + rows, cs]
        mean = acc / cnt_of(w)
        outs.append(_dot((mean - cur).astype(BF16), w_pool_ref[g]))
    return jnp.concatenate(outs, axis=1) * pool_scale


def _mix_residual(x, attn, pool, g_attn, g_pool, w_o_ref, g_post_mix):
    mixin = jnp.concatenate([_rms(attn, g_attn), _rms(pool, g_pool)], axis=1).astype(BF16)
    return x + _rms(_dot(mixin, w_o_ref[...]), g_post_mix)


def _gated(conv, tc):
    g, hv = conv[:, :tc], conv[:, tc:]
    inner = g * (GELU_C1 + GELU_C2 * (g * g))
    return (g * (1.0 + jnp.tanh(inner)) * hv).astype(BF16)


def _ff_cols(ref, c, tc):
    d_ff = ref.shape[-1] // 2
    return jnp.concatenate([ref[:, c * tc:(c + 1) * tc],
                            ref[:, d_ff + c * tc:d_ff + (c + 1) * tc]], axis=1)


def _row_slabs(rows):
    na = rows // SUBLANES - TAIL_SLABS
    assert na % 4 == 0 and (na // 4) % 2 == 1
    slabs = [(v, na) for v in range(na)]
    slabs += [(SUBLANES * na + v, TAIL_SLABS) for v in range(TAIL_SLABS)]
    return slabs, na


def _shifted_rows(ub, prev, na):
    S = SUBLANES
    width = ub.shape[1]
    sub = lax.broadcasted_iota(jnp.int32, (S, width), 0)
    slab = lambda i: ub[i * S:(i + 1) * S, :]

    def wrap(x, y):
        return pltpu.roll(jnp.where(sub == S - 1, y, x), 1, axis=0)

    a1 = wrap(slab(na - 1), prev[S:2 * S, :])
    a2 = wrap(slab(na - 2), prev[0:S, :])
    b1 = wrap(slab(na + 3), slab(na - 1))
    b2 = wrap(slab(na + 2), slab(na - 2))
    s1 = jnp.concatenate([a1, ub[0:(na - 1) * S, :], b1, ub[na * S:(na + 3) * S, :]], axis=0)
    s2 = jnp.concatenate([a2, a1, ub[0:(na - 2) * S, :], b2, b1, ub[na * S:(na + 2) * S, :]],
                         axis=0)
    return s1, s2


def _meta_kernel(x_ref, w_in_ref, b_in_ref, sink_ref, w_pool_ref, pscale_ref, g_attn_ref,
                 g_pool_ref, w_o_ref, g_pre_mix_ref, g_post_mix_ref, g_pre_ffn_ref, wup_ref,
                 k0_ref, v0_ref, u0_ref, up0_ref, ubuf):
    rows = x_ref.shape[0]
    pos0 = N_META - rows
    x = x_ref[...]
    z = _dot(_rms(x, g_pre_mix_ref[...]).astype(BF16), w_in_ref[...]) + b_in_ref[...]
    k = z[:, Q_W:Q_W + KV_W]
    v = z[:, Q_W + KV_W:Q_W + 2 * KV_W]
    u = z[:, Q_W + 2 * KV_W:]
    pos = pos0 + lax.broadcasted_iota(jnp.int32, (rows, 1), 0)
    zeros_kv = jnp.zeros((WINDOW, KV_W), BF16)
    k2 = jnp.concatenate([zeros_kv, k.astype(BF16)], axis=0)
    v2 = jnp.concatenate([zeros_kv, v.astype(BF16)], axis=0)
    attn = _attend_block(z[:, :Q_W] * SM_SCALE, k2, v2, pos0, lambda h: sink_ref[h])

    ubuf[0:U_HIST, :] = jnp.zeros((U_HIST, POOL_W), F32)
    ubuf[U_HIST:, :] = jnp.where(pos >= 0, u, 0.0)
    cnt_of = lambda w: jnp.clip(pos + 1, 1, w).astype(F32)
    pool = _pool_rows(ubuf, rows, w_pool_ref, pscale_ref[...], cnt_of)

    x1 = _mix_residual(x, attn, pool, g_attn_ref[...], g_pool_ref[...], w_o_ref,
                       g_post_mix_ref[...])
    h2 = _rms(x1, g_pre_ffn_ref[...]).astype(BF16)
    k0_ref[...] = k
    v0_ref[...] = v
    u0_ref[...] = u[rows - U_HIST:, :]
    tail = 2 * SUBLANES
    tc = up0_ref.shape[2] // 2
    for c in range(up0_ref.shape[0]):
        up_tail = _dot(h2[rows - tail:, :], _ff_cols(wup_ref, c, tc))
        up0_ref[c] = up_tail[tail - CONV_STATE:, :]


def _prompt_kernel(n_tiles, x_ref, k0_ref, v0_ref, u0_ref, up0_ref, w_in_ref, b_in_ref,
                   sink_ref, w_pool_ref, pscale_ref, g_attn_ref, g_pool_ref, w_o_ref,
                   g_pre_mix_ref, g_post_mix_ref, g_pre_ffn_ref, g_post_ffn_ref, wup_ref, cw_ref,
                   cb_ref, wdown_ref,
                   y_ref, kout_ref, vout_ref, uout_ref, cout_ref,
                   kbuf, vbuf, ubuf, upst, upbuf, abuf, acc, attn_buf, xbuf, obuf, h2buf, qsbuf):
    g = pl.program_id(0)
    n_total = pl.num_programs(0) - 1
    rows = x_ref.shape[1]
    n_chunks, _, tc2 = upst.shape
    tc = tc2 // 2
    n_col = x_ref.shape[2] // LANES
    S = SUBLANES
    slabs, na = _row_slabs(rows)
    st2_row, st1_row = S - 1, 2 * S - 1

    tf = lax.rem(jnp.minimum(g, n_total - 1), n_tiles)
    tj = lax.rem(jnp.maximum(g - 1, 0), n_tiles)
    wslot = lax.rem(g, 2)
    rslot = 1 - wslot

    @pl.when(tf == 0)
    def _():
        kbuf[0:WINDOW, :] = k0_ref[...]
        vbuf[0:WINDOW, :] = v0_ref[...]
        ubuf[0:U_HIST, :] = u0_ref[...]

    @pl.when(tj == 0)
    def _():
        upst[...] = jnp.zeros(upst.shape, F32)
        upst[:, st2_row:st2_row + 1, :] = up0_ref[:, 0:1, :]
        upst[:, st1_row:st1_row + 1, :] = up0_ref[:, 1:2, :]

    @pl.when(g == 0)
    def _():
        xbuf[1] = jnp.zeros(xbuf.shape[1:], F32)

    def mixer_in():
        z = _dot(_rms(x_ref[0], g_pre_mix_ref[...]).astype(BF16), w_in_ref[...]) + b_in_ref[...]
        kbuf[WINDOW:, :] = z[:, Q_W:Q_W + KV_W]
        vbuf[WINDOW:, :] = z[:, Q_W + KV_W:Q_W + 2 * KV_W]
        ubuf[U_HIST:, :] = z[:, Q_W + 2 * KV_W:]
        qsbuf[...] = z[:, :Q_W] * SM_SCALE

    def mixer_attend(i):
        r0 = i * WINDOW
        k2 = kbuf[r0:r0 + 2 * WINDOW, :].astype(BF16)
        v2 = vbuf[r0:r0 + 2 * WINDOW, :].astype(BF16)
        attn_buf[r0:r0 + WINDOW, :] = _attend_block(
            qsbuf[r0:r0 + WINDOW, :], k2, v2, N_META + tf * rows + r0, lambda h: sink_ref[h])

    def mixer_out():
        pool = _pool_rows(ubuf, rows, w_pool_ref, pscale_ref[...], lambda w: float(w))
        x1 = _mix_residual(x_ref[0], attn_buf[...], pool, g_attn_ref[...], g_pool_ref[...],
                           w_o_ref, g_post_mix_ref[...])
        for j in range(n_col):
            xbuf[wslot, j] = x1[:, j * LANES:(j + 1) * LANES]
        kbuf[0:WINDOW, :] = kbuf[rows:rows + WINDOW, :]
        vbuf[0:WINDOW, :] = vbuf[rows:rows + WINDOW, :]
        ubuf[0:U_HIST, :] = ubuf[rows:rows + U_HIST, :]

    def x1_slab_order():
        return jnp.concatenate(
            [jnp.concatenate([xbuf[rslot, j, pl.ds(start, S, stride=stride), :]
                              for j in range(n_col)], axis=1) for start, stride in slabs], axis=0)

    def ffn_in():
        h2buf[...] = _rms(x1_slab_order(), g_pre_ffn_ref[...]).astype(BF16)

    def up_project(c):
        upbuf[c % 2] = _dot(h2buf[...], _ff_cols(wup_ref, c, tc))

    def activate(c):
        ub = upbuf.at[c % 2]
        s1, s2 = _shifted_rows(ub, upst[c], na)
        upst[c] = ub[rows - 2 * S:rows, :]
        cw = _ff_cols(cw_ref, c, tc)
        conv = _ff_cols(cb_ref, c, tc) + s2 * cw[0:1, :]
        conv = conv + s1 * cw[1:2, :]
        conv = conv + ub[...] * cw[2:3, :]
        abuf[c % 2] = _gated(conv, tc)

    def down_project(c):
        part = _dot(abuf[c % 2], wdown_ref[c * tc:(c + 1) * tc, :])
        if c == 0:
            acc[...] = part
        else:
            acc[...] += part

    def ffn_stage(c):
        if c + 1 < n_chunks:
            up_project(c + 1)
        if c < n_chunks:
            activate(c)
        if c >= 1:
            down_project(c - 1)

    def ffn_out():
        y = x1_slab_order() + _rms(acc[...], g_post_ffn_ref[...])
        for i, (start, stride) in enumerate(slabs):
            for j in range(n_col):
                obuf[j, pl.ds(start, S, stride=stride), :] = y[i * S:(i + 1) * S,
                                                              j * LANES:(j + 1) * LANES]
        y_ref[0] = jnp.concatenate([obuf[j] for j in range(n_col)], axis=1)

    ffn_stages = [lambda c=c: ffn_stage(c) for c in range(n_chunks + 1)]
    mixer_stages = [mixer_in] + [lambda i=i: mixer_attend(i) for i in range(rows // WINDOW)]
    mixer_stages.append(mixer_out)

    ffn_in()
    up_project(0)
    emitted = 0
    for c, stage in enumerate(ffn_stages):
        due = ((c + 1) * len(mixer_stages)) // len(ffn_stages)
        while emitted < due:
            mixer_stages[emitted]()
            emitted += 1
        stage()
    ffn_out()

    @pl.when((tf == n_tiles - 1) & (g < n_total))
    def _():
        kout_ref[0] = kbuf[0:WINDOW, :].T
        vout_ref[0] = vbuf[0:WINDOW, :].T
        uout_ref[0] = ubuf[0:U_HIST, :]

    @pl.when((tj == n_tiles - 1) & (g > 0))
    def _():
        cout_ref[0, :, 0:1, :] = upst[:, st2_row:st2_row + 1, :]
        cout_ref[0, :, 1:2, :] = upst[:, st1_row:st1_row + 1, :]


def _sample_a_kernel(x_ref, ck_ref, cv_ref, w_in_ref, b_in_ref, g_pre_mix_ref, sinkcol_ref,
                     slopecol_ref,
                     attn_ref, u_ref, ko_ref, vo_ref,
                     qs_buf, kn_buf, vn_buf, knt_buf, vnt_buf):
    step = pl.program_id(0)
    n_tok = ck_ref.shape[0]

    @pl.when(step == 0)
    def _():
        z = _dot(_rms(x_ref[...], g_pre_mix_ref[...]).astype(BF16), w_in_ref[...]) + b_in_ref[...]
        k = z[:, Q_W:Q_W + KV_W]
        v = z[:, Q_W + KV_W:Q_W + 2 * KV_W]
        qs_buf[...] = z[:, :Q_W] * SM_SCALE
        kn_buf[...] = k
        vn_buf[...] = v
        knt_buf[...] = k.T
        vnt_buf[...] = v.T
        u_ref[...] = z[:, Q_W + 2 * KV_W:]

    lo1 = _lane_lo((1, LANES))
    kj = lax.broadcasted_iota(jnp.int32, (N_HEADS, WINDOW), 1)
    neg_dist = (kj - WINDOW).astype(F32)
    bias = slopecol_ref[...] * neg_dist
    sink = sinkcol_ref[...]
    last_lane = lax.broadcasted_iota(jnp.int32, (KV_W, WINDOW), 1) == WINDOW - 1

    def token(j, carry):
        b = step * n_tok + j
        qrow = qs_buf[pl.ds(b, 1), :]
        kn = kn_buf[pl.ds(b, 1), :]
        vn = vn_buf[pl.ds(b, 1), :]
        kt = ck_ref[j]
        vt = cv_ref[j]
        heads = []
        for h in range(N_HEADS):
            p, e, g = h // 2, h % 2, h // (N_HEADS // N_KV_HEADS)
            blk = qrow[:, p * LANES:(p + 1) * LANES]
            src = blk if e == g else pltpu.roll(blk, HEAD_DIM, axis=1)
            keep = lo1 if g == 0 else jnp.logical_not(lo1)
            heads.append(jnp.where(keep, src, 0.0))
        qf = jnp.concatenate(heads, axis=0).astype(BF16)
        s = _dot(qf, kt.astype(BF16)) + bias
        s_self = jnp.sum(qf.astype(F32) * kn.astype(BF16).astype(F32), axis=-1, keepdims=True)
        m = jnp.maximum(jnp.maximum(jnp.max(s, axis=-1, keepdims=True), s_self), sink)
        pe = jnp.exp(s - m)
        pe_self = jnp.exp(s_self - m)
        den = jnp.sum(pe, axis=-1, keepdims=True) + pe_self + jnp.exp(sink - m)
        o = _dot_nt(pe.astype(BF16), vt.astype(BF16))
        o = o + pe_self.astype(BF16).astype(F32) * vn.astype(BF16).astype(F32)
        o = o / den
        pieces = []
        for p in range(N_HEADS // 2):
            g = (2 * p) // (N_HEADS // N_KV_HEADS)
            a, c = o[2 * p:2 * p + 1, :], o[2 * p + 1:2 * p + 2, :]
            if g == 0:
                pieces.append(jnp.where(lo1, a, pltpu.roll(c, HEAD_DIM, axis=1)))
            else:
                pieces.append(jnp.where(lo1, pltpu.roll(a, HEAD_DIM, axis=1), c))
        attn_ref[pl.ds(b, 1), :] = jnp.concatenate(pieces, axis=1)
        bring = WINDOW - 1 - b
        ko_ref[j] = jnp.where(last_lane, pltpu.roll(knt_buf[...], bring, axis=1),
                              pltpu.roll(kt, WINDOW - 1, axis=1))
        vo_ref[j] = jnp.where(last_lane, pltpu.roll(vnt_buf[...], bring, axis=1),
                              pltpu.roll(vt, WINDOW - 1, axis=1))
        return carry

    lax.fori_loop(0, n_tok, token, 0, unroll=SAMPLE_UNROLL)


def _sample_b_kernel(x_ref, attn_ref, u_ref, sp_ref, sc_ref, w_pool_ref, pscale_ref,
                     g_attn_ref, g_pool_ref, w_o_ref, g_post_mix_ref, g_pre_ffn_ref,
                     g_post_ffn_ref, wup_ref, cw_ref, cb_ref, wdown_ref,
                     y_ref, po_ref, co_ref):
    tc = FF_CHUNK
    d_ff = wdown_ref.shape[0]
    n_chunks = d_ff // tc
    u = u_ref[...]

    outs = []
    for g, w in enumerate(POOL_WINDOWS):
        cs = slice(g * POOL_GW, (g + 1) * POOL_GW)
        cur = u[:, cs]
        acc_u = cur
        for j in range(1, w):
            acc_u = acc_u + sp_ref[POOL_STATE - j, :, cs]
        outs.append(_dot((acc_u / float(w) - cur).astype(BF16), w_pool_ref[g]))
    pool = jnp.concatenate(outs, axis=1) * pscale_ref[...]
    for r in range(POOL_STATE - 1):
        po_ref[r] = sp_ref[r + 1]
    po_ref[POOL_STATE - 1] = u

    x1 = _mix_residual(x_ref[...], attn_ref[...], pool, g_attn_ref[...], g_pool_ref[...],
                       w_o_ref, g_post_mix_ref[...])
    h2 = _rms(x1, g_pre_ffn_ref[...]).astype(BF16)

    co_ref[:, 0, :] = sc_ref[:, 1, :]
    ffn = jnp.zeros(x1.shape, F32)
    for c in range(n_chunks):
        up = _dot(h2, _ff_cols(wup_ref, c, tc))
        gcols = slice(c * tc, (c + 1) * tc)
        vcols = slice(d_ff + c * tc, d_ff + (c + 1) * tc)
        co_ref[:, 1, gcols] = up[:, :tc]
        co_ref[:, 1, vcols] = up[:, tc:]
        old0 = jnp.concatenate([sc_ref[:, 0, gcols], sc_ref[:, 0, vcols]], axis=1)
        old1 = jnp.concatenate([sc_ref[:, 1, gcols], sc_ref[:, 1, vcols]], axis=1)
        cw = _ff_cols(cw_ref, c, tc)
        conv = _ff_cols(cb_ref, c, tc) + old0 * cw[0:1, :]
        conv = conv + old1 * cw[1:2, :]
        conv = conv + up * cw[2:3, :]
        ffn = ffn + _dot(_gated(conv, tc), wdown_ref[c * tc:(c + 1) * tc, :])
    y_ref[...] = x1 + _rms(ffn, g_post_ffn_ref[...])


def _vmem():
    return pl.BlockSpec(memory_space=pltpu.VMEM)


def _smem():
    return pl.BlockSpec(memory_space=pltpu.SMEM)


def _resident(shape):
    nd = len(shape)
    return pl.BlockSpec(shape, lambda *_: (0,) * nd, pipeline_mode=pl.Buffered(1))


def _unchunk_cols(a):
    n_chunks, r, tc2 = a.shape
    a = a.reshape(n_chunks, r, 2, tc2 // 2)
    return jnp.transpose(a, (1, 2, 0, 3)).reshape(r, n_chunks * tc2)


def kernel(x_prompt, x_sample, cache_k, cache_v, state_pool, state_conv, meta, w_in, b_in, sinks,
           w_pool, pool_scale, g_attn_out, g_pool_out, w_o, g_pre_mix, g_post_mix, g_pre_ffn,
           g_post_ffn, w_up, conv_w, conv_b, w_down):
    assert w_in.shape[0] == 1, "single layer"
    batch, seq, d_model = x_prompt.shape
    dec_batch = x_sample.shape[0]
    d_ff = w_down.shape[1]
    tc = FF_CHUNK
    n_chunks = d_ff // tc
    assert n_chunks * tc == d_ff and seq % SEQ_TILE == 0 and dec_batch % SAMPLE_TB == 0
    assert meta.shape[0] == N_META and N_META > POOL_STATE

    row = lambda a: a[0].reshape(1, -1)
    w_in_b = w_in[0].astype(BF16)
    b_in_r = row(b_in)
    w_pool_b = w_pool[0].astype(BF16)
    w_o_b = w_o[0].astype(BF16)
    wup_c = w_up[0].astype(BF16)
    half_value = jnp.concatenate([jnp.ones((d_ff,), F32), jnp.full((d_ff,), 0.5, F32)])
    cw_c = conv_w[0] * half_value
    cb_c = (conv_b[0] * half_value).reshape(1, -1)
    wdown_c = w_down[0].astype(BF16)
    sink_s = sinks[0]
    sink_col = sinks[0].reshape(N_HEADS, 1)
    slope_col = jnp.asarray(np.array(SLOPES, np.float32).reshape(N_HEADS, 1))
    gains = dict(pscale=row(pool_scale), g_attn=row(g_attn_out), g_pool=row(g_pool_out),
                 g_pre_mix=row(g_pre_mix), g_post_mix=row(g_post_mix),
                 g_pre_ffn=row(g_pre_ffn), g_post_ffn=row(g_post_ffn))

    x_meta = jnp.concatenate([jnp.zeros((WINDOW - N_META, d_model), F32), meta.astype(F32)], 0)
    k0, v0, u0, up0 = pl.pallas_call(
        _meta_kernel,
        out_shape=(jax.ShapeDtypeStruct((WINDOW, KV_W), F32),
                   jax.ShapeDtypeStruct((WINDOW, KV_W), F32),
                   jax.ShapeDtypeStruct((U_HIST, POOL_W), F32),
                   jax.ShapeDtypeStruct((n_chunks, CONV_STATE, 2 * tc), F32)),
        in_specs=[_vmem(), _vmem(), _vmem(), _smem()] + [_vmem()] * 9,
        out_specs=(_vmem(),) * 4,
        scratch_shapes=[pltpu.VMEM((U_HIST + WINDOW, POOL_W), F32)],
        compiler_params=pltpu.CompilerParams(vmem_limit_bytes=VMEM_LIMIT),
        name="meta",
    )(x_meta, w_in_b, b_in_r, sink_s, w_pool_b, gains["pscale"], gains["g_attn"],
      gains["g_pool"], w_o_b, gains["g_pre_mix"], gains["g_post_mix"], gains["g_pre_ffn"], wup_c)

    n_tiles = seq // SEQ_TILE
    n_total = batch * n_tiles
    mixer_tile = lambda g: jnp.minimum(g, n_total - 1)
    ffn_tile = lambda g: jnp.maximum(g - 1, 0)
    per_batch = lambda shape, tile: pl.BlockSpec(
        (1,) + shape, lambda g: (tile(g) // n_tiles,) + (0,) * len(shape))
    prompt_inputs = (
        x_prompt, k0, v0, u0, up0, w_in_b, b_in_r, sink_s, w_pool_b, gains["pscale"],
        gains["g_attn"], gains["g_pool"], w_o_b, gains["g_pre_mix"], gains["g_post_mix"],
        gains["g_pre_ffn"], gains["g_post_ffn"], wup_c, cw_c, cb_c, wdown_c)
    in_specs = [pl.BlockSpec((1, SEQ_TILE, d_model),
                             lambda g: (mixer_tile(g) // n_tiles, mixer_tile(g) % n_tiles, 0))]
    in_specs += [_smem() if a is sink_s else _resident(a.shape) for a in prompt_inputs[1:]]
    y_prompt, k_p, v_p, u_p, c_p = pl.pallas_call(
        functools.partial(_prompt_kernel, n_tiles),
        grid=(n_total + 1,),
        out_shape=(jax.ShapeDtypeStruct((batch, seq, d_model), F32),
                   jax.ShapeDtypeStruct((batch, WINDOW, KV_W), F32),
                   jax.ShapeDtypeStruct((batch, WINDOW, KV_W), F32),
                   jax.ShapeDtypeStruct((batch, U_HIST, POOL_W), F32),
                   jax.ShapeDtypeStruct((batch, n_chunks, CONV_STATE, 2 * tc), F32)),
        in_specs=in_specs,
        out_specs=(pl.BlockSpec((1, SEQ_TILE, d_model),
                                lambda g: (ffn_tile(g) // n_tiles, ffn_tile(g) % n_tiles, 0)),
                   per_batch((KV_W, WINDOW), mixer_tile), per_batch((KV_W, WINDOW), mixer_tile),
                   per_batch((U_HIST, POOL_W), mixer_tile),
                   per_batch((n_chunks, CONV_STATE, 2 * tc), ffn_tile)),
        scratch_shapes=[
            pltpu.VMEM((WINDOW + SEQ_TILE, KV_W), F32),
            pltpu.VMEM((WINDOW + SEQ_TILE, KV_W), F32),
            pltpu.VMEM((U_HIST + SEQ_TILE, POOL_W), F32),
            pltpu.VMEM((n_chunks, 2 * SUBLANES, 2 * tc), F32),
            pltpu.VMEM((2, SEQ_TILE, 2 * tc), F32),
            pltpu.VMEM((2, SEQ_TILE, tc), BF16),
            pltpu.VMEM((SEQ_TILE, d_model), F32),
            pltpu.VMEM((SEQ_TILE, Q_W), F32),
            pltpu.VMEM((2, d_model // LANES, SEQ_TILE, LANES), F32),
            pltpu.VMEM((d_model // LANES, SEQ_TILE, LANES), F32),
            pltpu.VMEM((SEQ_TILE, d_model), BF16),
            pltpu.VMEM((SEQ_TILE, Q_W), F32),
        ],
        compiler_params=pltpu.CompilerParams(
            dimension_semantics=("arbitrary",), vmem_limit_bytes=VMEM_LIMIT),
        name="prompt",
    )(*prompt_inputs)

    feat_pos = lambda c: jnp.transpose(c[0].reshape(dec_batch, WINDOW, KV_W), (0, 2, 1))
    ck, cv = feat_pos(cache_k), feat_pos(cache_v)
    xs = x_sample.reshape(dec_batch, d_model)
    tb = SAMPLE_TB
    whole = lambda shape: pl.BlockSpec(shape, lambda i: (0,) * len(shape))
    cache_spec = pl.BlockSpec((tb, KV_W, WINDOW), lambda i: (i, 0, 0))
    attn_s, u_s, k_s, v_s = pl.pallas_call(
        _sample_a_kernel,
        grid=(dec_batch // tb,),
        out_shape=(jax.ShapeDtypeStruct((dec_batch, Q_W), F32),
                   jax.ShapeDtypeStruct((dec_batch, POOL_W), F32),
                   jax.ShapeDtypeStruct((dec_batch, KV_W, WINDOW), F32),
                   jax.ShapeDtypeStruct((dec_batch, KV_W, WINDOW), F32)),
        in_specs=[whole(xs.shape), cache_spec, cache_spec, _resident(w_in_b.shape),
                  _resident(b_in_r.shape), _resident(gains["g_pre_mix"].shape),
                  _resident(sink_col.shape), _resident(slope_col.shape)],
        out_specs=(whole((dec_batch, Q_W)), whole((dec_batch, POOL_W)), cache_spec, cache_spec),
        scratch_shapes=[pltpu.VMEM((dec_batch, Q_W), F32), pltpu.VMEM((dec_batch, KV_W), F32),
                        pltpu.VMEM((dec_batch, KV_W), F32), pltpu.VMEM((KV_W, dec_batch), F32),
                        pltpu.VMEM((KV_W, dec_batch), F32)],
        compiler_params=pltpu.CompilerParams(
            dimension_semantics=("arbitrary",), vmem_limit_bytes=VMEM_LIMIT),
        name="sample_a",
    )(xs, ck, cv, w_in_b, b_in_r, gains["g_pre_mix"], sink_col, slope_col)

    sp = jnp.transpose(state_pool[0], (1, 0, 2))
    sc = state_conv[0]
    y_s, pool_s, conv_s = pl.pallas_call(
        _sample_b_kernel,
        out_shape=(jax.ShapeDtypeStruct((dec_batch, d_model), F32),
                   jax.ShapeDtypeStruct(sp.shape, F32),
                   jax.ShapeDtypeStruct(sc.shape, F32)),
        in_specs=[_vmem()] * 17,
        out_specs=(_vmem(),) * 3,
        compiler_params=pltpu.CompilerParams(vmem_limit_bytes=VMEM_LIMIT),
        name="sample_b",
    )(xs, attn_s, u_s, sp, sc, w_pool_b, gains["pscale"], gains["g_attn"], gains["g_pool"],
      w_o_b, gains["g_post_mix"], gains["g_pre_ffn"], gains["g_post_ffn"], wup_c, cw_c, cb_c,
      wdown_c)

    def pos_feat(c):
        n = c.shape[0]
        return jnp.transpose(c, (0, 2, 1)).reshape(1, n, WINDOW, N_KV_HEADS, HEAD_DIM)

    conv_p = jax.vmap(_unchunk_cols)(c_p)
    return (y_prompt,
            y_s.reshape(dec_batch, 1, d_model),
            pos_feat(k_p), pos_feat(v_p),
            u_p[:, U_HIST - POOL_STATE:, :][None],
            conv_p[None],
            pos_feat(k_s), pos_feat(v_s),
            jnp.transpose(pool_s, (1, 0, 2))[None],
            conv_s[None])
```

```python
import functools

import numpy as np
import jax
import jax.numpy as jnp
from jax import lax
from jax.experimental import pallas as pl
from jax.experimental.pallas import tpu as pltpu

F32 = jnp.float32
BF16 = jnp.bfloat16

N_META = 16
HEAD_DIM = 64
N_HEADS = 8
N_KV_HEADS = 2
WINDOW = 128
POOL_WINDOWS = (2, 4, 8, 16)
POOL_STATE = 15
CONV_STATE = 2
RMS_EPS = 1e-6
SM_SCALE = HEAD_DIM ** -0.5
SLOPES = tuple(2.0 ** (-(h + 1) * (8.0 / N_HEADS)) for h in range(N_HEADS))

LANES = 128
SUBLANES = 8
KV_W = N_KV_HEADS * HEAD_DIM
Q_W = N_HEADS * HEAD_DIM
POOL_W = 512
POOL_GW = POOL_W // len(POOL_WINDOWS)
U_HIST = 16

SEQ_TILE = 512
FF_CHUNK = 256
SAMPLE_TB = 32
SAMPLE_UNROLL = 8
VMEM_LIMIT = 56 * 1024 * 1024
TAIL_SLABS = 4
GELU_C1 = float(np.sqrt(2.0 / np.pi))
GELU_C2 = GELU_C1 * 0.044715


def _rms(x, g):
    ms = jnp.mean(x * x, axis=-1, keepdims=True)
    return x * lax.rsqrt(ms + RMS_EPS) * g


def _dot(a, b):
    return jnp.dot(a, b, preferred_element_type=F32)


def _dot_nt(a, b):
    return lax.dot_general(a, b, (((1,), (1,)), ((), ())), preferred_element_type=F32)


def _lane_lo(shape):
    return lax.broadcasted_iota(jnp.int32, shape, len(shape) - 1) < HEAD_DIM


def _attend_block(qs, k2, v2, pos_start, sink_of):
    nq, nk = qs.shape[0], k2.shape[0]
    qi = lax.broadcasted_iota(jnp.int32, (nq, nk), 0)
    kj = lax.broadcasted_iota(jnp.int32, (nq, nk), 1)
    valid = (kj >= qi) & (kj <= qi + WINDOW) & (kj >= WINDOW - pos_start)
    neg_dist = jnp.where(valid, (kj - qi - WINDOW).astype(F32), -jnp.inf)
    lo = _lane_lo((nq, LANES))
    group = N_HEADS // N_KV_HEADS
    q_heads = []
    for p in range(N_HEADS // 2):
        g = (2 * p) // group
        blk = qs[:, p * LANES:(p + 1) * LANES]
        rolled = pltpu.roll(blk, HEAD_DIM, axis=1)
        keep = lo if g == 0 else jnp.logical_not(lo)
        for e in range(2):
            q_heads.append(jnp.where(keep, blk if e == g else rolled, 0.0).astype(BF16))
    s_all = _dot_nt(jnp.concatenate(q_heads, axis=0), k2)
    probs, dens = [], []
    for h in range(N_HEADS):
        s = s_all[h * nq:(h + 1) * nq, :] + SLOPES[h] * neg_dist
        sink = sink_of(h)
        m = jnp.maximum(jnp.max(s, axis=-1, keepdims=True), sink)
        pe = jnp.exp(s - m)
        dens.append(jnp.sum(pe, axis=-1, keepdims=True) + jnp.exp(sink - m))
        probs.append(pe.astype(BF16))
    o_all = _dot(jnp.concatenate(probs, axis=0), v2)
    outs = [o_all[h * nq:(h + 1) * nq, :] / dens[h] for h in range(N_HEADS)]
    pieces = []
    for p in range(N_HEADS // 2):
        a, b = outs[2 * p], outs[2 * p + 1]
        if (2 * p) // group == 0:
            pieces.append(jnp.where(lo, a, pltpu.roll(b, HEAD_DIM, axis=1)))
        else:
            pieces.append(jnp.where(lo, pltpu.roll(a, HEAD_DIM, axis=1), b))
    return jnp.concatenate(pieces, axis=1)


def _pool_rows(ubuf, rows, w_pool_ref, pool_scale, cnt_of):
    outs = []
    for g, w in enumerate(POOL_WINDOWS):
        cs = slice(g * POOL_GW, (g + 1) * POOL_GW)
        cur = ubuf[U_HIST:U_HIST + rows, cs]
        acc = cur
        for j in range(1, w):
            acc = acc + ubuf[U_HIST - j:U_HIST - j + rows, cs]
        mean = acc / cnt_of(w)
        outs.append(_dot((mean - cur).astype(BF16), w_pool_ref[g]))
    return jnp.concatenate(outs, axis=1) * pool_scale


def _pool_rows_full(ubuf, rows, w_pool_ref, pool_scale):
    outs = []
    for g, w in enumerate(POOL_WINDOWS):
        h = ubuf[:, g * POOL_GW:(g + 1) * POOL_GW]
        s, k = h, 1
        while k < w:
            if k < SUBLANES:
                s = s + pltpu.roll(s, k, axis=0)
            else:
                s = s + jnp.concatenate([s[:k, :], s[:-k, :]], axis=0)
            k *= 2
        cur = h[U_HIST:, :]
        outs.append(_dot((s[U_HIST:, :] * (1.0 / w) - cur).astype(BF16), w_pool_ref[g]))
    return jnp.concatenate(outs, axis=1) * pool_scale


def _mix_residual(x, attn, pool, g_attn, g_pool, w_o_ref, g_post_mix):
    mixin = jnp.concatenate([_rms(attn, g_attn), _rms(pool, g_pool)], axis=1).astype(BF16)
    return x + _rms(_dot(mixin, w_o_ref[...]), g_post_mix)


def _gated(conv, tc):
    g, hv = conv[:, :tc], conv[:, tc:]
    inner = g * (GELU_C1 + GELU_C2 * (g * g))
    return (g * (1.0 + jnp.tanh(inner)) * hv).astype(BF16)


def _ff_cols(ref, c, tc):
    d_ff = ref.shape[-1] // 2
    return jnp.concatenate([ref[:, c * tc:(c + 1) * tc],
                            ref[:, d_ff + c * tc:d_ff + (c + 1) * tc]], axis=1)


def _row_slabs(rows):
    na = rows // SUBLANES - TAIL_SLABS
    assert na % 4 == 0 and (na // 4) % 2 == 1
    slabs = [(v, na) for v in range(na)]
    slabs += [(SUBLANES * na + v, TAIL_SLABS) for v in range(TAIL_SLABS)]
    return slabs, na


def _shifted_rows(ub, prev, na):
    S = SUBLANES
    width = ub.shape[1]
    sub = lax.broadcasted_iota(jnp.int32, (S, width), 0)
    slab = lambda i: ub[i * S:(i + 1) * S, :]

    def wrap(x, y):
        return pltpu.roll(jnp.where(sub == S - 1, y, x), 1, axis=0)

    a1 = wrap(slab(na - 1), prev[S:2 * S, :])
    a2 = wrap(slab(na - 2), prev[0:S, :])
    b1 = wrap(slab(na + 3), slab(na - 1))
    b2 = wrap(slab(na + 2), slab(na - 2))
    s1 = jnp.concatenate([a1, ub[0:(na - 1) * S, :], b1, ub[na * S:(na + 3) * S, :]], axis=0)
    s2 = jnp.concatenate([a2, a1, ub[0:(na - 2) * S, :], b2, b1, ub[na * S:(na + 2) * S, :]],
                         axis=0)
    return s1, s2


def _meta_kernel(x_ref, w_in_ref, b_in_ref, sink_ref, w_pool_ref, pscale_ref, g_attn_ref,
                 g_pool_ref, w_o_ref, g_pre_mix_ref, g_post_mix_ref, g_pre_ffn_ref, wup_ref,
                 k0_ref, v0_ref, u0_ref, up0_ref, ubuf):
    rows = x_ref.shape[0]
    pos0 = N_META - rows
    x = x_ref[...]
    z = _dot(_rms(x, g_pre_mix_ref[...]).astype(BF16), w_in_ref[...]) + b_in_ref[...]
    k = z[:, Q_W:Q_W + KV_W]
    v = z[:, Q_W + KV_W:Q_W + 2 * KV_W]
    u = z[:, Q_W + 2 * KV_W:]
    pos = pos0 + lax.broadcasted_iota(jnp.int32, (rows, 1), 0)
    zeros_kv = jnp.zeros((WINDOW, KV_W), BF16)
    k2 = jnp.concatenate([zeros_kv, k.astype(BF16)], axis=0)
    v2 = jnp.concatenate([zeros_kv, v.astype(BF16)], axis=0)
    attn = _attend_block(z[:, :Q_W] * SM_SCALE, k2, v2, pos0, lambda h: sink_ref[h])

    ubuf[0:U_HIST, :] = jnp.zeros((U_HIST, POOL_W), F32)
    ubuf[U_HIST:, :] = jnp.where(pos >= 0, u, 0.0)
    cnt_of = lambda w: jnp.clip(pos + 1, 1, w).astype(F32)
    pool = _pool_rows(ubuf, rows, w_pool_ref, pscale_ref[...], cnt_of)

    x1 = _mix_residual(x, attn, pool, g_attn_ref[...], g_pool_ref[...], w_o_ref,
                       g_post_mix_ref[...])
    h2 = _rms(x1, g_pre_ffn_ref[...]).astype(BF16)
    k0_ref[...] = k
    v0_ref[...] = v
    u0_ref[...] = u[rows - U_HIST:, :]
    tail = 2 * SUBLANES
    tc = up0_ref.shape[2] // 2
    for c in range(up0_ref.shape[0]):
        up_tail = _dot(h2[rows - tail:, :], _ff_cols(wup_ref, c, tc))
        up0_ref[c] = up_tail[tail - CONV_STATE:, :]


def _prompt_kernel(n_tiles, x_ref, k0_ref, v0_ref, u0_ref, up0_ref, w_in_ref, b_in_ref,
                   sink_ref, w_pool_ref, pscale_ref, g_attn_ref, g_pool_ref, w_o_ref,
                   g_pre_mix_ref, g_post_mix_ref, g_pre_ffn_ref, g_post_ffn_ref, wup_ref, cw_ref,
                   cb_ref, wdown_ref,
                   y_ref, kout_ref, vout_ref, uout_ref, cout_ref,
                   kbuf, vbuf, ubuf, upst, upbuf, abuf, acc, attn_buf, xbuf, obuf, h2buf, qsbuf):
    g = pl.program_id(0)
    n_total = pl.num_programs(0) - 1
    rows = x_ref.shape[1]
    n_chunks, _, tc2 = upst.shape
    tc = tc2 // 2
    n_col = x_ref.shape[2] // LANES
    S = SUBLANES
    slabs, na = _row_slabs(rows)
    st2_row, st1_row = S - 1, 2 * S - 1

    tf = lax.rem(jnp.minimum(g, n_total - 1), n_tiles)
    tj = lax.rem(jnp.maximum(g - 1, 0), n_tiles)
    wslot = lax.rem(g, 2)
    rslot = 1 - wslot

    @pl.when(tf == 0)
    def _():
        kbuf[0:WINDOW, :] = k0_ref[...]
        vbuf[0:WINDOW, :] = v0_ref[...]
        ubuf[0:U_HIST, :] = u0_ref[...]

    @pl.when(tj == 0)
    def _():
        upst[...] = jnp.zeros(upst.shape, F32)
        upst[:, st2_row:st2_row + 1, :] = up0_ref[:, 0:1, :]
        upst[:, st1_row:st1_row + 1, :] = up0_ref[:, 1:2, :]

    @pl.when(g == 0)
    def _():
        xbuf[1] = jnp.zeros(xbuf.shape[1:], F32)

    def mixer_in():
        z = _dot(_rms(x_ref[0], g_pre_mix_ref[...]).astype(BF16), w_in_ref[...]) + b_in_ref[...]
        kbuf[WINDOW:, :] = z[:, Q_W:Q_W + KV_W]
        vbuf[WINDOW:, :] = z[:, Q_W + KV_W:Q_W + 2 * KV_W]
        ubuf[U_HIST:, :] = z[:, Q_W + 2 * KV_W:]
        qsbuf[...] = z[:, :Q_W] * SM_SCALE

    def mixer_attend(i):
        r0 = i * WINDOW
        k2 = kbuf[r0:r0 + 2 * WINDOW, :].astype(BF16)
        v2 = vbuf[r0:r0 + 2 * WINDOW, :].astype(BF16)
        attn_buf[r0:r0 + WINDOW, :] = _attend_block(
            qsbuf[r0:r0 + WINDOW, :], k2, v2, N_META + tf * rows + r0, lambda h: sink_ref[h])

    mixer_vals = {}

    def mixer_pool():
        mixer_vals["pool"] = _pool_rows_full(ubuf, rows, w_pool_ref, pscale_ref[...])

    def mixer_out():
        x1 = _mix_residual(x_ref[0], attn_buf[...], mixer_vals["pool"], g_attn_ref[...],
                           g_pool_ref[...], w_o_ref, g_post_mix_ref[...])
        for j in range(n_col):
            xbuf[wslot, j] = x1[:, j * LANES:(j + 1) * LANES]
        kbuf[0:WINDOW, :] = kbuf[rows:rows + WINDOW, :]
        vbuf[0:WINDOW, :] = vbuf[rows:rows + WINDOW, :]
        ubuf[0:U_HIST, :] = ubuf[rows:rows + U_HIST, :]

    def x1_slab_order():
        return jnp.concatenate(
            [jnp.concatenate([xbuf[rslot, j, pl.ds(start, S, stride=stride), :]
                              for j in range(n_col)], axis=1) for start, stride in slabs], axis=0)

    def ffn_in():
        h2buf[...] = _rms(x1_slab_order(), g_pre_ffn_ref[...]).astype(BF16)

    def up_project(c):
        upbuf[c % 2] = _dot(h2buf[...], _ff_cols(wup_ref, c, tc))

    def activate(c):
        ub = upbuf.at[c % 2]
        s1, s2 = _shifted_rows(ub, upst[c], na)
        upst[c] = ub[rows - 2 * S:rows, :]
        cw = _ff_cols(cw_ref, c, tc)
        conv = _ff_cols(cb_ref, c, tc) + s2 * cw[0:1, :]
        conv = conv + s1 * cw[1:2, :]
        conv = conv + ub[...] * cw[2:3, :]
        abuf[(c // 2) % 2, :, (c % 2) * tc:(c % 2 + 1) * tc] = _gated(conv, tc)

    def down_project(p):
        c0 = 2 * p
        kw = (min(c0 + 2, n_chunks) - c0) * tc
        part = _dot(abuf[p % 2, :, 0:kw], wdown_ref[c0 * tc:c0 * tc + kw, :])
        if p == 0:
            acc[...] = part
        elif p < n_pairs - 1:
            acc[...] += part
        else:
            ffn_vals["last"] = part

    n_pairs = (n_chunks + 1) // 2
    ffn_vals = {}
    down_due = {min(2 * p + 1, n_chunks - 1) + 1: p for p in range(n_pairs)}

    def ffn_stage(c):
        if c + 1 < n_chunks:
            up_project(c + 1)
        if c < n_chunks:
            activate(c)
        if c in down_due:
            down_project(down_due[c])

    def ffn_out():
        y = x1_slab_order() + _rms(acc[...] + ffn_vals["last"], g_post_ffn_ref[...])
        for i, (start, stride) in enumerate(slabs):
            for j in range(n_col):
                obuf[j, pl.ds(start, S, stride=stride), :] = y[i * S:(i + 1) * S,
                                                              j * LANES:(j + 1) * LANES]
        y_ref[0] = jnp.concatenate([obuf[j] for j in range(n_col)], axis=1)

    n_blocks = rows // WINDOW
    spread = [lambda i=i: mixer_attend(i) for i in range(n_blocks)]
    spread.insert(n_blocks // 2, mixer_pool)
    mixer_in()
    ffn_in()
    up_project(0)
    emitted = 0
    for c in range(n_chunks + 1):
        due = ((c + 1) * len(spread)) // (n_chunks + 1)
        while emitted < due:
            spread[emitted]()
            emitted += 1
        ffn_stage(c)
    ffn_out()
    mixer_out()

    @pl.when((tf == n_tiles - 1) & (g < n_total))
    def _():
        kout_ref[0] = kbuf[0:WINDOW, :].T
        vout_ref[0] = vbuf[0:WINDOW, :].T
        uout_ref[0] = ubuf[0:U_HIST, :]

    @pl.when((tj == n_tiles - 1) & (g > 0))
    def _():
        cout_ref[0, :, 0:1, :] = upst[:, st2_row:st2_row + 1, :]
        cout_ref[0, :, 1:2, :] = upst[:, st1_row:st1_row + 1, :]


def _sample_a_kernel(x_ref, ck_ref, cv_ref, w_in_ref, b_in_ref, g_pre_mix_ref, sinkcol_ref,
                     slopecol_ref,
                     attn_ref, u_ref, ko_ref, vo_ref,
                     qs_buf, kn_buf, vn_buf, knt_buf, vnt_buf):
    step = pl.program_id(0)
    n_tok = ck_ref.shape[0]

    @pl.when(step == 0)
    def _():
        z = _dot(_rms(x_ref[...], g_pre_mix_ref[...]).astype(BF16), w_in_ref[...]) + b_in_ref[...]
        k = z[:, Q_W:Q_W + KV_W]
        v = z[:, Q_W + KV_W:Q_W + 2 * KV_W]
        qs_buf[...] = z[:, :Q_W] * SM_SCALE
        kn_buf[...] = k
        vn_buf[...] = v
        knt_buf[...] = k.T
        vnt_buf[...] = v.T
        u_ref[...] = z[:, Q_W + 2 * KV_W:]

    lo1 = _lane_lo((1, LANES))
    kj = lax.broadcasted_iota(jnp.int32, (N_HEADS, WINDOW), 1)
    neg_dist = (kj - WINDOW).astype(F32)
    bias = slopecol_ref[...] * neg_dist
    sink = sinkcol_ref[...]
    last_lane = lax.broadcasted_iota(jnp.int32, (KV_W, WINDOW), 1) == WINDOW - 1

    def token(j, carry):
        b = step * n_tok + j
        qrow = qs_buf[pl.ds(b, 1), :]
        kn = kn_buf[pl.ds(b, 1), :]
        vn = vn_buf[pl.ds(b, 1), :]
        kt = ck_ref[j]
        vt = cv_ref[j]
        heads = []
        for h in range(N_HEADS):
            p, e, g = h // 2, h % 2, h // (N_HEADS // N_KV_HEADS)
            blk = qrow[:, p * LANES:(p + 1) * LANES]
            src = blk if e == g else pltpu.roll(blk, HEAD_DIM, axis=1)
            keep = lo1 if g == 0 else jnp.logical_not(lo1)
            heads.append(jnp.where(keep, src, 0.0))
        qf = jnp.concatenate(heads, axis=0).astype(BF16)
        s = _dot(qf, kt.astype(BF16)) + bias
        s_self = jnp.sum(qf.astype(F32) * kn.astype(BF16).astype(F32), axis=-1, keepdims=True)
        m = jnp.maximum(jnp.maximum(jnp.max(s, axis=-1, keepdims=True), s_self), sink)
        pe = jnp.exp(s - m)
        pe_self = jnp.exp(s_self - m)
        den = jnp.sum(pe, axis=-1, keepdims=True) + pe_self + jnp.exp(sink - m)
        o = _dot_nt(pe.astype(BF16), vt.astype(BF16))
        o = o + pe_self.astype(BF16).astype(F32) * vn.astype(BF16).astype(F32)
        o = o / den
        pieces = []
        for p in range(N_HEADS // 2):
            g = (2 * p) // (N_HEADS // N_KV_HEADS)
            a, c = o[2 * p:2 * p + 1, :], o[2 * p + 1:2 * p + 2, :]
            if g == 0:
                pieces.append(jnp.where(lo1, a, pltpu.roll(c, HEAD_DIM, axis=1)))
            else:
                pieces.append(jnp.where(lo1, pltpu.roll(a, HEAD_DIM, axis=1), c))
        attn_ref[pl.ds(b, 1), :] = jnp.concatenate(pieces, axis=1)
        bring = WINDOW - 1 - b
        ko_ref[j] = jnp.where(last_lane, pltpu.roll(knt_buf[...], bring, axis=1),
                              pltpu.roll(kt, WINDOW - 1, axis=1))
        vo_ref[j] = jnp.where(last_lane, pltpu.roll(vnt_buf[...], bring, axis=1),
                              pltpu.roll(vt, WINDOW - 1, axis=1))
        return carry

    lax.fori_loop(0, n_tok, token, 0, unroll=SAMPLE_UNROLL)


def _sample_b_kernel(x_ref, attn_ref, u_ref, sp_ref, sc_ref, w_pool_ref, pscale_ref,
                     g_attn_ref, g_pool_ref, w_o_ref, g_post_mix_ref, g_pre_ffn_ref,
                     g_post_ffn_ref, wup_ref, cw_ref, cb_ref, wdown_ref,
                     y_ref, po_ref, co_ref):
    tc = FF_CHUNK
    d_ff = wdown_ref.shape[0]
    n_chunks = d_ff // tc
    u = u_ref[...]

    outs = []
    for g, w in enumerate(POOL_WINDOWS):
        cs = slice(g * POOL_GW, (g + 1) * POOL_GW)
        cur = u[:, cs]
        acc_u = cur
        for j in range(1, w):
            acc_u = acc_u + sp_ref[POOL_STATE - j, :, cs]
        outs.append(_dot((acc_u / float(w) - cur).astype(BF16), w_pool_ref[g]))
    pool = jnp.concatenate(outs, axis=1) * pscale_ref[...]
    for r in range(POOL_STATE - 1):
        po_ref[r] = sp_ref[r + 1]
    po_ref[POOL_STATE - 1] = u

    x1 = _mix_residual(x_ref[...], attn_ref[...], pool, g_attn_ref[...], g_pool_ref[...],
                       w_o_ref, g_post_mix_ref[...])
    h2 = _rms(x1, g_pre_ffn_ref[...]).astype(BF16)

    co_ref[:, 0, :] = sc_ref[:, 1, :]
    ffn = jnp.zeros(x1.shape, F32)
    for c in range(n_chunks):
        up = _dot(h2, _ff_cols(wup_ref, c, tc))
        gcols = slice(c * tc, (c + 1) * tc)
        vcols = slice(d_ff + c * tc, d_ff + (c + 1) * tc)
        co_ref[:, 1, gcols] = up[:, :tc]
        co_ref[:, 1, vcols] = up[:, tc:]
        old0 = jnp.concatenate([sc_ref[:, 0, gcols], sc_ref[:, 0, vcols]], axis=1)
        old1 = jnp.concatenate([sc_ref[:, 1, gcols], sc_ref[:, 1, vcols]], axis=1)
        cw = _ff_cols(cw_ref, c, tc)
        conv = _ff_cols(cb_ref, c, tc) + old0 * cw[0:1, :]
        conv = conv + old1 * cw[1:2, :]
        conv = conv + up * cw[2:3, :]
        ffn = ffn + _dot(_gated(conv, tc), wdown_ref[c * tc:(c + 1) * tc, :])
    y_ref[...] = x1 + _rms(ffn, g_post_ffn_ref[...])


def _vmem():
    return pl.BlockSpec(memory_space=pltpu.VMEM)


def _smem():
    return pl.BlockSpec(memory_space=pltpu.SMEM)


def _resident(shape):
    nd = len(shape)
    return pl.BlockSpec(shape, lambda *_: (0,) * nd, pipeline_mode=pl.Buffered(1))


def _unchunk_cols(a):
    n_chunks, r, tc2 = a.shape
    a = a.reshape(n_chunks, r, 2, tc2 // 2)
    return jnp.transpose(a, (1, 2, 0, 3)).reshape(r, n_chunks * tc2)


def kernel(x_prompt, x_sample, cache_k, cache_v, state_pool, state_conv, meta, w_in, b_in, sinks,
           w_pool, pool_scale, g_attn_out, g_pool_out, w_o, g_pre_mix, g_post_mix, g_pre_ffn,
           g_post_ffn, w_up, conv_w, conv_b, w_down):
    assert w_in.shape[0] == 1, "single layer"
    batch, seq, d_model = x_prompt.shape
    dec_batch = x_sample.shape[0]
    d_ff = w_down.shape[1]
    tc = FF_CHUNK
    n_chunks = d_ff // tc
    assert n_chunks * tc == d_ff and seq % SEQ_TILE == 0 and dec_batch % SAMPLE_TB == 0
    assert meta.shape[0] == N_META and N_META > POOL_STATE

    row = lambda a: a[0].reshape(1, -1)
    w_in_b = w_in[0].astype(BF16)
    b_in_r = row(b_in)
    w_pool_b = w_pool[0].astype(BF16)
    w_o_b = w_o[0].astype(BF16)
    wup_c = w_up[0].astype(BF16)
    half_value = jnp.concatenate([jnp.ones((d_ff,), F32), jnp.full((d_ff,), 0.5, F32)])
    cw_c = conv_w[0] * half_value
    cb_c = (conv_b[0] * half_value).reshape(1, -1)
    wdown_c = w_down[0].astype(BF16)
    sink_s = sinks[0]
    sink_col = sinks[0].reshape(N_HEADS, 1)
    slope_col = jnp.asarray(np.array(SLOPES, np.float32).reshape(N_HEADS, 1))
    gains = dict(pscale=row(pool_scale), g_attn=row(g_attn_out), g_pool=row(g_pool_out),
                 g_pre_mix=row(g_pre_mix), g_post_mix=row(g_post_mix),
                 g_pre_ffn=row(g_pre_ffn), g_post_ffn=row(g_post_ffn))

    x_meta = jnp.concatenate([jnp.zeros((WINDOW - N_META, d_model), F32), meta.astype(F32)], 0)
    k0, v0, u0, up0 = pl.pallas_call(
        _meta_kernel,
        out_shape=(jax.ShapeDtypeStruct((WINDOW, KV_W), F32),
                   jax.ShapeDtypeStruct((WINDOW, KV_W), F32),
                   jax.ShapeDtypeStruct((U_HIST, POOL_W), F32),
                   jax.ShapeDtypeStruct((n_chunks, CONV_STATE, 2 * tc), F32)),
        in_specs=[_vmem(), _vmem(), _vmem(), _smem()] + [_vmem()] * 9,
        out_specs=(_vmem(),) * 4,
        scratch_shapes=[pltpu.VMEM((U_HIST + WINDOW, POOL_W), F32)],
        compiler_params=pltpu.CompilerParams(vmem_limit_bytes=VMEM_LIMIT),
        name="meta",
    )(x_meta, w_in_b, b_in_r, sink_s, w_pool_b, gains["pscale"], gains["g_attn"],
      gains["g_pool"], w_o_b, gains["g_pre_mix"], gains["g_post_mix"], gains["g_pre_ffn"], wup_c)

    n_tiles = seq // SEQ_TILE
    n_total = batch * n_tiles
    mixer_tile = lambda g: jnp.minimum(g, n_total - 1)
    ffn_tile = lambda g: jnp.maximum(g - 1, 0)
    per_batch = lambda shape, tile: pl.BlockSpec(
        (1,) + shape, lambda g: (tile(g) // n_tiles,) + (0,) * len(shape))
    prompt_inputs = (
        x_prompt, k0, v0, u0, up0, w_in_b, b_in_r, sink_s, w_pool_b, gains["pscale"],
        gains["g_attn"], gains["g_pool"], w_o_b, gains["g_pre_mix"], gains["g_post_mix"],
        gains["g_pre_ffn"], gains["g_post_ffn"], wup_c, cw_c, cb_c, wdown_c)
    in_specs = [pl.BlockSpec((1, SEQ_TILE, d_model),
                             lambda g: (mixer_tile(g) // n_tiles, mixer_tile(g) % n_tiles, 0))]
    in_specs += [_smem() if a is sink_s else _resident(a.shape) for a in prompt_inputs[1:]]
    y_prompt, k_p, v_p, u_p, c_p = pl.pallas_call(
        functools.partial(_prompt_kernel, n_tiles),
        grid=(n_total + 1,),
        out_shape=(jax.ShapeDtypeStruct((batch, seq, d_model), F32),
                   jax.ShapeDtypeStruct((batch, WINDOW, KV_W), F32),
                   jax.ShapeDtypeStruct((batch, WINDOW, KV_W), F32),
                   jax.ShapeDtypeStruct((batch, U_HIST, POOL_W), F32),
                   jax.ShapeDtypeStruct((batch, n_chunks, CONV_STATE, 2 * tc), F32)),
        in_specs=in_specs,
        out_specs=(pl.BlockSpec((1, SEQ_TILE, d_model),
                                lambda g: (ffn_tile(g) // n_tiles, ffn_tile(g) % n_tiles, 0)),
                   per_batch((KV_W, WINDOW), mixer_tile), per_batch((KV_W, WINDOW), mixer_tile),
                   per_batch((U_HIST, POOL_W), mixer_tile),
                   per_batch((n_chunks, CONV_STATE, 2 * tc), ffn_tile)),
        scratch_shapes=[
            pltpu.VMEM((WINDOW + SEQ_TILE, KV_W), F32),
            pltpu.VMEM((WINDOW + SEQ_TILE, KV_W), F32),
            pltpu.VMEM((U_HIST + SEQ_TILE, POOL_W), F32),
            pltpu.VMEM((n_chunks, 2 * SUBLANES, 2 * tc), F32),
            pltpu.VMEM((2, SEQ_TILE, 2 * tc), F32),
            pltpu.VMEM((2, SEQ_TILE, 2 * tc), BF16),
            pltpu.VMEM((SEQ_TILE, d_model), F32),
            pltpu.VMEM((SEQ_TILE, Q_W), F32),
            pltpu.VMEM((2, d_model // LANES, SEQ_TILE, LANES), F32),
            pltpu.VMEM((d_model // LANES, SEQ_TILE, LANES), F32),
            pltpu.VMEM((SEQ_TILE, d_model), BF16),
            pltpu.VMEM((SEQ_TILE, Q_W), F32),
        ],
        compiler_params=pltpu.CompilerParams(
            dimension_semantics=("arbitrary",), vmem_limit_bytes=VMEM_LIMIT),
        name="prompt",
    )(*prompt_inputs)

    feat_pos = lambda c: jnp.transpose(c[0].reshape(dec_batch, WINDOW, KV_W), (0, 2, 1))
    ck, cv = feat_pos(cache_k), feat_pos(cache_v)
    xs = x_sample.reshape(dec_batch, d_model)
    tb = SAMPLE_TB
    whole = lambda shape: pl.BlockSpec(shape, lambda i: (0,) * len(shape))
    cache_spec = pl.BlockSpec((tb, KV_W, WINDOW), lambda i: (i, 0, 0))
    attn_s, u_s, k_s, v_s = pl.pallas_call(
        _sample_a_kernel,
        grid=(dec_batch // tb,),
        out_shape=(jax.ShapeDtypeStruct((dec_batch, Q_W), F32),
                   jax.ShapeDtypeStruct((dec_batch, POOL_W), F32),
                   jax.ShapeDtypeStruct((dec_batch, KV_W, WINDOW), F32),
                   jax.ShapeDtypeStruct((dec_batch, KV_W, WINDOW), F32)),
        in_specs=[whole(xs.shape), cache_spec, cache_spec, _resident(w_in_b.shape),
                  _resident(b_in_r.shape), _resident(gains["g_pre_mix"].shape),
                  _resident(sink_col.shape), _resident(slope_col.shape)],
        out_specs=(whole((dec_batch, Q_W)), whole((dec_batch, POOL_W)), cache_spec, cache_spec),
        scratch_shapes=[pltpu.VMEM((dec_batch, Q_W), F32), pltpu.VMEM((dec_batch, KV_W), F32),
                        pltpu.VMEM((dec_batch, KV_W), F32), pltpu.VMEM((KV_W, dec_batch), F32),
                        pltpu.VMEM((KV_W, dec_batch), F32)],
        compiler_params=pltpu.CompilerParams(
            dimension_semantics=("arbitrary",), vmem_limit_bytes=VMEM_LIMIT),
        name="sample_a",
    )(xs, ck, cv, w_in_b, b_in_r, gains["g_pre_mix"], sink_col, slope_col)

    sp = jnp.transpose(state_pool[0], (1, 0, 2))
    sc = state_conv[0]
    y_s, pool_s, conv_s = pl.pallas_call(
        _sample_b_kernel,
        out_shape=(jax.ShapeDtypeStruct((dec_batch, d_model), F32),
                   jax.ShapeDtypeStruct(sp.shape, F32),
                   jax.ShapeDtypeStruct(sc.shape, F32)),
        in_specs=[_vmem()] * 17,
        out_specs=(_vmem(),) * 3,
        compiler_params=pltpu.CompilerParams(vmem_limit_bytes=VMEM_LIMIT),
        name="sample_b",
    )(xs, attn_s, u_s, sp, sc, w_pool_b, gains["pscale"], gains["g_attn"], gains["g_pool"],
      w_o_b, gains["g_post_mix"], gains["g_pre_ffn"], gains["g_post_ffn"], wup_c, cw_c, cb_c,
      wdown_c)

    def pos_feat(c):
        n = c.shape[0]
        return jnp.transpose(c, (0, 2, 1)).reshape(1, n, WINDOW, N_KV_HEADS, HEAD_DIM)

    conv_p = jax.vmap(_unchunk_cols)(c_p)
    return (y_prompt,
            y_s.reshape(dec_batch, 1, d_model),
            pos_feat(k_p), pos_feat(v_p),
            u_p[:, U_HIST - POOL_STATE:, :][None],
            conv_p[None],
            pos_feat(k_s), pos_feat(v_s),
            jnp.transpose(pool_s, (1, 0, 2))[None],
            conv_s[None])
```

```python
import functools

import numpy as np
import jax
import jax.numpy as jnp
from jax import lax
from jax.experimental import pallas as pl
from jax.experimental.pallas import tpu as pltpu

F32 = jnp.float32
BF16 = jnp.bfloat16

N_META = 16
HEAD_DIM = 64
N_HEADS = 8
N_KV_HEADS = 2
WINDOW = 128
POOL_WINDOWS = (2, 4, 8, 16)
POOL_STATE = 15
CONV_STATE = 2
RMS_EPS = 1e-6
SM_SCALE = HEAD_DIM ** -0.5
SLOPES = tuple(2.0 ** (-(h + 1) * (8.0 / N_HEADS)) for h in range(N_HEADS))

LANES = 128
SUBLANES = 8
KV_W = N_KV_HEADS * HEAD_DIM
Q_W = N_HEADS * HEAD_DIM
POOL_W = 512
POOL_GW = POOL_W // len(POOL_WINDOWS)
U_HIST = 16

SEQ_TILE = 512
FF_CHUNK = 256
SAMPLE_TB = 32
SAMPLE_UNROLL = 8
VMEM_LIMIT = 56 * 1024 * 1024
TAIL_SLABS = 4
GELU_C1 = float(np.sqrt(2.0 / np.pi))
GELU_C2 = GELU_C1 * 0.044715


def _rms(x, g):
    ms = jnp.mean(x * x, axis=-1, keepdims=True)
    return x * lax.rsqrt(ms + RMS_EPS) * g


def _dot(a, b):
    return jnp.dot(a, b, preferred_element_type=F32)


def _dot_nt(a, b):
    return lax.dot_general(a, b, (((1,), (1,)), ((), ())), preferred_element_type=F32)


def _lane_lo(shape):
    return lax.broadcasted_iota(jnp.int32, shape, len(shape) - 1) < HEAD_DIM


def _attend_block(qs, k2, v2, pos_start, sink_of):
    nq, nk = qs.shape[0], k2.shape[0]
    qi = lax.broadcasted_iota(jnp.int32, (nq, nk), 0)
    kj = lax.broadcasted_iota(jnp.int32, (nq, nk), 1)
    valid = (kj >= qi) & (kj <= qi + WINDOW) & (kj >= WINDOW - pos_start)
    neg_dist = jnp.where(valid, (kj - qi - WINDOW).astype(F32), -jnp.inf)
    lo = _lane_lo((nq, LANES))
    group = N_HEADS // N_KV_HEADS
    q_heads = []
    for p in range(N_HEADS // 2):
        g = (2 * p) // group
        blk = qs[:, p * LANES:(p + 1) * LANES]
        rolled = pltpu.roll(blk, HEAD_DIM, axis=1)
        keep = lo if g == 0 else jnp.logical_not(lo)
        for e in range(2):
            q_heads.append(jnp.where(keep, blk if e == g else rolled, 0.0).astype(BF16))
    s_all = _dot_nt(jnp.concatenate(q_heads, axis=0), k2)
    probs, dens = [], []
    for h in range(N_HEADS):
        s = s_all[h * nq:(h + 1) * nq, :] + SLOPES[h] * neg_dist
        sink = sink_of(h)
        m = jnp.maximum(jnp.max(s, axis=-1, keepdims=True), sink)
        pe = jnp.exp(s - m)
        dens.append(jnp.sum(pe, axis=-1, keepdims=True) + jnp.exp(sink - m))
        probs.append(pe.astype(BF16))
    o_all = _dot(jnp.concatenate(probs, axis=0), v2)
    outs = [o_all[h * nq:(h + 1) * nq, :] / dens[h] for h in range(N_HEADS)]
    pieces = []
    for p in range(N_HEADS // 2):
        a, b = outs[2 * p], outs[2 * p + 1]
        if (2 * p) // group == 0:
            pieces.append(jnp.where(lo, a, pltpu.roll(b, HEAD_DIM, axis=1)))
        else:
            pieces.append(jnp.where(lo, pltpu.roll(a, HEAD_DIM, axis=1), b))
    return jnp.concatenate(pieces, axis=1)


def _pool_rows(ubuf, rows, w_pool_ref, pool_scale, cnt_of):
    outs = []
    for g, w in enumerate(POOL_WINDOWS):
        cs = slice(g * POOL_GW, (g + 1) * POOL_GW)
        cur = ubuf[U_HIST:U_HIST + rows, cs]
        acc = cur
        for j in range(1, w):
            acc = acc + ubuf[U_HIST - j:U_HIST - j + rows, cs]
        mean = acc / cnt_of(w)
        outs.append(_dot((mean - cur).astype(BF16), w_pool_ref[g]))
    return jnp.concatenate(outs, axis=1) * pool_scale


def _pool_rows_full(ubuf, rows, w_pool_ref, pool_scale):
    outs = []
    for g, w in enumerate(POOL_WINDOWS):
        h = ubuf[:, g * POOL_GW:(g + 1) * POOL_GW]
        s, k = h, 1
        while k < w:
            if k < SUBLANES:
                s = s + pltpu.roll(s, k, axis=0)
            else:
                s = s + jnp.concatenate([s[:k, :], s[:-k, :]], axis=0)
            k *= 2
        cur = h[U_HIST:, :]
        outs.append(_dot((s[U_HIST:, :] * (1.0 / w) - cur).astype(BF16), w_pool_ref[g]))
    return jnp.concatenate(outs, axis=1) * pool_scale


def _mix_residual(x, attn, pool, g_attn, g_pool, w_o_ref, g_post_mix):
    mixin = jnp.concatenate([_rms(attn, g_attn), _rms(pool, g_pool)], axis=1).astype(BF16)
    return x + _rms(_dot(mixin, w_o_ref[...]), g_post_mix)


def _gated(conv, tc):
    g, hv = conv[:, :tc], conv[:, tc:]
    inner = g * (GELU_C1 + GELU_C2 * (g * g))
    return (g * (1.0 + jnp.tanh(inner)) * hv).astype(BF16)


def _ff_cols(ref, c, tc):
    d_ff = ref.shape[-1] // 2
    return jnp.concatenate([ref[:, c * tc:(c + 1) * tc],
                            ref[:, d_ff + c * tc:d_ff + (c + 1) * tc]], axis=1)


def _row_slabs(rows):
    na = rows // SUBLANES - TAIL_SLABS
    assert na % 4 == 0 and (na // 4) % 2 == 1
    slabs = [(v, na) for v in range(na)]
    slabs += [(SUBLANES * na + v, TAIL_SLABS) for v in range(TAIL_SLABS)]
    return slabs, na


def _shifted_rows(ub, prev, na):
    S = SUBLANES
    width = ub.shape[1]
    sub = lax.broadcasted_iota(jnp.int32, (S, width), 0)
    slab = lambda i: ub[i * S:(i + 1) * S, :]

    def wrap(x, y):
        return pltpu.roll(jnp.where(sub == S - 1, y, x), 1, axis=0)

    a1 = wrap(slab(na - 1), prev[S:2 * S, :])
    a2 = wrap(slab(na - 2), prev[0:S, :])
    b1 = wrap(slab(na + 3), slab(na - 1))
    b2 = wrap(slab(na + 2), slab(na - 2))
    s1 = jnp.concatenate([a1, ub[0:(na - 1) * S, :], b1, ub[na * S:(na + 3) * S, :]], axis=0)
    s2 = jnp.concatenate([a2, a1, ub[0:(na - 2) * S, :], b2, b1, ub[na * S:(na + 2) * S, :]],
                         axis=0)
    return s1, s2


def _meta_kernel(x_ref, w_in_ref, b_in_ref, sink_ref, w_pool_ref, pscale_ref, g_attn_ref,
                 g_pool_ref, w_o_ref, g_pre_mix_ref, g_post_mix_ref, g_pre_ffn_ref, wup_ref,
                 k0_ref, v0_ref, u0_ref, up0_ref, ubuf):
    rows = x_ref.shape[0]
    pos0 = N_META - rows
    x = x_ref[...]
    z = _dot(_rms(x, g_pre_mix_ref[...]).astype(BF16), w_in_ref[...]) + b_in_ref[...]
    k = z[:, Q_W:Q_W + KV_W]
    v = z[:, Q_W + KV_W:Q_W + 2 * KV_W]
    u = z[:, Q_W + 2 * KV_W:]
    pos = pos0 + lax.broadcasted_iota(jnp.int32, (rows, 1), 0)
    zeros_kv = jnp.zeros((WINDOW, KV_W), BF16)
    k2 = jnp.concatenate([zeros_kv, k.astype(BF16)], axis=0)
    v2 = jnp.concatenate([zeros_kv, v.astype(BF16)], axis=0)
    attn = _attend_block(z[:, :Q_W] * SM_SCALE, k2, v2, pos0, lambda h: sink_ref[h])

    ubuf[0:U_HIST, :] = jnp.zeros((U_HIST, POOL_W), F32)
    ubuf[U_HIST:, :] = jnp.where(pos >= 0, u, 0.0)
    cnt_of = lambda w: jnp.clip(pos + 1, 1, w).astype(F32)
    pool = _pool_rows(ubuf, rows, w_pool_ref, pscale_ref[...], cnt_of)

    x1 = _mix_residual(x, attn, pool, g_attn_ref[...], g_pool_ref[...], w_o_ref,
                       g_post_mix_ref[...])
    h2 = _rms(x1, g_pre_ffn_ref[...]).astype(BF16)
    k0_ref[...] = k
    v0_ref[...] = v
    u0_ref[...] = u[rows - U_HIST:, :]
    tail = 2 * SUBLANES
    tc = up0_ref.shape[2] // 2
    for c in range(up0_ref.shape[0]):
        up_tail = _dot(h2[rows - tail:, :], _ff_cols(wup_ref, c, tc))
        up0_ref[c] = up_tail[tail - CONV_STATE:, :]


def _prompt_kernel(n_tiles, x_ref, k0_ref, v0_ref, u0_ref, up0_ref, w_in_ref, b_in_ref,
                   sink_ref, w_pool_ref, pscale_ref, g_attn_ref, g_pool_ref, w_o_ref,
                   g_pre_mix_ref, g_post_mix_ref, g_pre_ffn_ref, g_post_ffn_ref, wup_ref, cw_ref,
                   cb_ref, wdown_ref,
                   y_ref, kout_ref, vout_ref, uout_ref, cout_ref,
                   kbuf, vbuf, ubuf, upst, upbuf, abuf, acc, attn_buf, xbuf, obuf, h2buf, qsbuf):
    g = pl.program_id(0)
    n_total = pl.num_programs(0) - 1
    rows = x_ref.shape[1]
    n_chunks, _, tc2 = upst.shape
    tc = tc2 // 2
    n_col = x_ref.shape[2] // LANES
    S = SUBLANES
    slabs, na = _row_slabs(rows)
    st2_row, st1_row = S - 1, 2 * S - 1

    tf = lax.rem(jnp.minimum(g, n_total - 1), n_tiles)
    tj = lax.rem(jnp.maximum(g - 1, 0), n_tiles)
    wslot = lax.rem(g, 2)
    rslot = 1 - wslot

    @pl.when(tf == 0)
    def _():
        kbuf[0:WINDOW, :] = k0_ref[...]
        vbuf[0:WINDOW, :] = v0_ref[...]
        ubuf[0:U_HIST, :] = u0_ref[...]

    @pl.when(tj == 0)
    def _():
        upst[...] = jnp.zeros(upst.shape, F32)
        upst[:, st2_row:st2_row + 1, :] = up0_ref[:, 0:1, :]
        upst[:, st1_row:st1_row + 1, :] = up0_ref[:, 1:2, :]

    @pl.when(g == 0)
    def _():
        xbuf[1] = jnp.zeros(xbuf.shape[1:], F32)

    def mixer_in():
        z = _dot(_rms(x_ref[0], g_pre_mix_ref[...]).astype(BF16), w_in_ref[...]) + b_in_ref[...]
        kbuf[WINDOW:, :] = z[:, Q_W:Q_W + KV_W]
        vbuf[WINDOW:, :] = z[:, Q_W + KV_W:Q_W + 2 * KV_W]
        ubuf[U_HIST:, :] = z[:, Q_W + 2 * KV_W:]
        qsbuf[...] = z[:, :Q_W] * SM_SCALE

    def mixer_attend(i):
        r0 = i * WINDOW
        k2 = kbuf[r0:r0 + 2 * WINDOW, :].astype(BF16)
        v2 = vbuf[r0:r0 + 2 * WINDOW, :].astype(BF16)
        attn_buf[r0:r0 + WINDOW, :] = _attend_block(
            qsbuf[r0:r0 + WINDOW, :], k2, v2, N_META + tf * rows + r0, lambda h: sink_ref[h])

    def mixer_out():
        pool = _pool_rows_full(ubuf, rows, w_pool_ref, pscale_ref[...])
        x1 = _mix_residual(x_ref[0], attn_buf[...], pool, g_attn_ref[...], g_pool_ref[...],
                           w_o_ref, g_post_mix_ref[...])
        for j in range(n_col):
            xbuf[wslot, j] = x1[:, j * LANES:(j + 1) * LANES]
        kbuf[0:WINDOW, :] = kbuf[rows:rows + WINDOW, :]
        vbuf[0:WINDOW, :] = vbuf[rows:rows + WINDOW, :]
        ubuf[0:U_HIST, :] = ubuf[rows:rows + U_HIST, :]

    def x1_slab_order():
        return jnp.concatenate(
            [jnp.concatenate([xbuf[rslot, j, pl.ds(start, S, stride=stride), :]
                              for j in range(n_col)], axis=1) for start, stride in slabs], axis=0)

    def ffn_in():
        h2buf[...] = _rms(x1_slab_order(), g_pre_ffn_ref[...]).astype(BF16)

    def up_project(c):
        upbuf[c % 2] = _dot(h2buf[...], _ff_cols(wup_ref, c, tc))

    def activate(c):
        ub = upbuf.at[c % 2]
        s1, s2 = _shifted_rows(ub, upst[c], na)
        upst[c] = ub[rows - 2 * S:rows, :]
        cw = _ff_cols(cw_ref, c, tc)
        conv = _ff_cols(cb_ref, c, tc) + s2 * cw[0:1, :]
        conv = conv + s1 * cw[1:2, :]
        conv = conv + ub[...] * cw[2:3, :]
        abuf[c % 2] = _gated(conv, tc)

    def down_project(c):
        part = _dot(abuf[c % 2], wdown_ref[c * tc:(c + 1) * tc, :])
        if c == 0:
            acc[...] = part
        else:
            acc[...] += part

    def ffn_stage(c):
        if c + 1 < n_chunks:
            up_project(c + 1)
        if c < n_chunks:
            activate(c)
        if c >= 1:
            down_project(c - 1)

    def ffn_out():
        y = x1_slab_order() + _rms(acc[...], g_post_ffn_ref[...])
        for i, (start, stride) in enumerate(slabs):
            for j in range(n_col):
                obuf[j, pl.ds(start, S, stride=stride), :] = y[i * S:(i + 1) * S,
                                                              j * LANES:(j + 1) * LANES]
        y_ref[0] = jnp.concatenate([obuf[j] for j in range(n_col)], axis=1)

    ffn_stages = [lambda c=c: ffn_stage(c) for c in range(n_chunks + 1)]
    mixer_stages = [mixer_in] + [lambda i=i: mixer_attend(i) for i in range(rows // WINDOW)]
    mixer_stages.append(mixer_out)

    ffn_in()
    up_project(0)
    emitted = 0
    for c, stage in enumerate(ffn_stages):
        due = ((c + 1) * len(mixer_stages)) // len(ffn_stages)
        while emitted < due:
            mixer_stages[emitted]()
            emitted += 1
        stage()
    ffn_out()

    @pl.when((tf == n_tiles - 1) & (g < n_total))
    def _():
        kout_ref[0] = kbuf[0:WINDOW, :].T
        vout_ref[0] = vbuf[0:WINDOW, :].T
        uout_ref[0] = ubuf[0:U_HIST, :]

    @pl.when((tj == n_tiles - 1) & (g > 0))
    def _():
        cout_ref[0, :, 0:1, :] = upst[:, st2_row:st2_row + 1, :]
        cout_ref[0, :, 1:2, :] = upst[:, st1_row:st1_row + 1, :]


def _sample_a_kernel(x_ref, ck_ref, cv_ref, w_in_ref, b_in_ref, g_pre_mix_ref, sinkcol_ref,
                     slopecol_ref,
                     attn_ref, u_ref, ko_ref, vo_ref,
                     qs_buf, kn_buf, vn_buf, knt_buf, vnt_buf):
    step = pl.program_id(0)
    n_tok = ck_ref.shape[0]

    @pl.when(step == 0)
    def _():
        z = _dot(_rms(x_ref[...], g_pre_mix_ref[...]).astype(BF16), w_in_ref[...]) + b_in_ref[...]
        k = z[:, Q_W:Q_W + KV_W]
        v = z[:, Q_W + KV_W:Q_W + 2 * KV_W]
        qs_buf[...] = z[:, :Q_W] * SM_SCALE
        kn_buf[...] = k
        vn_buf[...] = v
        knt_buf[...] = k.T
        vnt_buf[...] = v.T
        u_ref[...] = z[:, Q_W + 2 * KV_W:]

    lo1 = _lane_lo((1, LANES))
    kj = lax.broadcasted_iota(jnp.int32, (N_HEADS, WINDOW), 1)
    neg_dist = (kj - WINDOW).astype(F32)
    bias = slopecol_ref[...] * neg_dist
    sink = sinkcol_ref[...]
    last_lane = lax.broadcasted_iota(jnp.int32, (KV_W, WINDOW), 1) == WINDOW - 1

    def token(j, carry):
        b = step * n_tok + j
        qrow = qs_buf[pl.ds(b, 1), :]
        kn = kn_buf[pl.ds(b, 1), :]
        vn = vn_buf[pl.ds(b, 1), :]
        kt = ck_ref[j]
        vt = cv_ref[j]
        heads = []
        for h in range(N_HEADS):
            p, e, g = h // 2, h % 2, h // (N_HEADS // N_KV_HEADS)
            blk = qrow[:, p * LANES:(p + 1) * LANES]
            src = blk if e == g else pltpu.roll(blk, HEAD_DIM, axis=1)
            keep = lo1 if g == 0 else jnp.logical_not(lo1)
            heads.append(jnp.where(keep, src, 0.0))
        qf = jnp.concatenate(heads, axis=0).astype(BF16)
        s = _dot(qf, kt.astype(BF16)) + bias
        s_self = jnp.sum(qf.astype(F32) * kn.astype(BF16).astype(F32), axis=-1, keepdims=True)
        m = jnp.maximum(jnp.maximum(jnp.max(s, axis=-1, keepdims=True), s_self), sink)
        pe = jnp.exp(s - m)
        pe_self = jnp.exp(s_self - m)
        den = jnp.sum(pe, axis=-1, keepdims=True) + pe_self + jnp.exp(sink - m)
        o = _dot_nt(pe.astype(BF16), vt.astype(BF16))
        o = o + pe_self.astype(BF16).astype(F32) * vn.astype(BF16).astype(F32)
        o = o / den
        pieces = []
        for p in range(N_HEADS // 2):
            g = (2 * p) // (N_HEADS // N_KV_HEADS)
            a, c = o[2 * p:2 * p + 1, :], o[2 * p + 1:2 * p + 2, :]
            if g == 0:
                pieces.append(jnp.where(lo1, a, pltpu.roll(c, HEAD_DIM, axis=1)))
            else:
                pieces.append(jnp.where(lo1, pltpu.roll(a, HEAD_DIM, axis=1), c))
        attn_ref[pl.ds(b, 1), :] = jnp.concatenate(pieces, axis=1)
        bring = WINDOW - 1 - b
        ko_ref[j] = jnp.where(last_lane, pltpu.roll(knt_buf[...], bring, axis=1),
                              pltpu.roll(kt, WINDOW - 1, axis=1))
        vo_ref[j] = jnp.where(last_lane, pltpu.roll(vnt_buf[...], bring, axis=1),
                              pltpu.roll(vt, WINDOW - 1, axis=1))
        return carry

    lax.fori_loop(0, n_tok, token, 0, unroll=SAMPLE_UNROLL)


def _sample_b_kernel(x_ref, attn_ref, u_ref, sp_ref, sc_ref, w_pool_ref, pscale_ref,
                     g_attn_ref, g_pool_ref, w_o_ref, g_post_mix_ref, g_pre_ffn_ref,
                     g_post_ffn_ref, wup_ref, cw_ref, cb_ref, wdown_ref,
                     y_ref, po_ref, co_ref):
    tc = FF_CHUNK
    d_ff = wdown_ref.shape[0]
    n_chunks = d_ff // tc
    u = u_ref[...]

    outs = []
    for g, w in enumerate(POOL_WINDOWS):
        cs = slice(g * POOL_GW, (g + 1) * POOL_GW)
        cur = u[:, cs]
        acc_u = cur
        for j in range(1, w):
            acc_u = acc_u + sp_ref[POOL_STATE - j, :, cs]
        outs.append(_dot((acc_u / float(w) - cur).astype(BF16), w_pool_ref[g]))
    pool = jnp.concatenate(outs, axis=1) * pscale_ref[...]
    for r in range(POOL_STATE - 1):
        po_ref[r] = sp_ref[r + 1]
    po_ref[POOL_STATE - 1] = u

    x1 = _mix_residual(x_ref[...], attn_ref[...], pool, g_attn_ref[...], g_pool_ref[...],
                       w_o_ref, g_post_mix_ref[...])
    h2 = _rms(x1, g_pre_ffn_ref[...]).astype(BF16)

    co_ref[:, 0, :] = sc_ref[:, 1, :]
    ffn = jnp.zeros(x1.shape, F32)
    for c in range(n_chunks):
        up = _dot(h2, _ff_cols(wup_ref, c, tc))
        gcols = slice(c * tc, (c + 1) * tc)
        vcols = slice(d_ff + c * tc, d_ff + (c + 1) * tc)
        co_ref[:, 1, gcols] = up[:, :tc]
        co_ref[:, 1, vcols] = up[:, tc:]
        old0 = jnp.concatenate([sc_ref[:, 0, gcols], sc_ref[:, 0, vcols]], axis=1)
        old1 = jnp.concatenate([sc_ref[:, 1, gcols], sc_ref[:, 1, vcols]], axis=1)
        cw = _ff_cols(cw_ref, c, tc)
        conv = _ff_cols(cb_ref, c, tc) + old0 * cw[0:1, :]
        conv = conv + old1 * cw[1:2, :]
        conv = conv + up * cw[2:3, :]
        ffn = ffn + _dot(_gated(conv, tc), wdown_ref[c * tc:(c + 1) * tc, :])
    y_ref[...] = x1 + _rms(ffn, g_post_ffn_ref[...])


def _vmem():
    return pl.BlockSpec(memory_space=pltpu.VMEM)


def _smem():
    return pl.BlockSpec(memory_space=pltpu.SMEM)


def _resident(shape):
    nd = len(shape)
    return pl.BlockSpec(shape, lambda *_: (0,) * nd, pipeline_mode=pl.Buffered(1))


def _unchunk_cols(a):
    n_chunks, r, tc2 = a.shape
    a = a.reshape(n_chunks, r, 2, tc2 // 2)
    return jnp.transpose(a, (1, 2, 0, 3)).reshape(r, n_chunks * tc2)


def kernel(x_prompt, x_sample, cache_k, cache_v, state_pool, state_conv, meta, w_in, b_in, sinks,
           w_pool, pool_scale, g_attn_out, g_pool_out, w_o, g_pre_mix, g_post_mix, g_pre_ffn,
           g_post_ffn, w_up, conv_w, conv_b, w_down):
    assert w_in.shape[0] == 1, "single layer"
    batch, seq, d_model = x_prompt.shape
    dec_batch = x_sample.shape[0]
    d_ff = w_down.shape[1]
    tc = FF_CHUNK
    n_chunks = d_ff // tc
    assert n_chunks * tc == d_ff and seq % SEQ_TILE == 0 and dec_batch % SAMPLE_TB == 0
    assert meta.shape[0] == N_META and N_META > POOL_STATE

    row = lambda a: a[0].reshape(1, -1)
    w_in_b = w_in[0].astype(BF16)
    b_in_r = row(b_in)
    w_pool_b = w_pool[0].astype(BF16)
    w_o_b = w_o[0].astype(BF16)
    wup_c = w_up[0].astype(BF16)
    half_value = jnp.concatenate([jnp.ones((d_ff,), F32), jnp.full((d_ff,), 0.5, F32)])
    cw_c = conv_w[0] * half_value
    cb_c = (conv_b[0] * half_value).reshape(1, -1)
    wdown_c = w_down[0].astype(BF16)
    sink_s = sinks[0]
    sink_col = sinks[0].reshape(N_HEADS, 1)
    slope_col = jnp.asarray(np.array(SLOPES, np.float32).reshape(N_HEADS, 1))
    gains = dict(pscale=row(pool_scale), g_attn=row(g_attn_out), g_pool=row(g_pool_out),
                 g_pre_mix=row(g_pre_mix), g_post_mix=row(g_post_mix),
                 g_pre_ffn=row(g_pre_ffn), g_post_ffn=row(g_post_ffn))

    x_meta = jnp.concatenate([jnp.zeros((WINDOW - N_META, d_model), F32), meta.astype(F32)], 0)
    k0, v0, u0, up0 = pl.pallas_call(
        _meta_kernel,
        out_shape=(jax.ShapeDtypeStruct((WINDOW, KV_W), F32),
                   jax.ShapeDtypeStruct((WINDOW, KV_W), F32),
                   jax.ShapeDtypeStruct((U_HIST, POOL_W), F32),
                   jax.ShapeDtypeStruct((n_chunks, CONV_STATE, 2 * tc), F32)),
        in_specs=[_vmem(), _vmem(), _vmem(), _smem()] + [_vmem()] * 9,
        out_specs=(_vmem(),) * 4,
        scratch_shapes=[pltpu.VMEM((U_HIST + WINDOW, POOL_W), F32)],
        compiler_params=pltpu.CompilerParams(vmem_limit_bytes=VMEM_LIMIT),
        name="meta",
    )(x_meta, w_in_b, b_in_r, sink_s, w_pool_b, gains["pscale"], gains["g_attn"],
      gains["g_pool"], w_o_b, gains["g_pre_mix"], gains["g_post_mix"], gains["g_pre_ffn"], wup_c)

    n_tiles = seq // SEQ_TILE
    n_total = batch * n_tiles
    mixer_tile = lambda g: jnp.minimum(g, n_total - 1)
    ffn_tile = lambda g: jnp.maximum(g - 1, 0)
    per_batch = lambda shape, tile: pl.BlockSpec(
        (1,) + shape, lambda g: (tile(g) // n_tiles,) + (0,) * len(shape))
    prompt_inputs = (
        x_prompt, k0, v0, u0, up0, w_in_b, b_in_r, sink_s, w_pool_b, gains["pscale"],
        gains["g_attn"], gains["g_pool"], w_o_b, gains["g_pre_mix"], gains["g_post_mix"],
        gains["g_pre_ffn"], gains["g_post_ffn"], wup_c, cw_c, cb_c, wdown_c)
    in_specs = [pl.BlockSpec((1, SEQ_TILE, d_model),
                             lambda g: (mixer_tile(g) // n_tiles, mixer_tile(g) % n_tiles, 0))]
    in_specs += [_smem() if a is sink_s else _resident(a.shape) for a in prompt_inputs[1:]]
    y_prompt, k_p, v_p, u_p, c_p = pl.pallas_call(
        functools.partial(_prompt_kernel, n_tiles),
        grid=(n_total + 1,),
        out_shape=(jax.ShapeDtypeStruct((batch, seq, d_model), F32),
                   jax.ShapeDtypeStruct((batch, WINDOW, KV_W), F32),
                   jax.ShapeDtypeStruct((batch, WINDOW, KV_W), F32),
                   jax.ShapeDtypeStruct((batch, U_HIST, POOL_W), F32),
                   jax.ShapeDtypeStruct((batch, n_chunks, CONV_STATE, 2 * tc), F32)),
        in_specs=in_specs,
        out_specs=(pl.BlockSpec((1, SEQ_TILE, d_model),
                                lambda g: (ffn_tile(g) // n_tiles, ffn_tile(g) % n_tiles, 0)),
                   per_batch((KV_W, WINDOW), mixer_tile), per_batch((KV_W, WINDOW), mixer_tile),
                   per_batch((U_HIST, POOL_W), mixer_tile),
                   per_batch((n_chunks, CONV_STATE, 2 * tc), ffn_tile)),
        scratch_shapes=[
            pltpu.VMEM((WINDOW + SEQ_TILE, KV_W), F32),
            pltpu.VMEM((WINDOW + SEQ_TILE, KV_W), F32),
            pltpu.VMEM((U_HIST + SEQ_TILE, POOL_W), F32),
            pltpu.VMEM((n_chunks, 2 * SUBLANES, 2 * tc), F32),
            pltpu.VMEM((2, SEQ_TILE, 2 * tc), F32),
            pltpu.VMEM((2, SEQ_TILE, tc), BF16),
            pltpu.VMEM((SEQ_TILE, d_model), F32),
            pltpu.VMEM((SEQ_TILE, Q_W), F32),
            pltpu.VMEM((2, d_model // LANES, SEQ_TILE, LANES), F32),
            pltpu.VMEM((d_model // LANES, SEQ_TILE, LANES), F32),
            pltpu.VMEM((SEQ_TILE, d_model), BF16),
            pltpu.VMEM((SEQ_TILE, Q_W), F32),
        ],
        compiler_params=pltpu.CompilerParams(
            dimension_semantics=("arbitrary",), vmem_limit_bytes=VMEM_LIMIT),
        name="prompt",
    )(*prompt_inputs)

    feat_pos = lambda c: jnp.transpose(c[0].reshape(dec_batch, WINDOW, KV_W), (0, 2, 1))
    ck, cv = feat_pos(cache_k), feat_pos(cache_v)
    xs = x_sample.reshape(dec_batch, d_model)
    tb = SAMPLE_TB
    whole = lambda shape: pl.BlockSpec(shape, lambda i: (0,) * len(shape))
    cache_spec = pl.BlockSpec((tb, KV_W, WINDOW), lambda i: (i, 0, 0))
    attn_s, u_s, k_s, v_s = pl.pallas_call(
        _sample_a_kernel,
        grid=(dec_batch // tb,),
        out_shape=(jax.ShapeDtypeStruct((dec_batch, Q_W), F32),
                   jax.ShapeDtypeStruct((dec_batch, POOL_W), F32),
                   jax.ShapeDtypeStruct((dec_batch, KV_W, WINDOW), F32),
                   jax.ShapeDtypeStruct((dec_batch, KV_W, WINDOW), F32)),
        in_specs=[whole(xs.shape), cache_spec, cache_spec, _resident(w_in_b.shape),
                  _resident(b_in_r.shape), _resident(gains["g_pre_mix"].shape),
                  _resident(sink_col.shape), _resident(slope_col.shape)],
        out_specs=(whole((dec_batch, Q_W)), whole((dec_batch, POOL_W)), cache_spec, cache_spec),
        scratch_shapes=[pltpu.VMEM((dec_batch, Q_W), F32), pltpu.VMEM((dec_batch, KV_W), F32),
                        pltpu.VMEM((dec_batch, KV_W), F32), pltpu.VMEM((KV_W, dec_batch), F32),
                        pltpu.VMEM((KV_W, dec_batch), F32)],
        compiler_params=pltpu.CompilerParams(
            dimension_semantics=("arbitrary",), vmem_limit_bytes=VMEM_LIMIT),
        name="sample_a",
    )(xs, ck, cv, w_in_b, b_in_r, gains["g_pre_mix"], sink_col, slope_col)

    sp = jnp.transpose(state_pool[0], (1, 0, 2))
    sc = state_conv[0]
    y_s, pool_s, conv_s = pl.pallas_call(
        _sample_b_kernel,
        out_shape=(jax.ShapeDtypeStruct((dec_batch, d_model), F32),
                   jax.ShapeDtypeStruct(sp.shape, F32),
                   jax.ShapeDtypeStruct(sc.shape, F32)),
        in_specs=[_vmem()] * 17,
        out_specs=(_vmem(),) * 3,
        compiler_params=pltpu.CompilerParams(vmem_limit_bytes=VMEM_LIMIT),
        name="sample_b",
    )(xs, attn_s, u_s, sp, sc, w_pool_b, gains["pscale"], gains["g_attn"], gains["g_pool"],
      w_o_b, gains["g_post_mix"], gains["g_pre_ffn"], gains["g_post_ffn"], wup_c, cw_c, cb_c,
      wdown_c)

    def pos_feat(c):
        n = c.shape[0]
        return jnp.transpose(c, (0, 2, 1)).reshape(1, n, WINDOW, N_KV_HEADS, HEAD_DIM)

    conv_p = jax.vmap(_unchunk_cols)(c_p)
    return (y_prompt,
            y_s.reshape(dec_batch, 1, d_model),
            pos_feat(k_p), pos_feat(v_p),
            u_p[:, U_HIST - POOL_STATE:, :][None],
            conv_p[None],
            pos_feat(k_s), pos_feat(v_s),
            jnp.transpose(pool_s, (1, 0, 2))[None],
            conv_s[None])
```

```python
import functools

import numpy as np
import jax
import jax.numpy as jnp
from jax import lax
from jax.experimental import pallas as pl
from jax.experimental.pallas import tpu as pltpu

F32 = jnp.float32
BF16 = jnp.bfloat16

N_META = 16
HEAD_DIM = 64
N_HEADS = 8
N_KV_HEADS = 2
WINDOW = 128
POOL_WINDOWS = (2, 4, 8, 16)
POOL_STATE = 15
CONV_STATE = 2
RMS_EPS = 1e-6
SM_SCALE = HEAD_DIM ** -0.5
SLOPES = tuple(2.0 ** (-(h + 1) * (8.0 / N_HEADS)) for h in range(N_HEADS))

LANES = 128
SUBLANES = 8
KV_W = N_KV_HEADS * HEAD_DIM
Q_W = N_HEADS * HEAD_DIM
POOL_W = 512
POOL_GW = POOL_W // len(POOL_WINDOWS)
U_HIST = 16

SEQ_TILE = 512
FF_CHUNK = 256
SAMPLE_TB = 32
SAMPLE_UNROLL = 8
VMEM_LIMIT = 56 * 1024 * 1024
TAIL_SLABS = 4
GELU_C1 = float(np.sqrt(2.0 / np.pi))
GELU_C2 = GELU_C1 * 0.044715


def _rms(x, g):
    ms = jnp.mean(x * x, axis=-1, keepdims=True)
    return x * lax.rsqrt(ms + RMS_EPS) * g


def _dot(a, b):
    return jnp.dot(a, b, preferred_element_type=F32)


def _dot_nt(a, b):
    return lax.dot_general(a, b, (((1,), (1,)), ((), ())), preferred_element_type=F32)


def _lane_lo(shape):
    return lax.broadcasted_iota(jnp.int32, shape, len(shape) - 1) < HEAD_DIM


def _attend_block(qs, k2, v2, pos_start, sink_of):
    nq, nk = qs.shape[0], k2.shape[0]
    qi = lax.broadcasted_iota(jnp.int32, (nq, nk), 0)
    kj = lax.broadcasted_iota(jnp.int32, (nq, nk), 1)
    valid = (kj >= qi) & (kj <= qi + WINDOW) & (kj >= WINDOW - pos_start)
    neg_dist = jnp.where(valid, (kj - qi - WINDOW).astype(F32), -jnp.inf)
    lo = _lane_lo((nq, LANES))
    group = N_HEADS // N_KV_HEADS
    q_heads = []
    for p in range(N_HEADS // 2):
        g = (2 * p) // group
        blk = qs[:, p * LANES:(p + 1) * LANES]
        rolled = pltpu.roll(blk, HEAD_DIM, axis=1)
        keep = lo if g == 0 else jnp.logical_not(lo)
        for e in range(2):
            q_heads.append(jnp.where(keep, blk if e == g else rolled, 0.0).astype(BF16))
    s_all = _dot_nt(jnp.concatenate(q_heads, axis=0), k2)
    probs, dens = [], []
    for h in range(N_HEADS):
        s = s_all[h * nq:(h + 1) * nq, :] + SLOPES[h] * neg_dist
        sink = sink_of(h)
        m = jnp.maximum(jnp.max(s, axis=-1, keepdims=True), sink)
        pe = jnp.exp(s - m)
        dens.append(jnp.sum(pe, axis=-1, keepdims=True) + jnp.exp(sink - m))
        probs.append(pe.astype(BF16))
    o_all = _dot(jnp.concatenate(probs, axis=0), v2)
    outs = [o_all[h * nq:(h + 1) * nq, :] / dens[h] for h in range(N_HEADS)]
    pieces = []
    for p in range(N_HEADS // 2):
        a, b = outs[2 * p], outs[2 * p + 1]
        if (2 * p) // group == 0:
            pieces.append(jnp.where(lo, a, pltpu.roll(b, HEAD_DIM, axis=1)))
        else:
            pieces.append(jnp.where(lo, pltpu.roll(a, HEAD_DIM, axis=1), b))
    return jnp.concatenate(pieces, axis=1)


def _pool_rows(ubuf, rows, w_pool_ref, pool_scale, cnt_of):
    outs = []
    for g, w in enumerate(POOL_WINDOWS):
        cs = slice(g * POOL_GW, (g + 1) * POOL_GW)
        cur = ubuf[U_HIST:U_HIST + rows, cs]
        acc = cur
        for j in range(1, w):
            acc = acc + ubuf[U_HIST - j:U_HIST - j + rows, cs]
        mean = acc / cnt_of(w)
        outs.append(_dot((mean - cur).astype(BF16), w_pool_ref[g]))
    return jnp.concatenate(outs, axis=1) * pool_scale


def _pool_rows_full(ubuf, rows, w_pool_ref, pool_scale):
    outs = []
    for g, w in enumerate(POOL_WINDOWS):
        h = ubuf[:, g * POOL_GW:(g + 1) * POOL_GW]
        s, k = h, 1
        while k < w:
            if k < SUBLANES:
                s = s + pltpu.roll(s, k, axis=0)
            else:
                s = s + jnp.concatenate([s[:k, :], s[:-k, :]], axis=0)
            k *= 2
        cur = h[U_HIST:, :]
        outs.append(_dot((s[U_HIST:, :] * (1.0 / w) - cur).astype(BF16), w_pool_ref[g]))
    return jnp.concatenate(outs, axis=1) * pool_scale


def _mix_residual(x, attn, pool, g_attn, g_pool, w_o_ref, g_post_mix):
    mixin = jnp.concatenate([_rms(attn, g_attn), _rms(pool, g_pool)], axis=1).astype(BF16)
    return x + _rms(_dot(mixin, w_o_ref[...]), g_post_mix)


def _gated(conv, tc):
    g, hv = conv[:, :tc], conv[:, tc:]
    inner = g * (GELU_C1 + GELU_C2 * (g * g))
    return (g * (1.0 + jnp.tanh(inner)) * hv).astype(BF16)


def _ff_cols(ref, c, tc):
    d_ff = ref.shape[-1] // 2
    return jnp.concatenate([ref[:, c * tc:(c + 1) * tc],
                            ref[:, d_ff + c * tc:d_ff + (c + 1) * tc]], axis=1)


def _row_slabs(rows):
    na = rows // SUBLANES - TAIL_SLABS
    assert na % 4 == 0 and (na // 4) % 2 == 1
    slabs = [(v, na) for v in range(na)]
    slabs += [(SUBLANES * na + v, TAIL_SLABS) for v in range(TAIL_SLABS)]
    return slabs, na


def _shifted_rows(ub, prev, na):
    S = SUBLANES
    width = ub.shape[1]
    sub = lax.broadcasted_iota(jnp.int32, (S, width), 0)
    slab = lambda i: ub[i * S:(i + 1) * S, :]

    def wrap(x, y):
        return pltpu.roll(jnp.where(sub == S - 1, y, x), 1, axis=0)

    a1 = wrap(slab(na - 1), prev[S:2 * S, :])
    a2 = wrap(slab(na - 2), prev[0:S, :])
    b1 = wrap(slab(na + 3), slab(na - 1))
    b2 = wrap(slab(na + 2), slab(na - 2))
    s1 = jnp.concatenate([a1, ub[0:(na - 1) * S, :], b1, ub[na * S:(na + 3) * S, :]], axis=0)
    s2 = jnp.concatenate([a2, a1, ub[0:(na - 2) * S, :], b2, b1, ub[na * S:(na + 2) * S, :]],
                         axis=0)
    return s1, s2


def _meta_kernel(x_ref, w_in_ref, b_in_ref, sink_ref, w_pool_ref, pscale_ref, g_attn_ref,
                 g_pool_ref, w_o_ref, g_pre_mix_ref, g_post_mix_ref, g_pre_ffn_ref, wup_ref,
                 k0_ref, v0_ref, u0_ref, up0_ref, ubuf):
    rows = x_ref.shape[0]
    pos0 = N_META - rows
    x = x_ref[...]
    z = _dot(_rms(x, g_pre_mix_ref[...]).astype(BF16), w_in_ref[...]) + b_in_ref[...]
    k = z[:, Q_W:Q_W + KV_W]
    v = z[:, Q_W + KV_W:Q_W + 2 * KV_W]
    u = z[:, Q_W + 2 * KV_W:]
    pos = pos0 + lax.broadcasted_iota(jnp.int32, (rows, 1), 0)
    zeros_kv = jnp.zeros((WINDOW, KV_W), BF16)
    k2 = jnp.concatenate([zeros_kv, k.astype(BF16)], axis=0)
    v2 = jnp.concatenate([zeros_kv, v.astype(BF16)], axis=0)
    attn = _attend_block(z[:, :Q_W] * SM_SCALE, k2, v2, pos0, lambda h: sink_ref[h])

    ubuf[0:U_HIST, :] = jnp.zeros((U_HIST, POOL_W), F32)
    ubuf[U_HIST:, :] = jnp.where(pos >= 0, u, 0.0)
    cnt_of = lambda w: jnp.clip(pos + 1, 1, w).astype(F32)
    pool = _pool_rows(ubuf, rows, w_pool_ref, pscale_ref[...], cnt_of)

    x1 = _mix_residual(x, attn, pool, g_attn_ref[...], g_pool_ref[...], w_o_ref,
                       g_post_mix_ref[...])
    h2 = _rms(x1, g_pre_ffn_ref[...]).astype(BF16)
    k0_ref[...] = k
    v0_ref[...] = v
    u0_ref[...] = u[rows - U_HIST:, :]
    tail = 2 * SUBLANES
    tc = up0_ref.shape[2] // 2
    for c in range(up0_ref.shape[0]):
        up_tail = _dot(h2[rows - tail:, :], _ff_cols(wup_ref, c, tc))
        up0_ref[c] = up_tail[tail - CONV_STATE:, :]


def _prompt_kernel(n_tiles, x_ref, k0_ref, v0_ref, u0_ref, up0_ref, w_in_ref, b_in_ref,
                   sink_ref, w_pool_ref, pscale_ref, g_attn_ref, g_pool_ref, w_o_ref,
                   g_pre_mix_ref, g_post_mix_ref, g_pre_ffn_ref, g_post_ffn_ref, wup_ref, cw_ref,
                   cb_ref, wdown_ref,
                   y_ref, kout_ref, vout_ref, uout_ref, cout_ref,
                   kbuf, vbuf, ubuf, upst, upbuf, abuf, acc, attn_buf, xbuf, obuf, h2buf, qsbuf):
    g = pl.program_id(0)
    n_total = pl.num_programs(0) - 2
    rows = x_ref.shape[1]
    n_chunks, _, tc2 = upst.shape
    tc = tc2 // 2
    n_col = x_ref.shape[2] // LANES
    S = SUBLANES
    slabs, na = _row_slabs(rows)
    st2_row, st1_row = S - 1, 2 * S - 1

    tf = lax.rem(jnp.minimum(g, n_total - 1), n_tiles)
    tj = lax.rem(jnp.clip(g - 1, 0, n_total - 1), n_tiles)
    wslot = lax.rem(g, 3)
    rslot = lax.rem(g + 2, 3)
    oslot = lax.rem(g + 1, 3)

    @pl.when(tf == 0)
    def _():
        kbuf[0:WINDOW, :] = k0_ref[...]
        vbuf[0:WINDOW, :] = v0_ref[...]
        ubuf[0:U_HIST, :] = u0_ref[...]

    @pl.when(tj == 0)
    def _():
        upst[...] = jnp.zeros(upst.shape, F32)
        upst[:, st2_row:st2_row + 1, :] = up0_ref[:, 0:1, :]
        upst[:, st1_row:st1_row + 1, :] = up0_ref[:, 1:2, :]

    @pl.when(g == 0)
    def _():
        xbuf[...] = jnp.zeros(xbuf.shape, F32)
        acc[...] = jnp.zeros(acc.shape, F32)

    def mixer_in():
        z = _dot(_rms(x_ref[0], g_pre_mix_ref[...]).astype(BF16), w_in_ref[...]) + b_in_ref[...]
        kbuf[WINDOW:, :] = z[:, Q_W:Q_W + KV_W]
        vbuf[WINDOW:, :] = z[:, Q_W + KV_W:Q_W + 2 * KV_W]
        ubuf[U_HIST:, :] = z[:, Q_W + 2 * KV_W:]
        qsbuf[...] = z[:, :Q_W] * SM_SCALE

    def mixer_attend(i):
        r0 = i * WINDOW
        k2 = kbuf[r0:r0 + 2 * WINDOW, :].astype(BF16)
        v2 = vbuf[r0:r0 + 2 * WINDOW, :].astype(BF16)
        attn_buf[r0:r0 + WINDOW, :] = _attend_block(
            qsbuf[r0:r0 + WINDOW, :], k2, v2, N_META + tf * rows + r0, lambda h: sink_ref[h])

    def mixer_out():
        pool = _pool_rows_full(ubuf, rows, w_pool_ref, pscale_ref[...])
        x1 = _mix_residual(x_ref[0], attn_buf[...], pool, g_attn_ref[...], g_pool_ref[...],
                           w_o_ref, g_post_mix_ref[...])
        for j in range(n_col):
            xbuf[wslot, j] = x1[:, j * LANES:(j + 1) * LANES]
        kbuf[0:WINDOW, :] = kbuf[rows:rows + WINDOW, :]
        vbuf[0:WINDOW, :] = vbuf[rows:rows + WINDOW, :]
        ubuf[0:U_HIST, :] = ubuf[rows:rows + U_HIST, :]

    def x1_slab_order(slot):
        return jnp.concatenate(
            [jnp.concatenate([xbuf[slot, j, pl.ds(start, S, stride=stride), :]
                              for j in range(n_col)], axis=1) for start, stride in slabs], axis=0)

    def ffn_in():
        h2buf[...] = _rms(x1_slab_order(rslot), g_pre_ffn_ref[...]).astype(BF16)

    def up_project(c):
        upbuf[c % 2] = _dot(h2buf[...], _ff_cols(wup_ref, c, tc))

    def activate(c):
        ub = upbuf.at[c % 2]
        s1, s2 = _shifted_rows(ub, upst[c], na)
        upst[c] = ub[rows - 2 * S:rows, :]
        cw = _ff_cols(cw_ref, c, tc)
        conv = _ff_cols(cb_ref, c, tc) + s2 * cw[0:1, :]
        conv = conv + s1 * cw[1:2, :]
        conv = conv + ub[...] * cw[2:3, :]
        abuf[c % 2] = _gated(conv, tc)

    def down_project(c):
        part = _dot(abuf[c % 2], wdown_ref[c * tc:(c + 1) * tc, :])
        if c == 0:
            acc[...] = part
        else:
            acc[...] += part

    def ffn_stage(c):
        if c + 1 < n_chunks:
            up_project(c + 1)
        if c < n_chunks:
            activate(c)
        if c >= 1:
            down_project(c - 1)

    def ffn_out():
        y = x1_slab_order(oslot) + _rms(acc[...], g_post_ffn_ref[...])
        for i, (start, stride) in enumerate(slabs):
            for j in range(n_col):
                obuf[j, pl.ds(start, S, stride=stride), :] = y[i * S:(i + 1) * S,
                                                              j * LANES:(j + 1) * LANES]
        y_ref[0] = jnp.concatenate([obuf[j] for j in range(n_col)], axis=1)

    ffn_stages = [lambda c=c: ffn_stage(c) for c in range(n_chunks + 1)]
    mixer_stages = [mixer_in] + [lambda i=i: mixer_attend(i) for i in range(rows // WINDOW)]
    mixer_stages.append(mixer_out)

    ffn_out()
    ffn_in()
    up_project(0)
    emitted = 0
    for c, stage in enumerate(ffn_stages):
        due = ((c + 1) * len(mixer_stages)) // len(ffn_stages)
        while emitted < due:
            mixer_stages[emitted]()
            emitted += 1
        stage()

    @pl.when((tf == n_tiles - 1) & (g < n_total))
    def _():
        kout_ref[0] = kbuf[0:WINDOW, :].T
        vout_ref[0] = vbuf[0:WINDOW, :].T
        uout_ref[0] = ubuf[0:U_HIST, :]

    @pl.when((tj == n_tiles - 1) & (g >= 1) & (g <= n_total))
    def _():
        cout_ref[0, :, 0:1, :] = upst[:, st2_row:st2_row + 1, :]
        cout_ref[0, :, 1:2, :] = upst[:, st1_row:st1_row + 1, :]


def _sample_a_kernel(x_ref, ck_ref, cv_ref, w_in_ref, b_in_ref, g_pre_mix_ref, sinkcol_ref,
                     slopecol_ref,
                     attn_ref, u_ref, ko_ref, vo_ref,
                     qs_buf, kn_buf, vn_buf, knt_buf, vnt_buf):
    step = pl.program_id(0)
    n_tok = ck_ref.shape[0]

    @pl.when(step == 0)
    def _():
        z = _dot(_rms(x_ref[...], g_pre_mix_ref[...]).astype(BF16), w_in_ref[...]) + b_in_ref[...]
        k = z[:, Q_W:Q_W + KV_W]
        v = z[:, Q_W + KV_W:Q_W + 2 * KV_W]
        qs_buf[...] = z[:, :Q_W] * SM_SCALE
        kn_buf[...] = k
        vn_buf[...] = v
        knt_buf[...] = k.T
        vnt_buf[...] = v.T
        u_ref[...] = z[:, Q_W + 2 * KV_W:]

    lo1 = _lane_lo((1, LANES))
    kj = lax.broadcasted_iota(jnp.int32, (N_HEADS, WINDOW), 1)
    neg_dist = (kj - WINDOW).astype(F32)
    bias = slopecol_ref[...] * neg_dist
    sink = sinkcol_ref[...]
    last_lane = lax.broadcasted_iota(jnp.int32, (KV_W, WINDOW), 1) == WINDOW - 1

    def token(j, carry):
        b = step * n_tok + j
        qrow = qs_buf[pl.ds(b, 1), :]
        kn = kn_buf[pl.ds(b, 1), :]
        vn = vn_buf[pl.ds(b, 1), :]
        kt = ck_ref[j]
        vt = cv_ref[j]
        heads = []
        for h in range(N_HEADS):
            p, e, g = h // 2, h % 2, h // (N_HEADS // N_KV_HEADS)
            blk = qrow[:, p * LANES:(p + 1) * LANES]
            src = blk if e == g else pltpu.roll(blk, HEAD_DIM, axis=1)
            keep = lo1 if g == 0 else jnp.logical_not(lo1)
            heads.append(jnp.where(keep, src, 0.0))
        qf = jnp.concatenate(heads, axis=0).astype(BF16)
        s = _dot(qf, kt.astype(BF16)) + bias
        s_self = jnp.sum(qf.astype(F32) * kn.astype(BF16).astype(F32), axis=-1, keepdims=True)
        m = jnp.maximum(jnp.maximum(jnp.max(s, axis=-1, keepdims=True), s_self), sink)
        pe = jnp.exp(s - m)
        pe_self = jnp.exp(s_self - m)
        den = jnp.sum(pe, axis=-1, keepdims=True) + pe_self + jnp.exp(sink - m)
        o = _dot_nt(pe.astype(BF16), vt.astype(BF16))
        o = o + pe_self.astype(BF16).astype(F32) * vn.astype(BF16).astype(F32)
        o = o / den
        pieces = []
        for p in range(N_HEADS // 2):
            g = (2 * p) // (N_HEADS // N_KV_HEADS)
            a, c = o[2 * p:2 * p + 1, :], o[2 * p + 1:2 * p + 2, :]
            if g == 0:
                pieces.append(jnp.where(lo1, a, pltpu.roll(c, HEAD_DIM, axis=1)))
            else:
                pieces.append(jnp.where(lo1, pltpu.roll(a, HEAD_DIM, axis=1), c))
        attn_ref[pl.ds(b, 1), :] = jnp.concatenate(pieces, axis=1)
        bring = WINDOW - 1 - b
        ko_ref[j] = jnp.where(last_lane, pltpu.roll(knt_buf[...], bring, axis=1),
                              pltpu.roll(kt, WINDOW - 1, axis=1))
        vo_ref[j] = jnp.where(last_lane, pltpu.roll(vnt_buf[...], bring, axis=1),
                              pltpu.roll(vt, WINDOW - 1, axis=1))
        return carry

    lax.fori_loop(0, n_tok, token, 0, unroll=SAMPLE_UNROLL)


def _sample_b_kernel(x_ref, attn_ref, u_ref, sp_ref, sc_ref, w_pool_ref, pscale_ref,
                     g_attn_ref, g_pool_ref, w_o_ref, g_post_mix_ref, g_pre_ffn_ref,
                     g_post_ffn_ref, wup_ref, cw_ref, cb_ref, wdown_ref,
                     y_ref, po_ref, co_ref):
    tc = FF_CHUNK
    d_ff = wdown_ref.shape[0]
    n_chunks = d_ff // tc
    u = u_ref[...]

    outs = []
    for g, w in enumerate(POOL_WINDOWS):
        cs = slice(g * POOL_GW, (g + 1) * POOL_GW)
        cur = u[:, cs]
        acc_u = cur
        for j in range(1, w):
            acc_u = acc_u + sp_ref[POOL_STATE - j, :, cs]
        outs.append(_dot((acc_u / float(w) - cur).astype(BF16), w_pool_ref[g]))
    pool = jnp.concatenate(outs, axis=1) * pscale_ref[...]
    for r in range(POOL_STATE - 1):
        po_ref[r] = sp_ref[r + 1]
    po_ref[POOL_STATE - 1] = u

    x1 = _mix_residual(x_ref[...], attn_ref[...], pool, g_attn_ref[...], g_pool_ref[...],
                       w_o_ref, g_post_mix_ref[...])
    h2 = _rms(x1, g_pre_ffn_ref[...]).astype(BF16)

    co_ref[:, 0, :] = sc_ref[:, 1, :]
    ffn = jnp.zeros(x1.shape, F32)
    for c in range(n_chunks):
        up = _dot(h2, _ff_cols(wup_ref, c, tc))
        gcols = slice(c * tc, (c + 1) * tc)
        vcols = slice(d_ff + c * tc, d_ff + (c + 1) * tc)
        co_ref[:, 1, gcols] = up[:, :tc]
        co_ref[:, 1, vcols] = up[:, tc:]
        old0 = jnp.concatenate([sc_ref[:, 0, gcols], sc_ref[:, 0, vcols]], axis=1)
        old1 = jnp.concatenate([sc_ref[:, 1, gcols], sc_ref[:, 1, vcols]], axis=1)
        cw = _ff_cols(cw_ref, c, tc)
        conv = _ff_cols(cb_ref, c, tc) + old0 * cw[0:1, :]
        conv = conv + old1 * cw[1:2, :]
        conv = conv + up * cw[2:3, :]
        ffn = ffn + _dot(_gated(conv, tc), wdown_ref[c * tc:(c + 1) * tc, :])
    y_ref[...] = x1 + _rms(ffn, g_post_ffn_ref[...])


def _vmem():
    return pl.BlockSpec(memory_space=pltpu.VMEM)


def _smem():
    return pl.BlockSpec(memory_space=pltpu.SMEM)


def _resident(shape):
    nd = len(shape)
    return pl.BlockSpec(shape, lambda *_: (0,) * nd, pipeline_mode=pl.Buffered(1))


def _unchunk_cols(a):
    n_chunks, r, tc2 = a.shape
    a = a.reshape(n_chunks, r, 2, tc2 // 2)
    return jnp.transpose(a, (1, 2, 0, 3)).reshape(r, n_chunks * tc2)


def kernel(x_prompt, x_sample, cache_k, cache_v, state_pool, state_conv, meta, w_in, b_in, sinks,
           w_pool, pool_scale, g_attn_out, g_pool_out, w_o, g_pre_mix, g_post_mix, g_pre_ffn,
           g_post_ffn, w_up, conv_w, conv_b, w_down):
    assert w_in.shape[0] == 1, "single layer"
    batch, seq, d_model = x_prompt.shape
    dec_batch = x_sample.shape[0]
    d_ff = w_down.shape[1]
    tc = FF_CHUNK
    n_chunks = d_ff // tc
    assert n_chunks * tc == d_ff and seq % SEQ_TILE == 0 and dec_batch % SAMPLE_TB == 0
    assert meta.shape[0] == N_META and N_META > POOL_STATE

    row = lambda a: a[0].reshape(1, -1)
    w_in_b = w_in[0].astype(BF16)
    b_in_r = row(b_in)
    w_pool_b = w_pool[0].astype(BF16)
    w_o_b = w_o[0].astype(BF16)
    wup_c = w_up[0].astype(BF16)
    half_value = jnp.concatenate([jnp.ones((d_ff,), F32), jnp.full((d_ff,), 0.5, F32)])
    cw_c = conv_w[0] * half_value
    cb_c = (conv_b[0] * half_value).reshape(1, -1)
    wdown_c = w_down[0].astype(BF16)
    sink_s = sinks[0]
    sink_col = sinks[0].reshape(N_HEADS, 1)
    slope_col = jnp.asarray(np.array(SLOPES, np.float32).reshape(N_HEADS, 1))
    gains = dict(pscale=row(pool_scale), g_attn=row(g_attn_out), g_pool=row(g_pool_out),
                 g_pre_mix=row(g_pre_mix), g_post_mix=row(g_post_mix),
                 g_pre_ffn=row(g_pre_ffn), g_post_ffn=row(g_post_ffn))

    x_meta = jnp.concatenate([jnp.zeros((WINDOW - N_META, d_model), F32), meta.astype(F32)], 0)
    k0, v0, u0, up0 = pl.pallas_call(
        _meta_kernel,
        out_shape=(jax.ShapeDtypeStruct((WINDOW, KV_W), F32),
                   jax.ShapeDtypeStruct((WINDOW, KV_W), F32),
                   jax.ShapeDtypeStruct((U_HIST, POOL_W), F32),
                   jax.ShapeDtypeStruct((n_chunks, CONV_STATE, 2 * tc), F32)),
        in_specs=[_vmem(), _vmem(), _vmem(), _smem()] + [_vmem()] * 9,
        out_specs=(_vmem(),) * 4,
        scratch_shapes=[pltpu.VMEM((U_HIST + WINDOW, POOL_W), F32)],
        compiler_params=pltpu.CompilerParams(vmem_limit_bytes=VMEM_LIMIT),
        name="meta",
    )(x_meta, w_in_b, b_in_r, sink_s, w_pool_b, gains["pscale"], gains["g_attn"],
      gains["g_pool"], w_o_b, gains["g_pre_mix"], gains["g_post_mix"], gains["g_pre_ffn"], wup_c)

    n_tiles = seq // SEQ_TILE
    n_total = batch * n_tiles
    mixer_tile = lambda g: jnp.minimum(g, n_total - 1)
    ffn_tile = lambda g: jnp.clip(g - 1, 0, n_total - 1)
    out_tile = lambda g: jnp.clip(g - 2, 0, n_total - 1)
    per_batch = lambda shape, tile: pl.BlockSpec(
        (1,) + shape, lambda g: (tile(g) // n_tiles,) + (0,) * len(shape))
    prompt_inputs = (
        x_prompt, k0, v0, u0, up0, w_in_b, b_in_r, sink_s, w_pool_b, gains["pscale"],
        gains["g_attn"], gains["g_pool"], w_o_b, gains["g_pre_mix"], gains["g_post_mix"],
        gains["g_pre_ffn"], gains["g_post_ffn"], wup_c, cw_c, cb_c, wdown_c)
    in_specs = [pl.BlockSpec((1, SEQ_TILE, d_model),
                             lambda g: (mixer_tile(g) // n_tiles, mixer_tile(g) % n_tiles, 0))]
    in_specs += [_smem() if a is sink_s else _resident(a.shape) for a in prompt_inputs[1:]]
    y_prompt, k_p, v_p, u_p, c_p = pl.pallas_call(
        functools.partial(_prompt_kernel, n_tiles),
        grid=(n_total + 2,),
        out_shape=(jax.ShapeDtypeStruct((batch, seq, d_model), F32),
                   jax.ShapeDtypeStruct((batch, WINDOW, KV_W), F32),
                   jax.ShapeDtypeStruct((batch, WINDOW, KV_W), F32),
                   jax.ShapeDtypeStruct((batch, U_HIST, POOL_W), F32),
                   jax.ShapeDtypeStruct((batch, n_chunks, CONV_STATE, 2 * tc), F32)),
        in_specs=in_specs,
        out_specs=(pl.BlockSpec((1, SEQ_TILE, d_model),
                                lambda g: (out_tile(g) // n_tiles, out_tile(g) % n_tiles, 0)),
                   per_batch((KV_W, WINDOW), mixer_tile), per_batch((KV_W, WINDOW), mixer_tile),
                   per_batch((U_HIST, POOL_W), mixer_tile),
                   per_batch((n_chunks, CONV_STATE, 2 * tc), ffn_tile)),
        scratch_shapes=[
            pltpu.VMEM((WINDOW + SEQ_TILE, KV_W), F32),
            pltpu.VMEM((WINDOW + SEQ_TILE, KV_W), F32),
            pltpu.VMEM((U_HIST + SEQ_TILE, POOL_W), F32),
            pltpu.VMEM((n_chunks, 2 * SUBLANES, 2 * tc), F32),
            pltpu.VMEM((2, SEQ_TILE, 2 * tc), F32),
            pltpu.VMEM((2, SEQ_TILE, tc), BF16),
            pltpu.VMEM((SEQ_TILE, d_model), F32),
            pltpu.VMEM((SEQ_TILE, Q_W), F32),
            pltpu.VMEM((3, d_model // LANES, SEQ_TILE, LANES), F32),
            pltpu.VMEM((d_model // LANES, SEQ_TILE, LANES), F32),
            pltpu.VMEM((SEQ_TILE, d_model), BF16),
            pltpu.VMEM((SEQ_TILE, Q_W), F32),
        ],
        compiler_params=pltpu.CompilerParams(
            dimension_semantics=("arbitrary",), vmem_limit_bytes=VMEM_LIMIT),
        name="prompt",
    )(*prompt_inputs)

    feat_pos = lambda c: jnp.transpose(c[0].reshape(dec_batch, WINDOW, KV_W), (0, 2, 1))
    ck, cv = feat_pos(cache_k), feat_pos(cache_v)
    xs = x_sample.reshape(dec_batch, d_model)
    tb = SAMPLE_TB
    whole = lambda shape: pl.BlockSpec(shape, lambda i: (0,) * len(shape))
    cache_spec = pl.BlockSpec((tb, KV_W, WINDOW), lambda i: (i, 0, 0))
    attn_s, u_s, k_s, v_s = pl.pallas_call(
        _sample_a_kernel,
        grid=(dec_batch // tb,),
        out_shape=(jax.ShapeDtypeStruct((dec_batch, Q_W), F32),
                   jax.ShapeDtypeStruct((dec_batch, POOL_W), F32),
                   jax.ShapeDtypeStruct((dec_batch, KV_W, WINDOW), F32),
                   jax.ShapeDtypeStruct((dec_batch, KV_W, WINDOW), F32)),
        in_specs=[whole(xs.shape), cache_spec, cache_spec, _resident(w_in_b.shape),
                  _resident(b_in_r.shape), _resident(gains["g_pre_mix"].shape),
                  _resident(sink_col.shape), _resident(slope_col.shape)],
        out_specs=(whole((dec_batch, Q_W)), whole((dec_batch, POOL_W)), cache_spec, cache_spec),
        scratch_shapes=[pltpu.VMEM((dec_batch, Q_W), F32), pltpu.VMEM((dec_batch, KV_W), F32),
                        pltpu.VMEM((dec_batch, KV_W), F32), pltpu.VMEM((KV_W, dec_batch), F32),
                        pltpu.VMEM((KV_W, dec_batch), F32)],
        compiler_params=pltpu.CompilerParams(
            dimension_semantics=("arbitrary",), vmem_limit_bytes=VMEM_LIMIT),
        name="sample_a",
    )(xs, ck, cv, w_in_b, b_in_r, gains["g_pre_mix"], sink_col, slope_col)

    sp = jnp.transpose(state_pool[0], (1, 0, 2))
    sc = state_conv[0]
    y_s, pool_s, conv_s = pl.pallas_call(
        _sample_b_kernel,
        out_shape=(jax.ShapeDtypeStruct((dec_batch, d_model), F32),
                   jax.ShapeDtypeStruct(sp.shape, F32),
                   jax.ShapeDtypeStruct(sc.shape, F32)),
        in_specs=[_vmem()] * 17,
        out_specs=(_vmem(),) * 3,
        compiler_params=pltpu.CompilerParams(vmem_limit_bytes=VMEM_LIMIT),
        name="sample_b",
    )(xs, attn_s, u_s, sp, sc, w_pool_b, gains["pscale"], gains["g_attn"], gains["g_pool"],
      w_o_b, gains["g_post_mix"], gains["g_pre_ffn"], gains["g_post_ffn"], wup_c, cw_c, cb_c,
      wdown_c)

    def pos_feat(c):
        n = c.shape[0]
        return jnp.transpose(c, (0, 2, 1)).reshape(1, n, WINDOW, N_KV_HEADS, HEAD_DIM)

    conv_p = jax.vmap(_unchunk_cols)(c_p)
    return (y_prompt,
            y_s.reshape(dec_batch, 1, d_model),
            pos_feat(k_p), pos_feat(v_p),
            u_p[:, U_HIST - POOL_STATE:, :][None],
            conv_p[None],
            pos_feat(k_s), pos_feat(v_s),
            jnp.transpose(pool_s, (1, 0, 2))[None],
            conv_s[None])
```

```python
import functools

import numpy as np
import jax
import jax.numpy as jnp
from jax import lax
from jax.experimental import pallas as pl
from jax.experimental.pallas import tpu as pltpu

F32 = jnp.float32
BF16 = jnp.bfloat16

N_META = 16
HEAD_DIM = 64
N_HEADS = 8
N_KV_HEADS = 2
WINDOW = 128
POOL_WINDOWS = (2, 4, 8, 16)
POOL_STATE = 15
CONV_STATE = 2
RMS_EPS = 1e-6
SM_SCALE = HEAD_DIM ** -0.5
SLOPES = tuple(2.0 ** (-(h + 1) * (8.0 / N_HEADS)) for h in range(N_HEADS))

LANES = 128
SUBLANES = 8
KV_W = N_KV_HEADS * HEAD_DIM
Q_W = N_HEADS * HEAD_DIM
POOL_W = 512
POOL_GW = POOL_W // len(POOL_WINDOWS)
U_HIST = 16

SEQ_TILE = 512
FF_CHUNK = 256
SAMPLE_TB = 32
SAMPLE_UNROLL = 8
VMEM_LIMIT = 56 * 1024 * 1024
TAIL_SLABS = 4
GELU_C1 = float(np.sqrt(2.0 / np.pi))
GELU_C2 = GELU_C1 * 0.044715


def _rms(x, g):
    ms = jnp.mean(x * x, axis=-1, keepdims=True)
    return x * lax.rsqrt(ms + RMS_EPS) * g


def _dot(a, b):
    return jnp.dot(a, b, preferred_element_type=F32)


def _dot_nt(a, b):
    return lax.dot_general(a, b, (((1,), (1,)), ((), ())), preferred_element_type=F32)


def _lane_lo(shape):
    return lax.broadcasted_iota(jnp.int32, shape, len(shape) - 1) < HEAD_DIM


def _attend_block(qs, k2, v2, pos_start, sink_of):
    nq, nk = qs.shape[0], k2.shape[0]
    qi = lax.broadcasted_iota(jnp.int32, (nq, nk), 0)
    kj = lax.broadcasted_iota(jnp.int32, (nq, nk), 1)
    valid = (kj >= qi) & (kj <= qi + WINDOW) & (kj >= WINDOW - pos_start)
    neg_dist = jnp.where(valid, (kj - qi - WINDOW).astype(F32), -jnp.inf)
    lo = _lane_lo((nq, LANES))
    group = N_HEADS // N_KV_HEADS
    q_heads = []
    for p in range(N_HEADS // 2):
        g = (2 * p) // group
        blk = qs[:, p * LANES:(p + 1) * LANES]
        rolled = pltpu.roll(blk, HEAD_DIM, axis=1)
        keep = lo if g == 0 else jnp.logical_not(lo)
        for e in range(2):
            q_heads.append(jnp.where(keep, blk if e == g else rolled, 0.0).astype(BF16))
    s_all = _dot_nt(jnp.concatenate(q_heads, axis=0), k2)
    probs, dens = [], []
    for h in range(N_HEADS):
        s = s_all[h * nq:(h + 1) * nq, :] + SLOPES[h] * neg_dist
        sink = sink_of(h)
        m = jnp.maximum(jnp.max(s, axis=-1, keepdims=True), sink)
        pe = jnp.exp(s - m)
        dens.append(jnp.sum(pe, axis=-1, keepdims=True) + jnp.exp(sink - m))
        probs.append(pe.astype(BF16))
    o_all = _dot(jnp.concatenate(probs, axis=0), v2)
    outs = [o_all[h * nq:(h + 1) * nq, :] / dens[h] for h in range(N_HEADS)]
    pieces = []
    for p in range(N_HEADS // 2):
        a, b = outs[2 * p], outs[2 * p + 1]
        if (2 * p) // group == 0:
            pieces.append(jnp.where(lo, a, pltpu.roll(b, HEAD_DIM, axis=1)))
        else:
            pieces.append(jnp.where(lo, pltpu.roll(a, HEAD_DIM, axis=1), b))
    return jnp.concatenate(pieces, axis=1)


def _pool_rows(ubuf, rows, w_pool_ref, pool_scale, cnt_of):
    outs = []
    for g, w in enumerate(POOL_WINDOWS):
        cs = slice(g * POOL_GW, (g + 1) * POOL_GW)
        cur = ubuf[U_HIST:U_HIST + rows, cs]
        acc = cur
        for j in range(1, w):
            acc = acc + ubuf[U_HIST - j:U_HIST - j + rows, cs]
        mean = acc / cnt_of(w)
        outs.append(_dot((mean - cur).astype(BF16), w_pool_ref[g]))
    return jnp.concatenate(outs, axis=1) * pool_scale


def _pool_rows_full(ubuf, rows, w_pool_ref, pool_scale):
    outs = []
    for g, w in enumerate(POOL_WINDOWS):
        h = ubuf[:, g * POOL_GW:(g + 1) * POOL_GW]
        s, k = h, 1
        while k < w:
            if k < SUBLANES:
                s = s + pltpu.roll(s, k, axis=0)
            else:
                s = s + jnp.concatenate([s[:k, :], s[:-k, :]], axis=0)
            k *= 2
        cur = h[U_HIST:, :]
        outs.append(_dot((s[U_HIST:, :] * (1.0 / w) - cur).astype(BF16), w_pool_ref[g]))
    return jnp.concatenate(outs, axis=1) * pool_scale


def _mix_residual(x, attn, pool, g_attn, g_pool, w_o_ref, g_post_mix):
    mixin = jnp.concatenate([_rms(attn, g_attn), _rms(pool, g_pool)], axis=1).astype(BF16)
    return x + _rms(_dot(mixin, w_o_ref[...]), g_post_mix)


def _gated(conv, tc):
    g, hv = conv[:, :tc], conv[:, tc:]
    inner = g * (GELU_C1 + GELU_C2 * (g * g))
    return (g * (1.0 + jnp.tanh(inner)) * hv).astype(BF16)


def _ff_cols(ref, c, tc):
    d_ff = ref.shape[-1] // 2
    return jnp.concatenate([ref[:, c * tc:(c + 1) * tc],
                            ref[:, d_ff + c * tc:d_ff + (c + 1) * tc]], axis=1)


def _row_slabs(rows):
    na = rows // SUBLANES - TAIL_SLABS
    assert na % 4 == 0 and (na // 4) % 2 == 1
    slabs = [(v, na) for v in range(na)]
    slabs += [(SUBLANES * na + v, TAIL_SLABS) for v in range(TAIL_SLABS)]
    return slabs, na


def _shifted_rows(ub, prev, na):
    S = SUBLANES
    width = ub.shape[1]
    sub = lax.broadcasted_iota(jnp.int32, (S, width), 0)
    slab = lambda i: ub[i * S:(i + 1) * S, :]

    def wrap(x, y):
        return pltpu.roll(jnp.where(sub == S - 1, y, x), 1, axis=0)

    a1 = wrap(slab(na - 1), prev[S:2 * S, :])
    a2 = wrap(slab(na - 2), prev[0:S, :])
    b1 = wrap(slab(na + 3), slab(na - 1))
    b2 = wrap(slab(na + 2), slab(na - 2))
    s1 = jnp.concatenate([a1, ub[0:(na - 1) * S, :], b1, ub[na * S:(na + 3) * S, :]], axis=0)
    s2 = jnp.concatenate([a2, a1, ub[0:(na - 2) * S, :], b2, b1, ub[na * S:(na + 2) * S, :]],
                         axis=0)
    return s1, s2


def _meta_kernel(x_ref, w_in_ref, b_in_ref, sink_ref, w_pool_ref, pscale_ref, g_attn_ref,
                 g_pool_ref, w_o_ref, g_pre_mix_ref, g_post_mix_ref, g_pre_ffn_ref, wup_ref,
                 k0_ref, v0_ref, u0_ref, up0_ref, ubuf):
    rows = x_ref.shape[0]
    pos0 = N_META - rows
    x = x_ref[...]
    z = _dot(_rms(x, g_pre_mix_ref[...]).astype(BF16), w_in_ref[...]) + b_in_ref[...]
    k = z[:, Q_W:Q_W + KV_W]
    v = z[:, Q_W + KV_W:Q_W + 2 * KV_W]
    u = z[:, Q_W + 2 * KV_W:]
    pos = pos0 + lax.broadcasted_iota(jnp.int32, (rows, 1), 0)
    zeros_kv = jnp.zeros((WINDOW, KV_W), BF16)
    k2 = jnp.concatenate([zeros_kv, k.astype(BF16)], axis=0)
    v2 = jnp.concatenate([zeros_kv, v.astype(BF16)], axis=0)
    attn = _attend_block(z[:, :Q_W] * SM_SCALE, k2, v2, pos0, lambda h: sink_ref[h])

    ubuf[0:U_HIST, :] = jnp.zeros((U_HIST, POOL_W), F32)
    ubuf[U_HIST:, :] = jnp.where(pos >= 0, u, 0.0)
    cnt_of = lambda w: jnp.clip(pos + 1, 1, w).astype(F32)
    pool = _pool_rows(ubuf, rows, w_pool_ref, pscale_ref[...], cnt_of)

    x1 = _mix_residual(x, attn, pool, g_attn_ref[...], g_pool_ref[...], w_o_ref,
                       g_post_mix_ref[...])
    h2 = _rms(x1, g_pre_ffn_ref[...]).astype(BF16)
    k0_ref[...] = k
    v0_ref[...] = v
    u0_ref[...] = u[rows - U_HIST:, :]
    tail = 2 * SUBLANES
    tc = up0_ref.shape[2] // 2
    for c in range(up0_ref.shape[0]):
        up_tail = _dot(h2[rows - tail:, :], _ff_cols(wup_ref, c, tc))
        up0_ref[c] = up_tail[tail - CONV_STATE:, :]


def _prompt_kernel(n_tiles, x_ref, k0_ref, v0_ref, u0_ref, up0_ref, w_in_ref, b_in_ref,
                   sink_ref, w_pool_ref, pscale_ref, g_attn_ref, g_pool_ref, w_o_ref,
                   g_pre_mix_ref, g_post_mix_ref, g_pre_ffn_ref, g_post_ffn_ref, wup_ref, cw_ref,
                   cb_ref, wdown_ref,
                   y_ref, kout_ref, vout_ref, uout_ref, cout_ref,
                   kbuf, vbuf, ubuf, upst, upbuf, abuf, acc, attn_buf, xbuf, obuf, h2buf, qsbuf):
    g = pl.program_id(0)
    n_total = pl.num_programs(0) - 2
    rows = x_ref.shape[1]
    n_chunks, _, tc2 = upst.shape
    tc = tc2 // 2
    n_col = x_ref.shape[2] // LANES
    S = SUBLANES
    slabs, na = _row_slabs(rows)
    st2_row, st1_row = S - 1, 2 * S - 1

    tf = lax.rem(jnp.minimum(g, n_total - 1), n_tiles)
    tj = lax.rem(jnp.clip(g - 1, 0, n_total - 1), n_tiles)
    wslot = lax.rem(g, 3)
    oslot = lax.rem(g + 1, 3)

    @pl.when(tf == 0)
    def _():
        kbuf[0:WINDOW, :] = k0_ref[...]
        vbuf[0:WINDOW, :] = v0_ref[...]
        ubuf[0:U_HIST, :] = u0_ref[...]

    @pl.when(tj == 0)
    def _():
        upst[...] = jnp.zeros(upst.shape, F32)
        upst[:, st2_row:st2_row + 1, :] = up0_ref[:, 0:1, :]
        upst[:, st1_row:st1_row + 1, :] = up0_ref[:, 1:2, :]

    @pl.when(g == 0)
    def _():
        xbuf[...] = jnp.zeros(xbuf.shape, F32)
        acc[...] = jnp.zeros(acc.shape, F32)
        h2buf[...] = jnp.zeros(h2buf.shape, BF16)

    def mixer_in():
        z = _dot(_rms(x_ref[0], g_pre_mix_ref[...]).astype(BF16), w_in_ref[...]) + b_in_ref[...]
        kbuf[WINDOW:, :] = z[:, Q_W:Q_W + KV_W]
        vbuf[WINDOW:, :] = z[:, Q_W + KV_W:Q_W + 2 * KV_W]
        ubuf[U_HIST:, :] = z[:, Q_W + 2 * KV_W:]
        qsbuf[...] = z[:, :Q_W] * SM_SCALE

    def mixer_attend(i):
        r0 = i * WINDOW
        k2 = kbuf[r0:r0 + 2 * WINDOW, :].astype(BF16)
        v2 = vbuf[r0:r0 + 2 * WINDOW, :].astype(BF16)
        attn_buf[r0:r0 + WINDOW, :] = _attend_block(
            qsbuf[r0:r0 + WINDOW, :], k2, v2, N_META + tf * rows + r0, lambda h: sink_ref[h])

    def mixer_out():
        pool = _pool_rows_full(ubuf, rows, w_pool_ref, pscale_ref[...])
        x1 = _mix_residual(x_ref[0], attn_buf[...], pool, g_attn_ref[...], g_pool_ref[...],
                           w_o_ref, g_post_mix_ref[...])
        for j in range(n_col):
            xbuf[wslot, j] = x1[:, j * LANES:(j + 1) * LANES]
        kbuf[0:WINDOW, :] = kbuf[rows:rows + WINDOW, :]
        vbuf[0:WINDOW, :] = vbuf[rows:rows + WINDOW, :]
        ubuf[0:U_HIST, :] = ubuf[rows:rows + U_HIST, :]
        h2buf[lax.rem(g, 2)] = _rms(x1_slab_order(wslot), g_pre_ffn_ref[...]).astype(BF16)

    def x1_slab_order(slot):
        return jnp.concatenate(
            [jnp.concatenate([xbuf[slot, j, pl.ds(start, S, stride=stride), :]
                              for j in range(n_col)], axis=1) for start, stride in slabs], axis=0)

    def up_project(c):
        upbuf[c % 2] = _dot(h2buf[lax.rem(g + 1, 2)], _ff_cols(wup_ref, c, tc))

    def activate(c):
        ub = upbuf.at[c % 2]
        s1, s2 = _shifted_rows(ub, upst[c], na)
        upst[c] = ub[rows - 2 * S:rows, :]
        cw = _ff_cols(cw_ref, c, tc)
        conv = _ff_cols(cb_ref, c, tc) + s2 * cw[0:1, :]
        conv = conv + s1 * cw[1:2, :]
        conv = conv + ub[...] * cw[2:3, :]
        abuf[c % 2] = _gated(conv, tc)

    def down_project(c):
        part = _dot(abuf[c % 2], wdown_ref[c * tc:(c + 1) * tc, :])
        if c == 0:
            acc[...] = part
        else:
            acc[...] += part

    def ffn_stage(c):
        if c + 1 < n_chunks:
            up_project(c + 1)
        if c < n_chunks:
            activate(c)
        if c >= 1:
            down_project(c - 1)

    def ffn_out():
        y = x1_slab_order(oslot) + _rms(acc[...], g_post_ffn_ref[...])
        for i, (start, stride) in enumerate(slabs):
            for j in range(n_col):
                obuf[j, pl.ds(start, S, stride=stride), :] = y[i * S:(i + 1) * S,
                                                              j * LANES:(j + 1) * LANES]
        y_ref[0] = jnp.concatenate([obuf[j] for j in range(n_col)], axis=1)

    ffn_stages = [lambda c=c: ffn_stage(c) for c in range(n_chunks + 1)]
    mixer_stages = [mixer_in] + [lambda i=i: mixer_attend(i) for i in range(rows // WINDOW)]
    mixer_stages.append(mixer_out)

    ffn_out()
    up_project(0)
    emitted = 0
    for c, stage in enumerate(ffn_stages):
        due = min(len(mixer_stages), ((c + 2) * len(mixer_stages)) // len(ffn_stages))
        while emitted < due:
            mixer_stages[emitted]()
            emitted += 1
        stage()

    @pl.when((tf == n_tiles - 1) & (g < n_total))
    def _():
        kout_ref[0] = kbuf[0:WINDOW, :].T
        vout_ref[0] = vbuf[0:WINDOW, :].T
        uout_ref[0] = ubuf[0:U_HIST, :]

    @pl.when((tj == n_tiles - 1) & (g >= 1) & (g <= n_total))
    def _():
        cout_ref[0, :, 0:1, :] = upst[:, st2_row:st2_row + 1, :]
        cout_ref[0, :, 1:2, :] = upst[:, st1_row:st1_row + 1, :]


def _sample_a_kernel(x_ref, ck_ref, cv_ref, w_in_ref, b_in_ref, g_pre_mix_ref, sinkcol_ref,
                     slopecol_ref,
                     attn_ref, u_ref, ko_ref, vo_ref,
                     qs_buf, kn_buf, vn_buf, knt_buf, vnt_buf):
    step = pl.program_id(0)
    n_tok = ck_ref.shape[0]

    @pl.when(step == 0)
    def _():
        z = _dot(_rms(x_ref[...], g_pre_mix_ref[...]).astype(BF16), w_in_ref[...]) + b_in_ref[...]
        k = z[:, Q_W:Q_W + KV_W]
        v = z[:, Q_W + KV_W:Q_W + 2 * KV_W]
        qs_buf[...] = z[:, :Q_W] * SM_SCALE
        kn_buf[...] = k
        vn_buf[...] = v
        knt_buf[...] = k.T
        vnt_buf[...] = v.T
        u_ref[...] = z[:, Q_W + 2 * KV_W:]

    lo1 = _lane_lo((1, LANES))
    kj = lax.broadcasted_iota(jnp.int32, (N_HEADS, WINDOW), 1)
    neg_dist = (kj - WINDOW).astype(F32)
    bias = slopecol_ref[...] * neg_dist
    sink = sinkcol_ref[...]
    last_lane = lax.broadcasted_iota(jnp.int32, (KV_W, WINDOW), 1) == WINDOW - 1

    def token(j, carry):
        b = step * n_tok + j
        qrow = qs_buf[pl.ds(b, 1), :]
        kn = kn_buf[pl.ds(b, 1), :]
        vn = vn_buf[pl.ds(b, 1), :]
        kt = ck_ref[j]
        vt = cv_ref[j]
        heads = []
        for h in range(N_HEADS):
            p, e, g = h // 2, h % 2, h // (N_HEADS // N_KV_HEADS)
            blk = qrow[:, p * LANES:(p + 1) * LANES]
            src = blk if e == g else pltpu.roll(blk, HEAD_DIM, axis=1)
            keep = lo1 if g == 0 else jnp.logical_not(lo1)
            heads.append(jnp.where(keep, src, 0.0))
        qf = jnp.concatenate(heads, axis=0).astype(BF16)
        s = _dot(qf, kt.astype(BF16)) + bias
        s_self = jnp.sum(qf.astype(F32) * kn.astype(BF16).astype(F32), axis=-1, keepdims=True)
        m = jnp.maximum(jnp.maximum(jnp.max(s, axis=-1, keepdims=True), s_self), sink)
        pe = jnp.exp(s - m)
        pe_self = jnp.exp(s_self - m)
        den = jnp.sum(pe, axis=-1, keepdims=True) + pe_self + jnp.exp(sink - m)
        o = _dot_nt(pe.astype(BF16), vt.astype(BF16))
        o = o + pe_self.astype(BF16).astype(F32) * vn.astype(BF16).astype(F32)
        o = o / den
        pieces = []
        for p in range(N_HEADS // 2):
            g = (2 * p) // (N_HEADS // N_KV_HEADS)
            a, c = o[2 * p:2 * p + 1, :], o[2 * p + 1:2 * p + 2, :]
            if g == 0:
                pieces.append(jnp.where(lo1, a, pltpu.roll(c, HEAD_DIM, axis=1)))
            else:
                pieces.append(jnp.where(lo1, pltpu.roll(a, HEAD_DIM, axis=1), c))
        attn_ref[pl.ds(b, 1), :] = jnp.concatenate(pieces, axis=1)
        bring = WINDOW - 1 - b
        ko_ref[j] = jnp.where(last_lane, pltpu.roll(knt_buf[...], bring, axis=1),
                              pltpu.roll(kt, WINDOW - 1, axis=1))
        vo_ref[j] = jnp.where(last_lane, pltpu.roll(vnt_buf[...], bring, axis=1),
                              pltpu.roll(vt, WINDOW - 1, axis=1))
        return carry

    lax.fori_loop(0, n_tok, token, 0, unroll=SAMPLE_UNROLL)


def _sample_b_kernel(x_ref, attn_ref, u_ref, sp_ref, sc_ref, w_pool_ref, pscale_ref,
                     g_attn_ref, g_pool_ref, w_o_ref, g_post_mix_ref, g_pre_ffn_ref,
                     g_post_ffn_ref, wup_ref, cw_ref, cb_ref, wdown_ref,
                     y_ref, po_ref, co_ref):
    tc = FF_CHUNK
    d_ff = wdown_ref.shape[0]
    n_chunks = d_ff // tc
    u = u_ref[...]

    outs = []
    for g, w in enumerate(POOL_WINDOWS):
        cs = slice(g * POOL_GW, (g + 1) * POOL_GW)
        cur = u[:, cs]
        acc_u = cur
        for j in range(1, w):
            acc_u = acc_u + sp_ref[POOL_STATE - j, :, cs]
        outs.append(_dot((acc_u / float(w) - cur).astype(BF16), w_pool_ref[g]))
    pool = jnp.concatenate(outs, axis=1) * pscale_ref[...]
    for r in range(POOL_STATE - 1):
        po_ref[r] = sp_ref[r + 1]
    po_ref[POOL_STATE - 1] = u

    x1 = _mix_residual(x_ref[...], attn_ref[...], pool, g_attn_ref[...], g_pool_ref[...],
                       w_o_ref, g_post_mix_ref[...])
    h2 = _rms(x1, g_pre_ffn_ref[...]).astype(BF16)

    co_ref[:, 0, :] = sc_ref[:, 1, :]
    ffn = jnp.zeros(x1.shape, F32)
    for c in range(n_chunks):
        up = _dot(h2, _ff_cols(wup_ref, c, tc))
        gcols = slice(c * tc, (c + 1) * tc)
        vcols = slice(d_ff + c * tc, d_ff + (c + 1) * tc)
        co_ref[:, 1, gcols] = up[:, :tc]
        co_ref[:, 1, vcols] = up[:, tc:]
        old0 = jnp.concatenate([sc_ref[:, 0, gcols], sc_ref[:, 0, vcols]], axis=1)
        old1 = jnp.concatenate([sc_ref[:, 1, gcols], sc_ref[:, 1, vcols]], axis=1)
        cw = _ff_cols(cw_ref, c, tc)
        conv = _ff_cols(cb_ref, c, tc) + old0 * cw[0:1, :]
        conv = conv + old1 * cw[1:2, :]
        conv = conv + up * cw[2:3, :]
        ffn = ffn + _dot(_gated(conv, tc), wdown_ref[c * tc:(c + 1) * tc, :])
    y_ref[...] = x1 + _rms(ffn, g_post_ffn_ref[...])


def _vmem():
    return pl.BlockSpec(memory_space=pltpu.VMEM)


def _smem():
    return pl.BlockSpec(memory_space=pltpu.SMEM)


def _resident(shape):
    nd = len(shape)
    return pl.BlockSpec(shape, lambda *_: (0,) * nd, pipeline_mode=pl.Buffered(1))


def _unchunk_cols(a):
    n_chunks, r, tc2 = a.shape
    a = a.reshape(n_chunks, r, 2, tc2 // 2)
    return jnp.transpose(a, (1, 2, 0, 3)).reshape(r, n_chunks * tc2)


def kernel(x_prompt, x_sample, cache_k, cache_v, state_pool, state_conv, meta, w_in, b_in, sinks,
           w_pool, pool_scale, g_attn_out, g_pool_out, w_o, g_pre_mix, g_post_mix, g_pre_ffn,
           g_post_ffn, w_up, conv_w, conv_b, w_down):
    assert w_in.shape[0] == 1, "single layer"
    batch, seq, d_model = x_prompt.shape
    dec_batch = x_sample.shape[0]
    d_ff = w_down.shape[1]
    tc = FF_CHUNK
    n_chunks = d_ff // tc
    assert n_chunks * tc == d_ff and seq % SEQ_TILE == 0 and dec_batch % SAMPLE_TB == 0
    assert meta.shape[0] == N_META and N_META > POOL_STATE

    row = lambda a: a[0].reshape(1, -1)
    w_in_b = w_in[0].astype(BF16)
    b_in_r = row(b_in)
    w_pool_b = w_pool[0].astype(BF16)
    w_o_b = w_o[0].astype(BF16)
    wup_c = w_up[0].astype(BF16)
    half_value = jnp.concatenate([jnp.ones((d_ff,), F32), jnp.full((d_ff,), 0.5, F32)])
    cw_c = conv_w[0] * half_value
    cb_c = (conv_b[0] * half_value).reshape(1, -1)
    wdown_c = w_down[0].astype(BF16)
    sink_s = sinks[0]
    sink_col = sinks[0].reshape(N_HEADS, 1)
    slope_col = jnp.asarray(np.array(SLOPES, np.float32).reshape(N_HEADS, 1))
    gains = dict(pscale=row(pool_scale), g_attn=row(g_attn_out), g_pool=row(g_pool_out),
                 g_pre_mix=row(g_pre_mix), g_post_mix=row(g_post_mix),
                 g_pre_ffn=row(g_pre_ffn), g_post_ffn=row(g_post_ffn))

    x_meta = jnp.concatenate([jnp.zeros((WINDOW - N_META, d_model), F32), meta.astype(F32)], 0)
    k0, v0, u0, up0 = pl.pallas_call(
        _meta_kernel,
        out_shape=(jax.ShapeDtypeStruct((WINDOW, KV_W), F32),
                   jax.ShapeDtypeStruct((WINDOW, KV_W), F32),
                   jax.ShapeDtypeStruct((U_HIST, POOL_W), F32),
                   jax.ShapeDtypeStruct((n_chunks, CONV_STATE, 2 * tc), F32)),
        in_specs=[_vmem(), _vmem(), _vmem(), _smem()] + [_vmem()] * 9,
        out_specs=(_vmem(),) * 4,
        scratch_shapes=[pltpu.VMEM((U_HIST + WINDOW, POOL_W), F32)],
        compiler_params=pltpu.CompilerParams(vmem_limit_bytes=VMEM_LIMIT),
        name="meta",
    )(x_meta, w_in_b, b_in_r, sink_s, w_pool_b, gains["pscale"], gains["g_attn"],
      gains["g_pool"], w_o_b, gains["g_pre_mix"], gains["g_post_mix"], gains["g_pre_ffn"], wup_c)

    n_tiles = seq // SEQ_TILE
    n_total = batch * n_tiles
    mixer_tile = lambda g: jnp.minimum(g, n_total - 1)
    ffn_tile = lambda g: jnp.clip(g - 1, 0, n_total - 1)
    out_tile = lambda g: jnp.clip(g - 2, 0, n_total - 1)
    per_batch = lambda shape, tile: pl.BlockSpec(
        (1,) + shape, lambda g: (tile(g) // n_tiles,) + (0,) * len(shape))
    prompt_inputs = (
        x_prompt, k0, v0, u0, up0, w_in_b, b_in_r, sink_s, w_pool_b, gains["pscale"],
        gains["g_attn"], gains["g_pool"], w_o_b, gains["g_pre_mix"], gains["g_post_mix"],
        gains["g_pre_ffn"], gains["g_post_ffn"], wup_c, cw_c, cb_c, wdown_c)
    in_specs = [pl.BlockSpec((1, SEQ_TILE, d_model),
                             lambda g: (mixer_tile(g) // n_tiles, mixer_tile(g) % n_tiles, 0))]
    in_specs += [_smem() if a is sink_s else _resident(a.shape) for a in prompt_inputs[1:]]
    y_prompt, k_p, v_p, u_p, c_p = pl.pallas_call(
        functools.partial(_prompt_kernel, n_tiles),
        grid=(n_total + 2,),
        out_shape=(jax.ShapeDtypeStruct((batch, seq, d_model), F32),
                   jax.ShapeDtypeStruct((batch, WINDOW, KV_W), F32),
                   jax.ShapeDtypeStruct((batch, WINDOW, KV_W), F32),
                   jax.ShapeDtypeStruct((batch, U_HIST, POOL_W), F32),
                   jax.ShapeDtypeStruct((batch, n_chunks, CONV_STATE, 2 * tc), F32)),
        in_specs=in_specs,
        out_specs=(pl.BlockSpec((1, SEQ_TILE, d_model),
                                lambda g: (out_tile(g) // n_tiles, out_tile(g) % n_tiles, 0)),
                   per_batch((KV_W, WINDOW), mixer_tile), per_batch((KV_W, WINDOW), mixer_tile),
                   per_batch((U_HIST, POOL_W), mixer_tile),
                   per_batch((n_chunks, CONV_STATE, 2 * tc), ffn_tile)),
        scratch_shapes=[
            pltpu.VMEM((WINDOW + SEQ_TILE, KV_W), F32),
            pltpu.VMEM((WINDOW + SEQ_TILE, KV_W), F32),
            pltpu.VMEM((U_HIST + SEQ_TILE, POOL_W), F32),
            pltpu.VMEM((n_chunks, 2 * SUBLANES, 2 * tc), F32),
            pltpu.VMEM((2, SEQ_TILE, 2 * tc), F32),
            pltpu.VMEM((2, SEQ_TILE, tc), BF16),
            pltpu.VMEM((SEQ_TILE, d_model), F32),
            pltpu.VMEM((SEQ_TILE, Q_W), F32),
            pltpu.VMEM((3, d_model // LANES, SEQ_TILE, LANES), F32),
            pltpu.VMEM((d_model // LANES, SEQ_TILE, LANES), F32),
            pltpu.VMEM((2, SEQ_TILE, d_model), BF16),
            pltpu.VMEM((SEQ_TILE, Q_W), F32),
        ],
        compiler_params=pltpu.CompilerParams(
            dimension_semantics=("arbitrary",), vmem_limit_bytes=VMEM_LIMIT),
        name="prompt",
    )(*prompt_inputs)

    feat_pos = lambda c: jnp.transpose(c[0].reshape(dec_batch, WINDOW, KV_W), (0, 2, 1))
    ck, cv = feat_pos(cache_k), feat_pos(cache_v)
    xs = x_sample.reshape(dec_batch, d_model)
    tb = SAMPLE_TB
    whole = lambda shape: pl.BlockSpec(shape, lambda i: (0,) * len(shape))
    cache_spec = pl.BlockSpec((tb, KV_W, WINDOW), lambda i: (i, 0, 0))
    attn_s, u_s, k_s, v_s = pl.pallas_call(
        _sample_a_kernel,
        grid=(dec_batch // tb,),
        out_shape=(jax.ShapeDtypeStruct((dec_batch, Q_W), F32),
                   jax.ShapeDtypeStruct((dec_batch, POOL_W), F32),
                   jax.ShapeDtypeStruct((dec_batch, KV_W, WINDOW), F32),
                   jax.ShapeDtypeStruct((dec_batch, KV_W, WINDOW), F32)),
        in_specs=[whole(xs.shape), cache_spec, cache_spec, _resident(w_in_b.shape),
                  _resident(b_in_r.shape), _resident(gains["g_pre_mix"].shape),
                  _resident(sink_col.shape), _resident(slope_col.shape)],
        out_specs=(whole((dec_batch, Q_W)), whole((dec_batch, POOL_W)), cache_spec, cache_spec),
        scratch_shapes=[pltpu.VMEM((dec_batch, Q_W), F32), pltpu.VMEM((dec_batch, KV_W), F32),
                        pltpu.VMEM((dec_batch, KV_W), F32), pltpu.VMEM((KV_W, dec_batch), F32),
                        pltpu.VMEM((KV_W, dec_batch), F32)],
        compiler_params=pltpu.CompilerParams(
            dimension_semantics=("arbitrary",), vmem_limit_bytes=VMEM_LIMIT),
        name="sample_a",
    )(xs, ck, cv, w_in_b, b_in_r, gains["g_pre_mix"], sink_col, slope_col)

    sp = jnp.transpose(state_pool[0], (1, 0, 2))
    sc = state_conv[0]
    y_s, pool_s, conv_s = pl.pallas_call(
        _sample_b_kernel,
        out_shape=(jax.ShapeDtypeStruct((dec_batch, d_model), F32),
                   jax.ShapeDtypeStruct(sp.shape, F32),
                   jax.ShapeDtypeStruct(sc.shape, F32)),
        in_specs=[_vmem()] * 17,
        out_specs=(_vmem(),) * 3,
        compiler_params=pltpu.CompilerParams(vmem_limit_bytes=VMEM_LIMIT),
        name="sample_b",
    )(xs, attn_s, u_s, sp, sc, w_pool_b, gains["pscale"], gains["g_attn"], gains["g_pool"],
      w_o_b, gains["g_post_mix"], gains["g_pre_ffn"], gains["g_post_ffn"], wup_c, cw_c, cb_c,
      wdown_c)

    def pos_feat(c):
        n = c.shape[0]
        return jnp.transpose(c, (0, 2, 1)).reshape(1, n, WINDOW, N_KV_HEADS, HEAD_DIM)

    conv_p = jax.vmap(_unchunk_cols)(c_p)
    return (y_prompt,
            y_s.reshape(dec_batch, 1, d_model),
            pos_feat(k_p), pos_feat(v_p),
            u_p[:, U_HIST - POOL_STATE:, :][None],
            conv_p[None],
            pos_feat(k_s), pos_feat(v_s),
            jnp.transpose(pool_s, (1, 0, 2))[None],
            conv_s[None])
```

```python
import functools

import numpy as np
import jax
import jax.numpy as jnp
from jax import lax
from jax.experimental import pallas as pl
from jax.experimental.pallas import tpu as pltpu

F32 = jnp.float32
BF16 = jnp.bfloat16

N_META = 16
HEAD_DIM = 64
N_HEADS = 8
N_KV_HEADS = 2
WINDOW = 128
POOL_WINDOWS = (2, 4, 8, 16)
POOL_STATE = 15
CONV_STATE = 2
RMS_EPS = 1e-6
SM_SCALE = HEAD_DIM ** -0.5
SLOPES = tuple(2.0 ** (-(h + 1) * (8.0 / N_HEADS)) for h in range(N_HEADS))

LANES = 128
SUBLANES = 8
KV_W = N_KV_HEADS * HEAD_DIM
Q_W = N_HEADS * HEAD_DIM
POOL_W = 512
POOL_GW = POOL_W // len(POOL_WINDOWS)
U_HIST = 16

SEQ_TILE = 512
FF_CHUNK = 256
SAMPLE_TB = 32
SAMPLE_UNROLL = 8
VMEM_LIMIT = 56 * 1024 * 1024
TAIL_SLABS = 4
GELU_C1 = float(np.sqrt(2.0 / np.pi))
GELU_C2 = GELU_C1 * 0.044715


def _rms(x, g):
    ms = jnp.mean(x * x, axis=-1, keepdims=True)
    return x * lax.rsqrt(ms + RMS_EPS) * g


def _dot(a, b):
    return jnp.dot(a, b, preferred_element_type=F32)


def _dot_nt(a, b):
    return lax.dot_general(a, b, (((1,), (1,)), ((), ())), preferred_element_type=F32)


def _lane_lo(shape):
    return lax.broadcasted_iota(jnp.int32, shape, len(shape) - 1) < HEAD_DIM


def _attend_block(qs, k2, v2, pos_start, sink_of):
    nq, nk = qs.shape[0], k2.shape[0]
    qi = lax.broadcasted_iota(jnp.int32, (nq, nk), 0)
    kj = lax.broadcasted_iota(jnp.int32, (nq, nk), 1)
    valid = (kj >= qi) & (kj <= qi + WINDOW) & (kj >= WINDOW - pos_start)
    neg_dist = jnp.where(valid, (kj - qi - WINDOW).astype(F32), -jnp.inf)
    lo = _lane_lo((nq, LANES))
    group = N_HEADS // N_KV_HEADS
    q_heads = []
    for p in range(N_HEADS // 2):
        g = (2 * p) // group
        blk = qs[:, p * LANES:(p + 1) * LANES]
        rolled = pltpu.roll(blk, HEAD_DIM, axis=1)
        keep = lo if g == 0 else jnp.logical_not(lo)
        for e in range(2):
            q_heads.append(jnp.where(keep, blk if e == g else rolled, 0.0).astype(BF16))
    s_all = _dot_nt(jnp.concatenate(q_heads, axis=0), k2)
    probs, dens = [], []
    for h in range(N_HEADS):
        s = s_all[h * nq:(h + 1) * nq, :] + SLOPES[h] * neg_dist
        sink = sink_of(h)
        m = jnp.maximum(jnp.max(s, axis=-1, keepdims=True), sink)
        pe = jnp.exp(s - m)
        dens.append(jnp.sum(pe, axis=-1, keepdims=True) + jnp.exp(sink - m))
        probs.append(pe.astype(BF16))
    o_all = _dot(jnp.concatenate(probs, axis=0), v2)
    outs = [o_all[h * nq:(h + 1) * nq, :] / dens[h] for h in range(N_HEADS)]
    pieces = []
    for p in range(N_HEADS // 2):
        a, b = outs[2 * p], outs[2 * p + 1]
        if (2 * p) // group == 0:
            pieces.append(jnp.where(lo, a, pltpu.roll(b, HEAD_DIM, axis=1)))
        else:
            pieces.append(jnp.where(lo, pltpu.roll(a, HEAD_DIM, axis=1), b))
    return jnp.concatenate(pieces, axis=1)


def _pool_rows(ubuf, rows, w_pool_ref, pool_scale, cnt_of):
    outs = []
    for g, w in enumerate(POOL_WINDOWS):
        cs = slice(g * POOL_GW, (g + 1) * POOL_GW)
        cur = ubuf[U_HIST:U_HIST + rows, cs]
        acc = cur
        for j in range(1, w):
            acc = acc + ubuf[U_HIST - j:U_HIST - j + rows, cs]
        mean = acc / cnt_of(w)
        outs.append(_dot((mean - cur).astype(BF16), w_pool_ref[g]))
    return jnp.concatenate(outs, axis=1) * pool_scale


def _pool_rows_full(ubuf, rows, w_pool_ref, pool_scale):
    outs = []
    for g, w in enumerate(POOL_WINDOWS):
        h = ubuf[:, g * POOL_GW:(g + 1) * POOL_GW]
        s, k = h, 1
        while k < w:
            if k < SUBLANES:
                s = s + pltpu.roll(s, k, axis=0)
            else:
                s = s + jnp.concatenate([s[:k, :], s[:-k, :]], axis=0)
            k *= 2
        cur = h[U_HIST:, :]
        outs.append(_dot((s[U_HIST:, :] * (1.0 / w) - cur).astype(BF16), w_pool_ref[g]))
    return jnp.concatenate(outs, axis=1) * pool_scale


def _mix_residual(x, attn, pool, g_attn, g_pool, w_o_ref, g_post_mix):
    mixin = jnp.concatenate([_rms(attn, g_attn), _rms(pool, g_pool)], axis=1).astype(BF16)
    return x + _rms(_dot(mixin, w_o_ref[...]), g_post_mix)


def _gated(conv, tc):
    g, hv = conv[:, :tc], conv[:, tc:]
    inner = g * (GELU_C1 + GELU_C2 * (g * g))
    return (g * (1.0 + jnp.tanh(inner)) * hv).astype(BF16)


def _ff_cols(ref, c, tc):
    d_ff = ref.shape[-1] // 2
    return jnp.concatenate([ref[:, c * tc:(c + 1) * tc],
                            ref[:, d_ff + c * tc:d_ff + (c + 1) * tc]], axis=1)


def _row_slabs(rows):
    na = rows // SUBLANES - TAIL_SLABS
    assert na % 4 == 0 and (na // 4) % 2 == 1
    slabs = [(v, na) for v in range(na)]
    slabs += [(SUBLANES * na + v, TAIL_SLABS) for v in range(TAIL_SLABS)]
    return slabs, na


def _shifted_rows(ub, prev, na):
    S = SUBLANES
    width = ub.shape[1]
    sub = lax.broadcasted_iota(jnp.int32, (S, width), 0)
    slab = lambda i: ub[i * S:(i + 1) * S, :]

    def wrap(x, y):
        return pltpu.roll(jnp.where(sub == S - 1, y, x), 1, axis=0)

    a1 = wrap(slab(na - 1), prev[S:2 * S, :])
    a2 = wrap(slab(na - 2), prev[0:S, :])
    b1 = wrap(slab(na + 3), slab(na - 1))
    b2 = wrap(slab(na + 2), slab(na - 2))
    s1 = jnp.concatenate([a1, ub[0:(na - 1) * S, :], b1, ub[na * S:(na + 3) * S, :]], axis=0)
    s2 = jnp.concatenate([a2, a1, ub[0:(na - 2) * S, :], b2, b1, ub[na * S:(na + 2) * S, :]],
                         axis=0)
    return s1, s2


def _meta_kernel(x_ref, w_in_ref, b_in_ref, sink_ref, w_pool_ref, pscale_ref, g_attn_ref,
                 g_pool_ref, w_o_ref, g_pre_mix_ref, g_post_mix_ref, g_pre_ffn_ref, wup_ref,
                 k0_ref, v0_ref, u0_ref, up0_ref, ubuf):
    rows = x_ref.shape[0]
    pos0 = N_META - rows
    x = x_ref[...]
    z = _dot(_rms(x, g_pre_mix_ref[...]).astype(BF16), w_in_ref[...]) + b_in_ref[...]
    k = z[:, Q_W:Q_W + KV_W]
    v = z[:, Q_W + KV_W:Q_W + 2 * KV_W]
    u = z[:, Q_W + 2 * KV_W:]
    pos = pos0 + lax.broadcasted_iota(jnp.int32, (rows, 1), 0)
    zeros_kv = jnp.zeros((WINDOW, KV_W), BF16)
    k2 = jnp.concatenate([zeros_kv, k.astype(BF16)], axis=0)
    v2 = jnp.concatenate([zeros_kv, v.astype(BF16)], axis=0)
    attn = _attend_block(z[:, :Q_W] * SM_SCALE, k2, v2, pos0, lambda h: sink_ref[h])

    ubuf[0:U_HIST, :] = jnp.zeros((U_HIST, POOL_W), F32)
    ubuf[U_HIST:, :] = jnp.where(pos >= 0, u, 0.0)
    cnt_of = lambda w: jnp.clip(pos + 1, 1, w).astype(F32)
    pool = _pool_rows(ubuf, rows, w_pool_ref, pscale_ref[...], cnt_of)

    x1 = _mix_residual(x, attn, pool, g_attn_ref[...], g_pool_ref[...], w_o_ref,
                       g_post_mix_ref[...])
    h2 = _rms(x1, g_pre_ffn_ref[...]).astype(BF16)
    k0_ref[...] = k
    v0_ref[...] = v
    u0_ref[...] = u[rows - U_HIST:, :]
    tail = 2 * SUBLANES
    tc = up0_ref.shape[2] // 2
    for c in range(up0_ref.shape[0]):
        up_tail = _dot(h2[rows - tail:, :], _ff_cols(wup_ref, c, tc))
        up0_ref[c] = up_tail[tail - CONV_STATE:, :]


def _prompt_kernel(n_tiles, x_ref, k0_ref, v0_ref, u0_ref, up0_ref, w_in_ref, b_in_ref,
                   sink_ref, w_pool_ref, pscale_ref, g_attn_ref, g_pool_ref, w_o_ref,
                   g_pre_mix_ref, g_post_mix_ref, g_pre_ffn_ref, g_post_ffn_ref, wup_ref, cw_ref,
                   cb_ref, wdown_ref,
                   y_ref, kout_ref, vout_ref, uout_ref, cout_ref,
                   kbuf, vbuf, ubuf, upst, attn_buf, xbuf, obuf, qsbuf):
    g = pl.program_id(0)
    n_total = pl.num_programs(0) - 1
    rows = x_ref.shape[1]
    n_chunks, _, tc2 = upst.shape
    tc = tc2 // 2
    n_col = x_ref.shape[2] // LANES
    S = SUBLANES
    slabs, na = _row_slabs(rows)
    st2_row, st1_row = S - 1, 2 * S - 1

    tf = lax.rem(jnp.minimum(g, n_total - 1), n_tiles)
    tj = lax.rem(jnp.maximum(g - 1, 0), n_tiles)
    wslot = lax.rem(g, 2)
    rslot = 1 - wslot

    @pl.when(tf == 0)
    def _():
        kbuf[0:WINDOW, :] = k0_ref[...]
        vbuf[0:WINDOW, :] = v0_ref[...]
        ubuf[0:U_HIST, :] = u0_ref[...]

    @pl.when(tj == 0)
    def _():
        upst[...] = jnp.zeros(upst.shape, F32)
        upst[:, st2_row:st2_row + 1, :] = up0_ref[:, 0:1, :]
        upst[:, st1_row:st1_row + 1, :] = up0_ref[:, 1:2, :]

    @pl.when(g == 0)
    def _():
        xbuf[1] = jnp.zeros(xbuf.shape[1:], F32)

    def mixer_in():
        z = _dot(_rms(x_ref[0], g_pre_mix_ref[...]).astype(BF16), w_in_ref[...]) + b_in_ref[...]
        kbuf[WINDOW:, :] = z[:, Q_W:Q_W + KV_W]
        vbuf[WINDOW:, :] = z[:, Q_W + KV_W:Q_W + 2 * KV_W]
        ubuf[U_HIST:, :] = z[:, Q_W + 2 * KV_W:]
        qsbuf[...] = z[:, :Q_W] * SM_SCALE

    def mixer_attend(i):
        r0 = i * WINDOW
        k2 = kbuf[r0:r0 + 2 * WINDOW, :].astype(BF16)
        v2 = vbuf[r0:r0 + 2 * WINDOW, :].astype(BF16)
        attn_buf[r0:r0 + WINDOW, :] = _attend_block(
            qsbuf[r0:r0 + WINDOW, :], k2, v2, N_META + tf * rows + r0, lambda h: sink_ref[h])

    def mixer_out():
        pool = _pool_rows_full(ubuf, rows, w_pool_ref, pscale_ref[...])
        x1 = _mix_residual(x_ref[0], attn_buf[...], pool, g_attn_ref[...], g_pool_ref[...],
                           w_o_ref, g_post_mix_ref[...])
        for j in range(n_col):
            xbuf[wslot, j] = x1[:, j * LANES:(j + 1) * LANES]
        kbuf[0:WINDOW, :] = kbuf[rows:rows + WINDOW, :]
        vbuf[0:WINDOW, :] = vbuf[rows:rows + WINDOW, :]
        ubuf[0:U_HIST, :] = ubuf[rows:rows + U_HIST, :]

    def x1_slab_order():
        return jnp.concatenate(
            [jnp.concatenate([xbuf[rslot, j, pl.ds(start, S, stride=stride), :]
                              for j in range(n_col)], axis=1) for start, stride in slabs], axis=0)

    ffn_vals = {}

    def ffn_in():
        ffn_vals["h2"] = _rms(x1_slab_order(), g_pre_ffn_ref[...]).astype(BF16)

    def up_project(c):
        ffn_vals["up", c] = _dot(ffn_vals["h2"], _ff_cols(wup_ref, c, tc))

    def activate(c):
        ub = ffn_vals.pop(("up", c))
        s1, s2 = _shifted_rows(ub, upst[c], na)
        upst[c] = ub[rows - 2 * S:rows, :]
        cw = _ff_cols(cw_ref, c, tc)
        conv = _ff_cols(cb_ref, c, tc) + s2 * cw[0:1, :]
        conv = conv + s1 * cw[1:2, :]
        conv = conv + ub * cw[2:3, :]
        ffn_vals["a", c] = _gated(conv, tc)

    def down_project(p):
        chunks = range(2 * p, min(2 * p + 2, n_chunks))
        a = jnp.concatenate([ffn_vals.pop(("a", c)) for c in chunks], axis=1)
        part = _dot(a, wdown_ref[chunks[0] * tc:(chunks[-1] + 1) * tc, :])
        ffn_vals["acc"] = part if p == 0 else ffn_vals["acc"] + part

    n_pairs = (n_chunks + 1) // 2
    down_due = {min(2 * p + 1, n_chunks - 1) + 1: p for p in range(n_pairs)}

    def ffn_stage(c):
        if c + 1 < n_chunks:
            up_project(c + 1)
        if c < n_chunks:
            activate(c)
        if c in down_due:
            down_project(down_due[c])

    def ffn_out():
        y = x1_slab_order() + _rms(ffn_vals["acc"], g_post_ffn_ref[...])
        for i, (start, stride) in enumerate(slabs):
            for j in range(n_col):
                obuf[j, pl.ds(start, S, stride=stride), :] = y[i * S:(i + 1) * S,
                                                              j * LANES:(j + 1) * LANES]
        y_ref[0] = jnp.concatenate([obuf[j] for j in range(n_col)], axis=1)

    ffn_stages = [lambda c=c: ffn_stage(c) for c in range(n_chunks + 1)]
    mixer_stages = [mixer_in] + [lambda i=i: mixer_attend(i) for i in range(rows // WINDOW)]
    mixer_stages.append(mixer_out)

    ffn_in()
    up_project(0)
    emitted = 0
    for c, stage in enumerate(ffn_stages):
        due = ((c + 1) * len(mixer_stages)) // len(ffn_stages)
        while emitted < due:
            mixer_stages[emitted]()
            emitted += 1
        stage()
    ffn_out()

    @pl.when((tf == n_tiles - 1) & (g < n_total))
    def _():
        kout_ref[0] = kbuf[0:WINDOW, :].T
        vout_ref[0] = vbuf[0:WINDOW, :].T
        uout_ref[0] = ubuf[0:U_HIST, :]

    @pl.when((tj == n_tiles - 1) & (g > 0))
    def _():
        cout_ref[0, :, 0:1, :] = upst[:, st2_row:st2_row + 1, :]
        cout_ref[0, :, 1:2, :] = upst[:, st1_row:st1_row + 1, :]


def _sample_a_kernel(x_ref, ck_ref, cv_ref, w_in_ref, b_in_ref, g_pre_mix_ref, sinkcol_ref,
                     slopecol_ref,
                     attn_ref, u_ref, ko_ref, vo_ref,
                     qs_buf, kn_buf, vn_buf, knt_buf, vnt_buf):
    step = pl.program_id(0)
    n_tok = ck_ref.shape[0]

    @pl.when(step == 0)
    def _():
        z = _dot(_rms(x_ref[...], g_pre_mix_ref[...]).astype(BF16), w_in_ref[...]) + b_in_ref[...]
        k = z[:, Q_W:Q_W + KV_W]
        v = z[:, Q_W + KV_W:Q_W + 2 * KV_W]
        qs_buf[...] = z[:, :Q_W] * SM_SCALE
        kn_buf[...] = k
        vn_buf[...] = v
        knt_buf[...] = k.T
        vnt_buf[...] = v.T
        u_ref[...] = z[:, Q_W + 2 * KV_W:]

    lo1 = _lane_lo((1, LANES))
    kj = lax.broadcasted_iota(jnp.int32, (N_HEADS, WINDOW), 1)
    neg_dist = (kj - WINDOW).astype(F32)
    bias = slopecol_ref[...] * neg_dist
    sink = sinkcol_ref[...]
    last_lane = lax.broadcasted_iota(jnp.int32, (KV_W, WINDOW), 1) == WINDOW - 1

    def token(j, carry):
        b = step * n_tok + j
        qrow = qs_buf[pl.ds(b, 1), :]
        kn = kn_buf[pl.ds(b, 1), :]
        vn = vn_buf[pl.ds(b, 1), :]
        kt = ck_ref[j]
        vt = cv_ref[j]
        heads = []
        for h in range(N_HEADS):
            p, e, g = h // 2, h % 2, h // (N_HEADS // N_KV_HEADS)
            blk = qrow[:, p * LANES:(p + 1) * LANES]
            src = blk if e == g else pltpu.roll(blk, HEAD_DIM, axis=1)
            keep = lo1 if g == 0 else jnp.logical_not(lo1)
            heads.append(jnp.where(keep, src, 0.0))
        qf = jnp.concatenate(heads, axis=0).astype(BF16)
        s = _dot(qf, kt.astype(BF16)) + bias
        s_self = jnp.sum(qf.astype(F32) * kn.astype(BF16).astype(F32), axis=-1, keepdims=True)
        m = jnp.maximum(jnp.maximum(jnp.max(s, axis=-1, keepdims=True), s_self), sink)
        pe = jnp.exp(s - m)
        pe_self = jnp.exp(s_self - m)
        den = jnp.sum(pe, axis=-1, keepdims=True) + pe_self + jnp.exp(sink - m)
        o = _dot_nt(pe.astype(BF16), vt.astype(BF16))
        o = o + pe_self.astype(BF16).astype(F32) * vn.astype(BF16).astype(F32)
        o = o / den
        pieces = []
        for p in range(N_HEADS // 2):
            g = (2 * p) // (N_HEADS // N_KV_HEADS)
            a, c = o[2 * p:2 * p + 1, :], o[2 * p + 1:2 * p + 2, :]
            if g == 0:
                pieces.append(jnp.where(lo1, a, pltpu.roll(c, HEAD_DIM, axis=1)))
            else:
                pieces.append(jnp.where(lo1, pltpu.roll(a, HEAD_DIM, axis=1), c))
        attn_ref[pl.ds(b, 1), :] = jnp.concatenate(pieces, axis=1)
        bring = WINDOW - 1 - b
        ko_ref[j] = jnp.where(last_lane, pltpu.roll(knt_buf[...], bring, axis=1),
                              pltpu.roll(kt, WINDOW - 1, axis=1))
        vo_ref[j] = jnp.where(last_lane, pltpu.roll(vnt_buf[...], bring, axis=1),
                              pltpu.roll(vt, WINDOW - 1, axis=1))
        return carry

    lax.fori_loop(0, n_tok, token, 0, unroll=SAMPLE_UNROLL)


def _sample_b_kernel(x_ref, attn_ref, u_ref, sp_ref, sc_ref, w_pool_ref, pscale_ref,
                     g_attn_ref, g_pool_ref, w_o_ref, g_post_mix_ref, g_pre_ffn_ref,
                     g_post_ffn_ref, wup_ref, cw_ref, cb_ref, wdown_ref,
                     y_ref, po_ref, co_ref):
    tc = FF_CHUNK
    d_ff = wdown_ref.shape[0]
    n_chunks = d_ff // tc
    u = u_ref[...]

    outs = []
    for g, w in enumerate(POOL_WINDOWS):
        cs = slice(g * POOL_GW, (g + 1) * POOL_GW)
        cur = u[:, cs]
        acc_u = cur
        for j in range(1, w):
            acc_u = acc_u + sp_ref[POOL_STATE - j, :, cs]
        outs.append(_dot((acc_u / float(w) - cur).astype(BF16), w_pool_ref[g]))
    pool = jnp.concatenate(outs, axis=1) * pscale_ref[...]
    for r in range(POOL_STATE - 1):
        po_ref[r] = sp_ref[r + 1]
    po_ref[POOL_STATE - 1] = u

    x1 = _mix_residual(x_ref[...], attn_ref[...], pool, g_attn_ref[...], g_pool_ref[...],
                       w_o_ref, g_post_mix_ref[...])
    h2 = _rms(x1, g_pre_ffn_ref[...]).astype(BF16)

    co_ref[:, 0, :] = sc_ref[:, 1, :]
    ffn = jnp.zeros(x1.shape, F32)
    for c in range(n_chunks):
        up = _dot(h2, _ff_cols(wup_ref, c, tc))
        gcols = slice(c * tc, (c + 1) * tc)
        vcols = slice(d_ff + c * tc, d_ff + (c + 1) * tc)
        co_ref[:, 1, gcols] = up[:, :tc]
        co_ref[:, 1, vcols] = up[:, tc:]
        old0 = jnp.concatenate([sc_ref[:, 0, gcols], sc_ref[:, 0, vcols]], axis=1)
        old1 = jnp.concatenate([sc_ref[:, 1, gcols], sc_ref[:, 1, vcols]], axis=1)
        cw = _ff_cols(cw_ref, c, tc)
        conv = _ff_cols(cb_ref, c, tc) + old0 * cw[0:1, :]
        conv = conv + old1 * cw[1:2, :]
        conv = conv + up * cw[2:3, :]
        ffn = ffn + _dot(_gated(conv, tc), wdown_ref[c * tc:(c + 1) * tc, :])
    y_ref[...] = x1 + _rms(ffn, g_post_ffn_ref[...])


def _vmem():
    return pl.BlockSpec(memory_space=pltpu.VMEM)


def _smem():
    return pl.BlockSpec(memory_space=pltpu.SMEM)


def _resident(shape):
    nd = len(shape)
    return pl.BlockSpec(shape, lambda *_: (0,) * nd, pipeline_mode=pl.Buffered(1))


def _unchunk_cols(a):
    n_chunks, r, tc2 = a.shape
    a = a.reshape(n_chunks, r, 2, tc2 // 2)
    return jnp.transpose(a, (1, 2, 0, 3)).reshape(r, n_chunks * tc2)


def kernel(x_prompt, x_sample, cache_k, cache_v, state_pool, state_conv, meta, w_in, b_in, sinks,
           w_pool, pool_scale, g_attn_out, g_pool_out, w_o, g_pre_mix, g_post_mix, g_pre_ffn,
           g_post_ffn, w_up, conv_w, conv_b, w_down):
    assert w_in.shape[0] == 1, "single layer"
    batch, seq, d_model = x_prompt.shape
    dec_batch = x_sample.shape[0]
    d_ff = w_down.shape[1]
    tc = FF_CHUNK
    n_chunks = d_ff // tc
    assert n_chunks * tc == d_ff and seq % SEQ_TILE == 0 and dec_batch % SAMPLE_TB == 0
    assert meta.shape[0] == N_META and N_META > POOL_STATE

    row = lambda a: a[0].reshape(1, -1)
    w_in_b = w_in[0].astype(BF16)
    b_in_r = row(b_in)
    w_pool_b = w_pool[0].astype(BF16)
    w_o_b = w_o[0].astype(BF16)
    wup_c = w_up[0].astype(BF16)
    half_value = jnp.concatenate([jnp.ones((d_ff,), F32), jnp.full((d_ff,), 0.5, F32)])
    cw_c = conv_w[0] * half_value
    cb_c = (conv_b[0] * half_value).reshape(1, -1)
    wdown_c = w_down[0].astype(BF16)
    sink_s = sinks[0]
    sink_col = sinks[0].reshape(N_HEADS, 1)
    slope_col = jnp.asarray(np.array(SLOPES, np.float32).reshape(N_HEADS, 1))
    gains = dict(pscale=row(pool_scale), g_attn=row(g_attn_out), g_pool=row(g_pool_out),
                 g_pre_mix=row(g_pre_mix), g_post_mix=row(g_post_mix),
                 g_pre_ffn=row(g_pre_ffn), g_post_ffn=row(g_post_ffn))

    x_meta = jnp.concatenate([jnp.zeros((WINDOW - N_META, d_model), F32), meta.astype(F32)], 0)
    k0, v0, u0, up0 = pl.pallas_call(
        _meta_kernel,
        out_shape=(jax.ShapeDtypeStruct((WINDOW, KV_W), F32),
                   jax.ShapeDtypeStruct((WINDOW, KV_W), F32),
                   jax.ShapeDtypeStruct((U_HIST, POOL_W), F32),
                   jax.ShapeDtypeStruct((n_chunks, CONV_STATE, 2 * tc), F32)),
        in_specs=[_vmem(), _vmem(), _vmem(), _smem()] + [_vmem()] * 9,
        out_specs=(_vmem(),) * 4,
        scratch_shapes=[pltpu.VMEM((U_HIST + WINDOW, POOL_W), F32)],
        compiler_params=pltpu.CompilerParams(vmem_limit_bytes=VMEM_LIMIT),
        name="meta",
    )(x_meta, w_in_b, b_in_r, sink_s, w_pool_b, gains["pscale"], gains["g_attn"],
      gains["g_pool"], w_o_b, gains["g_pre_mix"], gains["g_post_mix"], gains["g_pre_ffn"], wup_c)

    n_tiles = seq // SEQ_TILE
    n_total = batch * n_tiles
    mixer_tile = lambda g: jnp.minimum(g, n_total - 1)
    ffn_tile = lambda g: jnp.maximum(g - 1, 0)
    per_batch = lambda shape, tile: pl.BlockSpec(
        (1,) + shape, lambda g: (tile(g) // n_tiles,) + (0,) * len(shape))
    prompt_inputs = (
        x_prompt, k0, v0, u0, up0, w_in_b, b_in_r, sink_s, w_pool_b, gains["pscale"],
        gains["g_attn"], gains["g_pool"], w_o_b, gains["g_pre_mix"], gains["g_post_mix"],
        gains["g_pre_ffn"], gains["g_post_ffn"], wup_c, cw_c, cb_c, wdown_c)
    in_specs = [pl.BlockSpec((1, SEQ_TILE, d_model),
                             lambda g: (mixer_tile(g) // n_tiles, mixer_tile(g) % n_tiles, 0))]
    in_specs += [_smem() if a is sink_s else _resident(a.shape) for a in prompt_inputs[1:]]
    y_prompt, k_p, v_p, u_p, c_p = pl.pallas_call(
        functools.partial(_prompt_kernel, n_tiles),
        grid=(n_total + 1,),
        out_shape=(jax.ShapeDtypeStruct((batch, seq, d_model), F32),
                   jax.ShapeDtypeStruct((batch, WINDOW, KV_W), F32),
                   jax.ShapeDtypeStruct((batch, WINDOW, KV_W), F32),
                   jax.ShapeDtypeStruct((batch, U_HIST, POOL_W), F32),
                   jax.ShapeDtypeStruct((batch, n_chunks, CONV_STATE, 2 * tc), F32)),
        in_specs=in_specs,
        out_specs=(pl.BlockSpec((1, SEQ_TILE, d_model),
                                lambda g: (ffn_tile(g) // n_tiles, ffn_tile(g) % n_tiles, 0)),
                   per_batch((KV_W, WINDOW), mixer_tile), per_batch((KV_W, WINDOW), mixer_tile),
                   per_batch((U_HIST, POOL_W), mixer_tile),
                   per_batch((n_chunks, CONV_STATE, 2 * tc), ffn_tile)),
        scratch_shapes=[
            pltpu.VMEM((WINDOW + SEQ_TILE, KV_W), F32),
            pltpu.VMEM((WINDOW + SEQ_TILE, KV_W), F32),
            pltpu.VMEM((U_HIST + SEQ_TILE, POOL_W), F32),
            pltpu.VMEM((n_chunks, 2 * SUBLANES, 2 * tc), F32),
            pltpu.VMEM((SEQ_TILE, Q_W), F32),
            pltpu.VMEM((2, d_model // LANES, SEQ_TILE, LANES), F32),
            pltpu.VMEM((d_model // LANES, SEQ_TILE, LANES), F32),
            pltpu.VMEM((SEQ_TILE, Q_W), F32),
        ],
        compiler_params=pltpu.CompilerParams(
            dimension_semantics=("arbitrary",), vmem_limit_bytes=VMEM_LIMIT),
        name="prompt",
    )(*prompt_inputs)

    feat_pos = lambda c: jnp.transpose(c[0].reshape(dec_batch, WINDOW, KV_W), (0, 2, 1))
    ck, cv = feat_pos(cache_k), feat_pos(cache_v)
    xs = x_sample.reshape(dec_batch, d_model)
    tb = SAMPLE_TB
    whole = lambda shape: pl.BlockSpec(shape, lambda i: (0,) * len(shape))
    cache_spec = pl.BlockSpec((tb, KV_W, WINDOW), lambda i: (i, 0, 0))
    attn_s, u_s, k_s, v_s = pl.pallas_call(
        _sample_a_kernel,
        grid=(dec_batch // tb,),
        out_shape=(jax.ShapeDtypeStruct((dec_batch, Q_W), F32),
                   jax.ShapeDtypeStruct((dec_batch, POOL_W), F32),
                   jax.ShapeDtypeStruct((dec_batch, KV_W, WINDOW), F32),
                   jax.ShapeDtypeStruct((dec_batch, KV_W, WINDOW), F32)),
        in_specs=[whole(xs.shape), cache_spec, cache_spec, _resident(w_in_b.shape),
                  _resident(b_in_r.shape), _resident(gains["g_pre_mix"].shape),
                  _resident(sink_col.shape), _resident(slope_col.shape)],
        out_specs=(whole((dec_batch, Q_W)), whole((dec_batch, POOL_W)), cache_spec, cache_spec),
        scratch_shapes=[pltpu.VMEM((dec_batch, Q_W), F32), pltpu.VMEM((dec_batch, KV_W), F32),
                        pltpu.VMEM((dec_batch, KV_W), F32), pltpu.VMEM((KV_W, dec_batch), F32),
                        pltpu.VMEM((KV_W, dec_batch), F32)],
        compiler_params=pltpu.CompilerParams(
            dimension_semantics=("arbitrary",), vmem_limit_bytes=VMEM_LIMIT),
        name="sample_a",
    )(xs, ck, cv, w_in_b, b_in_r, gains["g_pre_mix"], sink_col, slope_col)

    sp = jnp.transpose(state_pool[0], (1, 0, 2))
    sc = state_conv[0]
    y_s, pool_s, conv_s = pl.pallas_call(
        _sample_b_kernel,
        out_shape=(jax.ShapeDtypeStruct((dec_batch, d_model), F32),
                   jax.ShapeDtypeStruct(sp.shape, F32),
                   jax.ShapeDtypeStruct(sc.shape, F32)),
        in_specs=[_vmem()] * 17,
        out_specs=(_vmem(),) * 3,
        compiler_params=pltpu.CompilerParams(vmem_limit_bytes=VMEM_LIMIT),
        name="sample_b",
    )(xs, attn_s, u_s, sp, sc, w_pool_b, gains["pscale"], gains["g_attn"], gains["g_pool"],
      w_o_b, gains["g_post_mix"], gains["g_pre_ffn"], gains["g_post_ffn"], wup_c, cw_c, cb_c,
      wdown_c)

    def pos_feat(c):
        n = c.shape[0]
        return jnp.transpose(c, (0, 2, 1)).reshape(1, n, WINDOW, N_KV_HEADS, HEAD_DIM)

    conv_p = jax.vmap(_unchunk_cols)(c_p)
    return (y_prompt,
            y_s.reshape(dec_batch, 1, d_model),
            pos_feat(k_p), pos_feat(v_p),
            u_p[:, U_HIST - POOL_STATE:, :][None],
            conv_p[None],
            pos_feat(k_s), pos_feat(v_s),
            jnp.transpose(pool_s, (1, 0, 2))[None],
            conv_s[None])
```

```python
import functools

import numpy as np
import jax
import jax.numpy as jnp
from jax import lax
from jax.experimental import pallas as pl
from jax.experimental.pallas import tpu as pltpu

F32 = jnp.float32
BF16 = jnp.bfloat16

N_META = 16
HEAD_DIM = 64
N_HEADS = 8
N_KV_HEADS = 2
WINDOW = 128
POOL_WINDOWS = (2, 4, 8, 16)
POOL_STATE = 15
CONV_STATE = 2
RMS_EPS = 1e-6
SM_SCALE = HEAD_DIM ** -0.5
SLOPES = tuple(2.0 ** (-(h + 1) * (8.0 / N_HEADS)) for h in range(N_HEADS))

LANES = 128
SUBLANES = 8
KV_W = N_KV_HEADS * HEAD_DIM
Q_W = N_HEADS * HEAD_DIM
POOL_W = 512
POOL_GW = POOL_W // len(POOL_WINDOWS)
U_HIST = 16

SEQ_TILE = 512
FF_CHUNK = 256
SAMPLE_TB = 32
SAMPLE_UNROLL = 8
VMEM_LIMIT = 56 * 1024 * 1024
TAIL_SLABS = 4
GELU_C1 = float(np.sqrt(2.0 / np.pi))
GELU_C2 = GELU_C1 * 0.044715


def _rms(x, g):
    ms = jnp.mean(x * x, axis=-1, keepdims=True)
    return x * lax.rsqrt(ms + RMS_EPS) * g


def _dot(a, b):
    return jnp.dot(a, b, preferred_element_type=F32)


def _dot_nt(a, b):
    return lax.dot_general(a, b, (((1,), (1,)), ((), ())), preferred_element_type=F32)


def _lane_lo(shape):
    return lax.broadcasted_iota(jnp.int32, shape, len(shape) - 1) < HEAD_DIM


def _attend_block(qs, k2, v2, pos_start, sink_of):
    nq, nk = qs.shape[0], k2.shape[0]
    qi = lax.broadcasted_iota(jnp.int32, (nq, nk), 0)
    kj = lax.broadcasted_iota(jnp.int32, (nq, nk), 1)
    valid = (kj >= qi) & (kj <= qi + WINDOW) & (kj >= WINDOW - pos_start)
    neg_dist = jnp.where(valid, (kj - qi - WINDOW).astype(F32), -jnp.inf)
    lo = _lane_lo((nq, LANES))
    group = N_HEADS // N_KV_HEADS
    q_heads = []
    for p in range(N_HEADS // 2):
        g = (2 * p) // group
        blk = qs[:, p * LANES:(p + 1) * LANES]
        rolled = pltpu.roll(blk, HEAD_DIM, axis=1)
        keep = lo if g == 0 else jnp.logical_not(lo)
        for e in range(2):
            q_heads.append(jnp.where(keep, blk if e == g else rolled, 0.0).astype(BF16))
    s_all = _dot_nt(jnp.concatenate(q_heads, axis=0), k2)
    probs, dens = [], []
    for h in range(N_HEADS):
        s = s_all[h * nq:(h + 1) * nq, :] + SLOPES[h] * neg_dist
        sink = sink_of(h)
        m = jnp.maximum(jnp.max(s, axis=-1, keepdims=True), sink)
        pe = jnp.exp(s - m)
        dens.append(jnp.sum(pe, axis=-1, keepdims=True) + jnp.exp(sink - m))
        probs.append(pe.astype(BF16))
    o_all = _dot(jnp.concatenate(probs, axis=0), v2)
    outs = [o_all[h * nq:(h + 1) * nq, :] / dens[h] for h in range(N_HEADS)]
    pieces = []
    for p in range(N_HEADS // 2):
        a, b = outs[2 * p], outs[2 * p + 1]
        if (2 * p) // group == 0:
            pieces.append(jnp.where(lo, a, pltpu.roll(b, HEAD_DIM, axis=1)))
        else:
            pieces.append(jnp.where(lo, pltpu.roll(a, HEAD_DIM, axis=1), b))
    return jnp.concatenate(pieces, axis=1)


def _pool_rows(ubuf, rows, w_pool_ref, pool_scale, cnt_of):
    outs = []
    for g, w in enumerate(POOL_WINDOWS):
        cs = slice(g * POOL_GW, (g + 1) * POOL_GW)
        cur = ubuf[U_HIST:U_HIST + rows, cs]
        acc = cur
        for j in range(1, w):
            acc = acc + ubuf[U_HIST - j:U_HIST - j + rows, cs]
        mean = acc / cnt_of(w)
        outs.append(_dot((mean - cur).astype(BF16), w_pool_ref[g]))
    return jnp.concatenate(outs, axis=1) * pool_scale


def _pool_rows_full(ubuf, rows, w_pool_ref, pool_scale):
    outs = []
    for g, w in enumerate(POOL_WINDOWS):
        h = ubuf[:, g * POOL_GW:(g + 1) * POOL_GW]
        s, k = h, 1
        while k < w:
            if k < SUBLANES:
                s = s + pltpu.roll(s, k, axis=0)
            else:
                s = s + jnp.concatenate([s[:k, :], s[:-k, :]], axis=0)
            k *= 2
        cur = h[U_HIST:, :]
        outs.append(_dot((s[U_HIST:, :] * (1.0 / w) - cur).astype(BF16), w_pool_ref[g]))
    return jnp.concatenate(outs, axis=1) * pool_scale


def _mix_residual(x, attn, pool, g_attn, g_pool, w_o_ref, g_post_mix):
    mixin = jnp.concatenate([_rms(attn, g_attn), _rms(pool, g_pool)], axis=1).astype(BF16)
    return x + _rms(_dot(mixin, w_o_ref[...]), g_post_mix)


def _gated(conv, tc):
    g, hv = conv[:, :tc], conv[:, tc:]
    inner = g * (GELU_C1 + GELU_C2 * (g * g))
    return (g * (1.0 + jnp.tanh(inner)) * hv).astype(BF16)


def _ff_cols(ref, c, tc):
    d_ff = ref.shape[-1] // 2
    return jnp.concatenate([ref[:, c * tc:(c + 1) * tc],
                            ref[:, d_ff + c * tc:d_ff + (c + 1) * tc]], axis=1)


def _row_slabs(rows):
    na = rows // SUBLANES - TAIL_SLABS
    assert na % 4 == 0 and (na // 4) % 2 == 1
    slabs = [(v, na) for v in range(na)]
    slabs += [(SUBLANES * na + v, TAIL_SLABS) for v in range(TAIL_SLABS)]
    return slabs, na


def _shifted_rows(ub, prev, na):
    S = SUBLANES
    width = ub.shape[1]
    sub = lax.broadcasted_iota(jnp.int32, (S, width), 0)
    slab = lambda i: ub[i * S:(i + 1) * S, :]

    def wrap(x, y):
        return pltpu.roll(jnp.where(sub == S - 1, y, x), 1, axis=0)

    a1 = wrap(slab(na - 1), prev[S:2 * S, :])
    a2 = wrap(slab(na - 2), prev[0:S, :])
    b1 = wrap(slab(na + 3), slab(na - 1))
    b2 = wrap(slab(na + 2), slab(na - 2))
    s1 = jnp.concatenate([a1, ub[0:(na - 1) * S, :], b1, ub[na * S:(na + 3) * S, :]], axis=0)
    s2 = jnp.concatenate([a2, a1, ub[0:(na - 2) * S, :], b2, b1, ub[na * S:(na + 2) * S, :]],
                         axis=0)
    return s1, s2


def _meta_kernel(x_ref, w_in_ref, b_in_ref, sink_ref, w_pool_ref, pscale_ref, g_attn_ref,
                 g_pool_ref, w_o_ref, g_pre_mix_ref, g_post_mix_ref, g_pre_ffn_ref, wup_ref,
                 k0_ref, v0_ref, u0_ref, up0_ref, ubuf):
    rows = x_ref.shape[0]
    pos0 = N_META - rows
    x = x_ref[...]
    z = _dot(_rms(x, g_pre_mix_ref[...]).astype(BF16), w_in_ref[...]) + b_in_ref[...]
    k = z[:, Q_W:Q_W + KV_W]
    v = z[:, Q_W + KV_W:Q_W + 2 * KV_W]
    u = z[:, Q_W + 2 * KV_W:]
    pos = pos0 + lax.broadcasted_iota(jnp.int32, (rows, 1), 0)
    zeros_kv = jnp.zeros((WINDOW, KV_W), BF16)
    k2 = jnp.concatenate([zeros_kv, k.astype(BF16)], axis=0)
    v2 = jnp.concatenate([zeros_kv, v.astype(BF16)], axis=0)
    attn = _attend_block(z[:, :Q_W] * SM_SCALE, k2, v2, pos0, lambda h: sink_ref[h])

    ubuf[0:U_HIST, :] = jnp.zeros((U_HIST, POOL_W), F32)
    ubuf[U_HIST:, :] = jnp.where(pos >= 0, u, 0.0)
    cnt_of = lambda w: jnp.clip(pos + 1, 1, w).astype(F32)
    pool = _pool_rows(ubuf, rows, w_pool_ref, pscale_ref[...], cnt_of)

    x1 = _mix_residual(x, attn, pool, g_attn_ref[...], g_pool_ref[...], w_o_ref,
                       g_post_mix_ref[...])
    h2 = _rms(x1, g_pre_ffn_ref[...]).astype(BF16)
    k0_ref[...] = k
    v0_ref[...] = v
    u0_ref[...] = u[rows - U_HIST:, :]
    tail = 2 * SUBLANES
    tc = up0_ref.shape[2] // 2
    for c in range(up0_ref.shape[0]):
        up_tail = _dot(h2[rows - tail:, :], _ff_cols(wup_ref, c, tc))
        up0_ref[c] = up_tail[tail - CONV_STATE:, :]


def _prompt_kernel(n_tiles, x_ref, k0_ref, v0_ref, u0_ref, up0_ref, w_in_ref, b_in_ref,
                   sink_ref, w_pool_ref, pscale_ref, g_attn_ref, g_pool_ref, w_o_ref,
                   g_pre_mix_ref, g_post_mix_ref, g_pre_ffn_ref, g_post_ffn_ref, wup_ref, cw_ref,
                   cb_ref, wdown_ref,
                   y_ref, kout_ref, vout_ref, uout_ref, cout_ref,
                   kbuf, vbuf, ubuf, upst, attn_buf, xbuf, obuf, qsbuf):
    g = pl.program_id(0)
    n_total = pl.num_programs(0) - 1
    rows = x_ref.shape[1]
    n_chunks, _, tc2 = upst.shape
    tc = tc2 // 2
    n_col = x_ref.shape[2] // LANES
    S = SUBLANES
    slabs, na = _row_slabs(rows)
    st2_row, st1_row = S - 1, 2 * S - 1

    tf = lax.rem(jnp.minimum(g, n_total - 1), n_tiles)
    tj = lax.rem(jnp.maximum(g - 1, 0), n_tiles)
    wslot = lax.rem(g, 2)
    rslot = 1 - wslot

    @pl.when(tf == 0)
    def _():
        kbuf[0:WINDOW, :] = k0_ref[...]
        vbuf[0:WINDOW, :] = v0_ref[...]
        ubuf[0:U_HIST, :] = u0_ref[...]

    @pl.when(tj == 0)
    def _():
        upst[...] = jnp.zeros(upst.shape, F32)
        upst[:, st2_row:st2_row + 1, :] = up0_ref[:, 0:1, :]
        upst[:, st1_row:st1_row + 1, :] = up0_ref[:, 1:2, :]

    def mixer_in():
        z = _dot(_rms(x_ref[0], g_pre_mix_ref[...]).astype(BF16), w_in_ref[...]) + b_in_ref[...]
        kbuf[WINDOW:, :] = z[:, Q_W:Q_W + KV_W]
        vbuf[WINDOW:, :] = z[:, Q_W + KV_W:Q_W + 2 * KV_W]
        ubuf[U_HIST:, :] = z[:, Q_W + 2 * KV_W:]
        qsbuf[...] = z[:, :Q_W] * SM_SCALE

    def mixer_attend(i):
        r0 = i * WINDOW
        k2 = kbuf[r0:r0 + 2 * WINDOW, :].astype(BF16)
        v2 = vbuf[r0:r0 + 2 * WINDOW, :].astype(BF16)
        attn_buf[r0:r0 + WINDOW, :] = _attend_block(
            qsbuf[r0:r0 + WINDOW, :], k2, v2, N_META + tf * rows + r0, lambda h: sink_ref[h])

    def mixer_out():
        pool = _pool_rows_full(ubuf, rows, w_pool_ref, pscale_ref[...])
        x1 = _mix_residual(x_ref[0], attn_buf[...], pool, g_attn_ref[...], g_pool_ref[...],
                           w_o_ref, g_post_mix_ref[...])
        for j in range(n_col):
            xbuf[wslot, j] = x1[:, j * LANES:(j + 1) * LANES]
        kbuf[0:WINDOW, :] = kbuf[rows:rows + WINDOW, :]
        vbuf[0:WINDOW, :] = vbuf[rows:rows + WINDOW, :]
        ubuf[0:U_HIST, :] = ubuf[rows:rows + U_HIST, :]

    def x1_slab_order():
        return jnp.concatenate(
            [jnp.concatenate([xbuf[rslot, j, pl.ds(start, S, stride=stride), :]
                              for j in range(n_col)], axis=1) for start, stride in slabs], axis=0)

    ffn_vals = {}

    def ffn_in():
        ffn_vals["h2"] = _rms(x1_slab_order(), g_pre_ffn_ref[...]).astype(BF16)

    def up_project(c):
        ffn_vals["up", c] = _dot(ffn_vals["h2"], _ff_cols(wup_ref, c, tc))

    def activate(c):
        ub = ffn_vals.pop(("up", c))
        s1, s2 = _shifted_rows(ub, upst[c], na)
        upst[c] = ub[rows - 2 * S:rows, :]
        cw = _ff_cols(cw_ref, c, tc)
        conv = _ff_cols(cb_ref, c, tc) + s2 * cw[0:1, :]
        conv = conv + s1 * cw[1:2, :]
        conv = conv + ub * cw[2:3, :]
        ffn_vals["a", c] = _gated(conv, tc)

    def down_project(c):
        part = _dot(ffn_vals.pop(("a", c)), wdown_ref[c * tc:(c + 1) * tc, :])
        ffn_vals["acc"] = part if c == 0 else ffn_vals["acc"] + part

    def ffn_stage(c):
        if c + 1 < n_chunks:
            up_project(c + 1)
        if c < n_chunks:
            activate(c)
        if c >= 1:
            down_project(c - 1)

    def ffn_out():
        y = x1_slab_order() + _rms(ffn_vals["acc"], g_post_ffn_ref[...])
        for i, (start, stride) in enumerate(slabs):
            for j in range(n_col):
                obuf[j, pl.ds(start, S, stride=stride), :] = y[i * S:(i + 1) * S,
                                                              j * LANES:(j + 1) * LANES]
        y_ref[0] = jnp.concatenate([obuf[j] for j in range(n_col)], axis=1)

    ffn_stages = [lambda c=c: ffn_stage(c) for c in range(n_chunks + 1)]
    mixer_stages = [mixer_in] + [lambda i=i: mixer_attend(i) for i in range(rows // WINDOW)]
    mixer_stages.append(mixer_out)

    @pl.when(g == 0)
    def _():
        for stage in mixer_stages:
            stage()

    @pl.when(g == n_total)
    def _():
        ffn_in()
        up_project(0)
        for stage in ffn_stages:
            stage()
        ffn_out()

    @pl.when((g > 0) & (g < n_total))
    def _():
        ffn_in()
        up_project(0)
        emitted = 0
        for c, stage in enumerate(ffn_stages):
            due = ((c + 1) * len(mixer_stages)) // len(ffn_stages)
            while emitted < due:
                mixer_stages[emitted]()
                emitted += 1
            stage()
        ffn_out()

    @pl.when((tf == n_tiles - 1) & (g < n_total))
    def _():
        kout_ref[0] = kbuf[0:WINDOW, :].T
        vout_ref[0] = vbuf[0:WINDOW, :].T
        uout_ref[0] = ubuf[0:U_HIST, :]

    @pl.when((tj == n_tiles - 1) & (g > 0))
    def _():
        cout_ref[0, :, 0:1, :] = upst[:, st2_row:st2_row + 1, :]
        cout_ref[0, :, 1:2, :] = upst[:, st1_row:st1_row + 1, :]


def _sample_a_kernel(x_ref, ck_ref, cv_ref, w_in_ref, b_in_ref, g_pre_mix_ref, sinkcol_ref,
                     slopecol_ref,
                     attn_ref, u_ref, ko_ref, vo_ref,
                     qs_buf, kn_buf, vn_buf, knt_buf, vnt_buf):
    step = pl.program_id(0)
    n_tok = ck_ref.shape[0]

    @pl.when(step == 0)
    def _():
        z = _dot(_rms(x_ref[...], g_pre_mix_ref[...]).astype(BF16), w_in_ref[...]) + b_in_ref[...]
        k = z[:, Q_W:Q_W + KV_W]
        v = z[:, Q_W + KV_W:Q_W + 2 * KV_W]
        qs_buf[...] = z[:, :Q_W] * SM_SCALE
        kn_buf[...] = k
        vn_buf[...] = v
        knt_buf[...] = k.T
        vnt_buf[...] = v.T
        u_ref[...] = z[:, Q_W + 2 * KV_W:]

    lo1 = _lane_lo((1, LANES))
    kj = lax.broadcasted_iota(jnp.int32, (N_HEADS, WINDOW), 1)
    neg_dist = (kj - WINDOW).astype(F32)
    bias = slopecol_ref[...] * neg_dist
    sink = sinkcol_ref[...]
    last_lane = lax.broadcasted_iota(jnp.int32, (KV_W, WINDOW), 1) == WINDOW - 1

    def token(j, carry):
        b = step * n_tok + j
        qrow = qs_buf[pl.ds(b, 1), :]
        kn = kn_buf[pl.ds(b, 1), :]
        vn = vn_buf[pl.ds(b, 1), :]
        kt = ck_ref[j]
        vt = cv_ref[j]
        heads = []
        for h in range(N_HEADS):
            p, e, g = h // 2, h % 2, h // (N_HEADS // N_KV_HEADS)
            blk = qrow[:, p * LANES:(p + 1) * LANES]
            src = blk if e == g else pltpu.roll(blk, HEAD_DIM, axis=1)
            keep = lo1 if g == 0 else jnp.logical_not(lo1)
            heads.append(jnp.where(keep, src, 0.0))
        qf = jnp.concatenate(heads, axis=0).astype(BF16)
        s = _dot(qf, kt.astype(BF16)) + bias
        s_self = jnp.sum(qf.astype(F32) * kn.astype(BF16).astype(F32), axis=-1, keepdims=True)
        m = jnp.maximum(jnp.maximum(jnp.max(s, axis=-1, keepdims=True), s_self), sink)
        pe = jnp.exp(s - m)
        pe_self = jnp.exp(s_self - m)
        den = jnp.sum(pe, axis=-1, keepdims=True) + pe_self + jnp.exp(sink - m)
        o = _dot_nt(pe.astype(BF16), vt.astype(BF16))
        o = o + pe_self.astype(BF16).astype(F32) * vn.astype(BF16).astype(F32)
        o = o / den
        pieces = []
        for p in range(N_HEADS // 2):
            g = (2 * p) // (N_HEADS // N_KV_HEADS)
            a, c = o[2 * p:2 * p + 1, :], o[2 * p + 1:2 * p + 2, :]
            if g == 0:
                pieces.append(jnp.where(lo1, a, pltpu.roll(c, HEAD_DIM, axis=1)))
            else:
                pieces.append(jnp.where(lo1, pltpu.roll(a, HEAD_DIM, axis=1), c))
        attn_ref[pl.ds(b, 1), :] = jnp.concatenate(pieces, axis=1)
        bring = WINDOW - 1 - b
        ko_ref[j] = jnp.where(last_lane, pltpu.roll(knt_buf[...], bring, axis=1),
                              pltpu.roll(kt, WINDOW - 1, axis=1))
        vo_ref[j] = jnp.where(last_lane, pltpu.roll(vnt_buf[...], bring, axis=1),
                              pltpu.roll(vt, WINDOW - 1, axis=1))
        return carry

    lax.fori_loop(0, n_tok, token, 0, unroll=SAMPLE_UNROLL)


def _sample_b_kernel(x_ref, attn_ref, u_ref, sp_ref, sc_ref, w_pool_ref, pscale_ref,
                     g_attn_ref, g_pool_ref, w_o_ref, g_post_mix_ref, g_pre_ffn_ref,
                     g_post_ffn_ref, wup_ref, cw_ref, cb_ref, wdown_ref,
                     y_ref, po_ref, co_ref):
    tc = FF_CHUNK
    d_ff = wdown_ref.shape[0]
    n_chunks = d_ff // tc
    u = u_ref[...]

    outs = []
    for g, w in enumerate(POOL_WINDOWS):
        cs = slice(g * POOL_GW, (g + 1) * POOL_GW)
        cur = u[:, cs]
        acc_u = cur
        for j in range(1, w):
            acc_u = acc_u + sp_ref[POOL_STATE - j, :, cs]
        outs.append(_dot((acc_u / float(w) - cur).astype(BF16), w_pool_ref[g]))
    pool = jnp.concatenate(outs, axis=1) * pscale_ref[...]
    for r in range(POOL_STATE - 1):
        po_ref[r] = sp_ref[r + 1]
    po_ref[POOL_STATE - 1] = u

    x1 = _mix_residual(x_ref[...], attn_ref[...], pool, g_attn_ref[...], g_pool_ref[...],
                       w_o_ref, g_post_mix_ref[...])
    h2 = _rms(x1, g_pre_ffn_ref[...]).astype(BF16)

    co_ref[:, 0, :] = sc_ref[:, 1, :]
    ffn = jnp.zeros(x1.shape, F32)
    for c in range(n_chunks):
        up = _dot(h2, _ff_cols(wup_ref, c, tc))
        gcols = slice(c * tc, (c + 1) * tc)
        vcols = slice(d_ff + c * tc, d_ff + (c + 1) * tc)
        co_ref[:, 1, gcols] = up[:, :tc]
        co_ref[:, 1, vcols] = up[:, tc:]
        old0 = jnp.concatenate([sc_ref[:, 0, gcols], sc_ref[:, 0, vcols]], axis=1)
        old1 = jnp.concatenate([sc_ref[:, 1, gcols], sc_ref[:, 1, vcols]], axis=1)
        cw = _ff_cols(cw_ref, c, tc)
        conv = _ff_cols(cb_ref, c, tc) + old0 * cw[0:1, :]
        conv = conv + old1 * cw[1:2, :]
        conv = conv + up * cw[2:3, :]
        ffn = ffn + _dot(_gated(conv, tc), wdown_ref[c * tc:(c + 1) * tc, :])
    y_ref[...] = x1 + _rms(ffn, g_post_ffn_ref[...])


def _vmem():
    return pl.BlockSpec(memory_space=pltpu.VMEM)


def _smem():
    return pl.BlockSpec(memory_space=pltpu.SMEM)


def _resident(shape):
    nd = len(shape)
    return pl.BlockSpec(shape, lambda *_: (0,) * nd, pipeline_mode=pl.Buffered(1))


def _unchunk_cols(a):
    n_chunks, r, tc2 = a.shape
    a = a.reshape(n_chunks, r, 2, tc2 // 2)
    return jnp.transpose(a, (1, 2, 0, 3)).reshape(r, n_chunks * tc2)


def kernel(x_prompt, x_sample, cache_k, cache_v, state_pool, state_conv, meta, w_in, b_in, sinks,
           w_pool, pool_scale, g_attn_out, g_pool_out, w_o, g_pre_mix, g_post_mix, g_pre_ffn,
           g_post_ffn, w_up, conv_w, conv_b, w_down):
    assert w_in.shape[0] == 1, "single layer"
    batch, seq, d_model = x_prompt.shape
    dec_batch = x_sample.shape[0]
    d_ff = w_down.shape[1]
    tc = FF_CHUNK
    n_chunks = d_ff // tc
    assert n_chunks * tc == d_ff and seq % SEQ_TILE == 0 and dec_batch % SAMPLE_TB == 0
    assert meta.shape[0] == N_META and N_META > POOL_STATE

    row = lambda a: a[0].reshape(1, -1)
    w_in_b = w_in[0].astype(BF16)
    b_in_r = row(b_in)
    w_pool_b = w_pool[0].astype(BF16)
    w_o_b = w_o[0].astype(BF16)
    wup_c = w_up[0].astype(BF16)
    half_value = jnp.concatenate([jnp.ones((d_ff,), F32), jnp.full((d_ff,), 0.5, F32)])
    cw_c = conv_w[0] * half_value
    cb_c = (conv_b[0] * half_value).reshape(1, -1)
    wdown_c = w_down[0].astype(BF16)
    sink_s = sinks[0]
    sink_col = sinks[0].reshape(N_HEADS, 1)
    slope_col = jnp.asarray(np.array(SLOPES, np.float32).reshape(N_HEADS, 1))
    gains = dict(pscale=row(pool_scale), g_attn=row(g_attn_out), g_pool=row(g_pool_out),
                 g_pre_mix=row(g_pre_mix), g_post_mix=row(g_post_mix),
                 g_pre_ffn=row(g_pre_ffn), g_post_ffn=row(g_post_ffn))

    x_meta = jnp.concatenate([jnp.zeros((WINDOW - N_META, d_model), F32), meta.astype(F32)], 0)
    k0, v0, u0, up0 = pl.pallas_call(
        _meta_kernel,
        out_shape=(jax.ShapeDtypeStruct((WINDOW, KV_W), F32),
                   jax.ShapeDtypeStruct((WINDOW, KV_W), F32),
                   jax.ShapeDtypeStruct((U_HIST, POOL_W), F32),
                   jax.ShapeDtypeStruct((n_chunks, CONV_STATE, 2 * tc), F32)),
        in_specs=[_vmem(), _vmem(), _vmem(), _smem()] + [_vmem()] * 9,
        out_specs=(_vmem(),) * 4,
        scratch_shapes=[pltpu.VMEM((U_HIST + WINDOW, POOL_W), F32)],
        compiler_params=pltpu.CompilerParams(vmem_limit_bytes=VMEM_LIMIT),
        name="meta",
    )(x_meta, w_in_b, b_in_r, sink_s, w_pool_b, gains["pscale"], gains["g_attn"],
      gains["g_pool"], w_o_b, gains["g_pre_mix"], gains["g_post_mix"], gains["g_pre_ffn"], wup_c)

    n_tiles = seq // SEQ_TILE
    n_total = batch * n_tiles
    mixer_tile = lambda g: jnp.minimum(g, n_total - 1)
    ffn_tile = lambda g: jnp.maximum(g - 1, 0)
    per_batch = lambda shape, tile: pl.BlockSpec(
        (1,) + shape, lambda g: (tile(g) // n_tiles,) + (0,) * len(shape))
    prompt_inputs = (
        x_prompt, k0, v0, u0, up0, w_in_b, b_in_r, sink_s, w_pool_b, gains["pscale"],
        gains["g_attn"], gains["g_pool"], w_o_b, gains["g_pre_mix"], gains["g_post_mix"],
        gains["g_pre_ffn"], gains["g_post_ffn"], wup_c, cw_c, cb_c, wdown_c)
    in_specs = [pl.BlockSpec((1, SEQ_TILE, d_model),
                             lambda g: (mixer_tile(g) // n_tiles, mixer_tile(g) % n_tiles, 0))]
    in_specs += [_smem() if a is sink_s else _resident(a.shape) for a in prompt_inputs[1:]]
    y_prompt, k_p, v_p, u_p, c_p = pl.pallas_call(
        functools.partial(_prompt_kernel, n_tiles),
        grid=(n_total + 1,),
        out_shape=(jax.ShapeDtypeStruct((batch, seq, d_model), F32),
                   jax.ShapeDtypeStruct((batch, WINDOW, KV_W), F32),
                   jax.ShapeDtypeStruct((batch, WINDOW, KV_W), F32),
                   jax.ShapeDtypeStruct((batch, U_HIST, POOL_W), F32),
                   jax.ShapeDtypeStruct((batch, n_chunks, CONV_STATE, 2 * tc), F32)),
        in_specs=in_specs,
        out_specs=(pl.BlockSpec((1, SEQ_TILE, d_model),
                                lambda g: (ffn_tile(g) // n_tiles, ffn_tile(g) % n_tiles, 0)),
                   per_batch((KV_W, WINDOW), mixer_tile), per_batch((KV_W, WINDOW), mixer_tile),
                   per_batch((U_HIST, POOL_W), mixer_tile),
                   per_batch((n_chunks, CONV_STATE, 2 * tc), ffn_tile)),
        scratch_shapes=[
            pltpu.VMEM((WINDOW + SEQ_TILE, KV_W), F32),
            pltpu.VMEM((WINDOW + SEQ_TILE, KV_W), F32),
            pltpu.VMEM((U_HIST + SEQ_TILE, POOL_W), F32),
            pltpu.VMEM((n_chunks, 2 * SUBLANES, 2 * tc), F32),
            pltpu.VMEM((SEQ_TILE, Q_W), F32),
            pltpu.VMEM((2, d_model // LANES, SEQ_TILE, LANES), F32),
            pltpu.VMEM((d_model // LANES, SEQ_TILE, LANES), F32),
            pltpu.VMEM((SEQ_TILE, Q_W), F32),
        ],
        compiler_params=pltpu.CompilerParams(
            dimension_semantics=("arbitrary",), vmem_limit_bytes=VMEM_LIMIT),
        name="prompt",
    )(*prompt_inputs)

    feat_pos = lambda c: jnp.transpose(c[0].reshape(dec_batch, WINDOW, KV_W), (0, 2, 1))
    ck, cv = feat_pos(cache_k), feat_pos(cache_v)
    xs = x_sample.reshape(dec_batch, d_model)
    tb = SAMPLE_TB
    whole = lambda shape: pl.BlockSpec(shape, lambda i: (0,) * len(shape))
    cache_spec = pl.BlockSpec((tb, KV_W, WINDOW), lambda i: (i, 0, 0))
    attn_s, u_s, k_s, v_s = pl.pallas_call(
        _sample_a_kernel,
        grid=(dec_batch // tb,),
        out_shape=(jax.ShapeDtypeStruct((dec_batch, Q_W), F32),
                   jax.ShapeDtypeStruct((dec_batch, POOL_W), F32),
                   jax.ShapeDtypeStruct((dec_batch, KV_W, WINDOW), F32),
                   jax.ShapeDtypeStruct((dec_batch, KV_W, WINDOW), F32)),
        in_specs=[whole(xs.shape), cache_spec, cache_spec, _resident(w_in_b.shape),
                  _resident(b_in_r.shape), _resident(gains["g_pre_mix"].shape),
                  _resident(sink_col.shape), _resident(slope_col.shape)],
        out_specs=(whole((dec_batch, Q_W)), whole((dec_batch, POOL_W)), cache_spec, cache_spec),
        scratch_shapes=[pltpu.VMEM((dec_batch, Q_W), F32), pltpu.VMEM((dec_batch, KV_W), F32),
                        pltpu.VMEM((dec_batch, KV_W), F32), pltpu.VMEM((KV_W, dec_batch), F32),
                        pltpu.VMEM((KV_W, dec_batch), F32)],
        compiler_params=pltpu.CompilerParams(
            dimension_semantics=("arbitrary",), vmem_limit_bytes=VMEM_LIMIT),
        name="sample_a",
    )(xs, ck, cv, w_in_b, b_in_r, gains["g_pre_mix"], sink_col, slope_col)

    sp = jnp.transpose(state_pool[0], (1, 0, 2))
    sc = state_conv[0]
    y_s, pool_s, conv_s = pl.pallas_call(
        _sample_b_kernel,
        out_shape=(jax.ShapeDtypeStruct((dec_batch, d_model), F32),
                   jax.ShapeDtypeStruct(sp.shape, F32),
                   jax.ShapeDtypeStruct(sc.shape, F32)),
        in_specs=[_vmem()] * 17,
        out_specs=(_vmem(),) * 3,
        compiler_params=pltpu.CompilerParams(vmem_limit_bytes=VMEM_LIMIT),
        name="sample_b",
    )(xs, attn_s, u_s, sp, sc, w_pool_b, gains["pscale"], gains["g_attn"], gains["g_pool"],
      w_o_b, gains["g_post_mix"], gains["g_pre_ffn"], gains["g_post_ffn"], wup_c, cw_c, cb_c,
      wdown_c)

    def pos_feat(c):
        n = c.shape[0]
        return jnp.transpose(c, (0, 2, 1)).reshape(1, n, WINDOW, N_KV_HEADS, HEAD_DIM)

    conv_p = jax.vmap(_unchunk_cols)(c_p)
    return (y_prompt,
            y_s.reshape(dec_batch, 1, d_model),
            pos_feat(k_p), pos_feat(v_p),
            u_p[:, U_HIST - POOL_STATE:, :][None],
            conv_p[None],
            pos_feat(k_s), pos_feat(v_s),
            jnp.transpose(pool_s, (1, 0, 2))[None],
            conv_s[None])
```

```python
import functools

import numpy as np
import jax
import jax.numpy as jnp
from jax import lax
from jax.experimental import pallas as pl
from jax.experimental.pallas import tpu as pltpu

F32 = jnp.float32
BF16 = jnp.bfloat16

N_META = 16
HEAD_DIM = 64
N_HEADS = 8
N_KV_HEADS = 2
WINDOW = 128
POOL_WINDOWS = (2, 4, 8, 16)
POOL_STATE = 15
CONV_STATE = 2
RMS_EPS = 1e-6
SM_SCALE = HEAD_DIM ** -0.5
SLOPES = tuple(2.0 ** (-(h + 1) * (8.0 / N_HEADS)) for h in range(N_HEADS))

LANES = 128
SUBLANES = 8
KV_W = N_KV_HEADS * HEAD_DIM
Q_W = N_HEADS * HEAD_DIM
POOL_W = 512
POOL_GW = POOL_W // len(POOL_WINDOWS)
U_HIST = 16

SEQ_TILE = 512
FF_CHUNK = 256
SAMPLE_TB = 32
SAMPLE_UNROLL = 8
VMEM_LIMIT = 56 * 1024 * 1024
TAIL_SLABS = 4
GELU_C1 = float(np.sqrt(2.0 / np.pi))
GELU_C2 = GELU_C1 * 0.044715


def _rms(x, g):
    ms = jnp.mean(x * x, axis=-1, keepdims=True)
    return x * lax.rsqrt(ms + RMS_EPS) * g


def _dot(a, b):
    return jnp.dot(a, b, preferred_element_type=F32)


def _dot_nt(a, b):
    return lax.dot_general(a, b, (((1,), (1,)), ((), ())), preferred_element_type=F32)


def _lane_lo(shape):
    return lax.broadcasted_iota(jnp.int32, shape, len(shape) - 1) < HEAD_DIM


def _attend_block(qs, k2, v2, pos_start, sink_of):
    nq, nk = qs.shape[0], k2.shape[0]
    qi = lax.broadcasted_iota(jnp.int32, (nq, nk), 0)
    kj = lax.broadcasted_iota(jnp.int32, (nq, nk), 1)
    valid = (kj >= qi) & (kj <= qi + WINDOW) & (kj >= WINDOW - pos_start)
    neg_dist = jnp.where(valid, (kj - qi - WINDOW).astype(F32), -jnp.inf)
    lo = _lane_lo((nq, LANES))
    group = N_HEADS // N_KV_HEADS
    q_heads = []
    for p in range(N_HEADS // 2):
        g = (2 * p) // group
        blk = qs[:, p * LANES:(p + 1) * LANES]
        rolled = pltpu.roll(blk, HEAD_DIM, axis=1)
        keep = lo if g == 0 else jnp.logical_not(lo)
        for e in range(2):
            q_heads.append(jnp.where(keep, blk if e == g else rolled, 0.0).astype(BF16))
    s_all = _dot_nt(jnp.concatenate(q_heads, axis=0), k2)
    probs, dens = [], []
    for h in range(N_HEADS):
        s = s_all[h * nq:(h + 1) * nq, :] + SLOPES[h] * neg_dist
        sink = sink_of(h)
        m = jnp.maximum(jnp.max(s, axis=-1, keepdims=True), sink)
        pe = jnp.exp(s - m)
        dens.append(jnp.sum(pe, axis=-1, keepdims=True) + jnp.exp(sink - m))
        probs.append(pe.astype(BF16))
    o_all = _dot(jnp.concatenate(probs, axis=0), v2)
    outs = [o_all[h * nq:(h + 1) * nq, :] / dens[h] for h in range(N_HEADS)]
    pieces = []
    for p in range(N_HEADS // 2):
        a, b = outs[2 * p], outs[2 * p + 1]
        if (2 * p) // group == 0:
            pieces.append(jnp.where(lo, a, pltpu.roll(b, HEAD_DIM, axis=1)))
        else:
            pieces.append(jnp.where(lo, pltpu.roll(a, HEAD_DIM, axis=1), b))
    return jnp.concatenate(pieces, axis=1)


def _pool_rows(ubuf, rows, w_pool_ref, pool_scale, cnt_of):
    outs = []
    for g, w in enumerate(POOL_WINDOWS):
        cs = slice(g * POOL_GW, (g + 1) * POOL_GW)
        cur = ubuf[U_HIST:U_HIST + rows, cs]
        acc = cur
        for j in range(1, w):
            acc = acc + ubuf[U_HIST - j:U_HIST - j + rows, cs]
        mean = acc / cnt_of(w)
        outs.append(_dot((mean - cur).astype(BF16), w_pool_ref[g]))
    return jnp.concatenate(outs, axis=1) * pool_scale


def _pool_rows_full(ubuf, rows, w_pool_ref, pool_scale):
    outs = []
    for g, w in enumerate(POOL_WINDOWS):
        h = ubuf[:, g * POOL_GW:(g + 1) * POOL_GW]
        s, k = h, 1
        while k < w:
            if k < SUBLANES:
                s = s + pltpu.roll(s, k, axis=0)
            else:
                s = s + jnp.concatenate([s[:k, :], s[:-k, :]], axis=0)
            k *= 2
        cur = h[U_HIST:, :]
        outs.append(_dot((s[U_HIST:, :] * (1.0 / w) - cur).astype(BF16), w_pool_ref[g]))
    return jnp.concatenate(outs, axis=1) * pool_scale


def _mix_residual(x, attn, pool, g_attn, g_pool, w_o_ref, g_post_mix):
    mixin = jnp.concatenate([_rms(attn, g_attn), _rms(pool, g_pool)], axis=1).astype(BF16)
    return x + _rms(_dot(mixin, w_o_ref[...]), g_post_mix)


def _gated(conv, tc):
    g, hv = conv[:, :tc], conv[:, tc:]
    inner = g * (GELU_C1 + GELU_C2 * (g * g))
    return (g * (1.0 + jnp.tanh(inner)) * hv).astype(BF16)


def _ff_cols(ref, c, tc):
    d_ff = ref.shape[-1] // 2
    return jnp.concatenate([ref[:, c * tc:(c + 1) * tc],
                            ref[:, d_ff + c * tc:d_ff + (c + 1) * tc]], axis=1)


def _row_slabs(rows):
    na = rows // SUBLANES - TAIL_SLABS
    assert na % 4 == 0 and (na // 4) % 2 == 1
    slabs = [(v, na) for v in range(na)]
    slabs += [(SUBLANES * na + v, TAIL_SLABS) for v in range(TAIL_SLABS)]
    return slabs, na


def _shifted_rows(ub, prev, na):
    S = SUBLANES
    width = ub.shape[1]
    sub = lax.broadcasted_iota(jnp.int32, (S, width), 0)
    slab = lambda i: ub[i * S:(i + 1) * S, :]

    def wrap(x, y):
        return pltpu.roll(jnp.where(sub == S - 1, y, x), 1, axis=0)

    a1 = wrap(slab(na - 1), prev[S:2 * S, :])
    a2 = wrap(slab(na - 2), prev[0:S, :])
    b1 = wrap(slab(na + 3), slab(na - 1))
    b2 = wrap(slab(na + 2), slab(na - 2))
    s1 = jnp.concatenate([a1, ub[0:(na - 1) * S, :], b1, ub[na * S:(na + 3) * S, :]], axis=0)
    s2 = jnp.concatenate([a2, a1, ub[0:(na - 2) * S, :], b2, b1, ub[na * S:(na + 2) * S, :]],
                         axis=0)
    return s1, s2


def _meta_kernel(x_ref, w_in_ref, b_in_ref, sink_ref, w_pool_ref, pscale_ref, g_attn_ref,
                 g_pool_ref, w_o_ref, g_pre_mix_ref, g_post_mix_ref, g_pre_ffn_ref, wup_ref,
                 k0_ref, v0_ref, u0_ref, up0_ref, ubuf):
    rows = x_ref.shape[0]
    pos0 = N_META - rows
    x = x_ref[...]
    z = _dot(_rms(x, g_pre_mix_ref[...]).astype(BF16), w_in_ref[...]) + b_in_ref[...]
    k = z[:, Q_W:Q_W + KV_W]
    v = z[:, Q_W + KV_W:Q_W + 2 * KV_W]
    u = z[:, Q_W + 2 * KV_W:]
    pos = pos0 + lax.broadcasted_iota(jnp.int32, (rows, 1), 0)
    zeros_kv = jnp.zeros((WINDOW, KV_W), BF16)
    k2 = jnp.concatenate([zeros_kv, k.astype(BF16)], axis=0)
    v2 = jnp.concatenate([zeros_kv, v.astype(BF16)], axis=0)
    attn = _attend_block(z[:, :Q_W] * SM_SCALE, k2, v2, pos0, lambda h: sink_ref[h])

    ubuf[0:U_HIST, :] = jnp.zeros((U_HIST, POOL_W), F32)
    ubuf[U_HIST:, :] = jnp.where(pos >= 0, u, 0.0)
    cnt_of = lambda w: jnp.clip(pos + 1, 1, w).astype(F32)
    pool = _pool_rows(ubuf, rows, w_pool_ref, pscale_ref[...], cnt_of)

    x1 = _mix_residual(x, attn, pool, g_attn_ref[...], g_pool_ref[...], w_o_ref,
                       g_post_mix_ref[...])
    h2 = _rms(x1, g_pre_ffn_ref[...]).astype(BF16)
    k0_ref[...] = k
    v0_ref[...] = v
    u0_ref[...] = u[rows - U_HIST:, :]
    tail = 2 * SUBLANES
    tc = up0_ref.shape[2] // 2
    for c in range(up0_ref.shape[0]):
        up_tail = _dot(h2[rows - tail:, :], _ff_cols(wup_ref, c, tc))
        up0_ref[c] = up_tail[tail - CONV_STATE:, :]


def _prompt_kernel(n_tiles, x_ref, k0_ref, v0_ref, u0_ref, up0_ref, w_in_ref, b_in_ref,
                   sink_ref, w_pool_ref, pscale_ref, g_attn_ref, g_pool_ref, w_o_ref,
                   g_pre_mix_ref, g_post_mix_ref, g_pre_ffn_ref, g_post_ffn_ref, wup_ref, cw_ref,
                   cb_ref, wdown_ref,
                   y_ref, kout_ref, vout_ref, uout_ref, cout_ref,
                   kbuf, vbuf, ubuf, upst, attn_buf, xbuf, obuf, qsbuf):
    g = pl.program_id(0)
    n_total = pl.num_programs(0) - 1
    rows = x_ref.shape[1]
    n_chunks, _, tc2 = upst.shape
    tc = tc2 // 2
    n_col = x_ref.shape[2] // LANES
    S = SUBLANES
    slabs, na = _row_slabs(rows)
    st2_row, st1_row = S - 1, 2 * S - 1

    tf = lax.rem(jnp.minimum(g, n_total - 1), n_tiles)
    tj = lax.rem(jnp.maximum(g - 1, 0), n_tiles)
    wslot = lax.rem(g, 2)
    rslot = 1 - wslot

    @pl.when(tf == 0)
    def _():
        kbuf[0:WINDOW, :] = k0_ref[...]
        vbuf[0:WINDOW, :] = v0_ref[...]
        ubuf[0:U_HIST, :] = u0_ref[...]

    @pl.when(tj == 0)
    def _():
        upst[...] = jnp.zeros(upst.shape, F32)
        upst[:, st2_row:st2_row + 1, :] = up0_ref[:, 0:1, :]
        upst[:, st1_row:st1_row + 1, :] = up0_ref[:, 1:2, :]

    @pl.when(g == 0)
    def _():
        xbuf[1] = jnp.zeros(xbuf.shape[1:], F32)

    def mixer_in():
        z = _dot(_rms(x_ref[0], g_pre_mix_ref[...]).astype(BF16), w_in_ref[...]) + b_in_ref[...]
        kbuf[WINDOW:, :] = z[:, Q_W:Q_W + KV_W]
        vbuf[WINDOW:, :] = z[:, Q_W + KV_W:Q_W + 2 * KV_W]
        ubuf[U_HIST:, :] = z[:, Q_W + 2 * KV_W:]
        qsbuf[...] = z[:, :Q_W] * SM_SCALE

    def mixer_attend(i):
        r0 = i * WINDOW
        k2 = kbuf[r0:r0 + 2 * WINDOW, :].astype(BF16)
        v2 = vbuf[r0:r0 + 2 * WINDOW, :].astype(BF16)
        attn_buf[r0:r0 + WINDOW, :] = _attend_block(
            qsbuf[r0:r0 + WINDOW, :], k2, v2, N_META + tf * rows + r0, lambda h: sink_ref[h])

    def mixer_out():
        pool = _pool_rows_full(ubuf, rows, w_pool_ref, pscale_ref[...])
        x1 = _mix_residual(x_ref[0], attn_buf[...], pool, g_attn_ref[...], g_pool_ref[...],
                           w_o_ref, g_post_mix_ref[...])
        for j in range(n_col):
            xbuf[wslot, j] = x1[:, j * LANES:(j + 1) * LANES]
        kbuf[0:WINDOW, :] = kbuf[rows:rows + WINDOW, :]
        vbuf[0:WINDOW, :] = vbuf[rows:rows + WINDOW, :]
        ubuf[0:U_HIST, :] = ubuf[rows:rows + U_HIST, :]

    def x1_slab_order():
        return jnp.concatenate(
            [jnp.concatenate([xbuf[rslot, j, pl.ds(start, S, stride=stride), :]
                              for j in range(n_col)], axis=1) for start, stride in slabs], axis=0)

    ffn_vals = {}

    def ffn_in():
        ffn_vals["h2"] = _rms(x1_slab_order(), g_pre_ffn_ref[...]).astype(BF16)

    def up_project(c):
        ffn_vals["up", c] = _dot(ffn_vals["h2"], _ff_cols(wup_ref, c, tc))

    def activate(c):
        ub = ffn_vals.pop(("up", c))
        s1, s2 = _shifted_rows(ub, upst[c], na)
        upst[c] = ub[rows - 2 * S:rows, :]
        cw = _ff_cols(cw_ref, c, tc)
        conv = _ff_cols(cb_ref, c, tc) + s2 * cw[0:1, :]
        conv = conv + s1 * cw[1:2, :]
        conv = conv + ub * cw[2:3, :]
        ffn_vals["a", c] = _gated(conv, tc)

    def down_project(c):
        part = _dot(ffn_vals.pop(("a", c)), wdown_ref[c * tc:(c + 1) * tc, :])
        ffn_vals["acc"] = part if c == 0 else ffn_vals["acc"] + part

    def ffn_stage(c):
        if c + 1 < n_chunks:
            up_project(c + 1)
        if c < n_chunks:
            activate(c)
        if c >= 1:
            down_project(c - 1)

    def ffn_out():
        y = x1_slab_order() + _rms(ffn_vals["acc"], g_post_ffn_ref[...])
        for i, (start, stride) in enumerate(slabs):
            for j in range(n_col):
                obuf[j, pl.ds(start, S, stride=stride), :] = y[i * S:(i + 1) * S,
                                                              j * LANES:(j + 1) * LANES]
        y_ref[0] = jnp.concatenate([obuf[j] for j in range(n_col)], axis=1)

    ffn_stages = [lambda c=c: ffn_stage(c) for c in range(n_chunks + 1)]
    mixer_stages = [mixer_in] + [lambda i=i: mixer_attend(i) for i in range(rows // WINDOW)]
    mixer_stages.append(mixer_out)

    ffn_in()
    up_project(0)
    emitted = 0
    for c, stage in enumerate(ffn_stages):
        due = ((c + 1) * len(mixer_stages)) // len(ffn_stages)
        while emitted < due:
            mixer_stages[emitted]()
            emitted += 1
        stage()
    ffn_out()

    @pl.when((tf == n_tiles - 1) & (g < n_total))
    def _():
        kout_ref[0] = kbuf[0:WINDOW, :].T
        vout_ref[0] = vbuf[0:WINDOW, :].T
        uout_ref[0] = ubuf[0:U_HIST, :]

    @pl.when((tj == n_tiles - 1) & (g > 0))
    def _():
        cout_ref[0, :, 0:1, :] = upst[:, st2_row:st2_row + 1, :]
        cout_ref[0, :, 1:2, :] = upst[:, st1_row:st1_row + 1, :]


def _sample_a_kernel(x_ref, ck_ref, cv_ref, w_in_ref, b_in_ref, g_pre_mix_ref, sinkcol_ref,
                     slopecol_ref,
                     attn_ref, u_ref, ko_ref, vo_ref,
                     qs_buf, kn_buf, vn_buf, knt_buf, vnt_buf):
    step = pl.program_id(0)
    n_tok = ck_ref.shape[0]

    @pl.when(step == 0)
    def _():
        z = _dot(_rms(x_ref[...], g_pre_mix_ref[...]).astype(BF16), w_in_ref[...]) + b_in_ref[...]
        k = z[:, Q_W:Q_W + KV_W]
        v = z[:, Q_W + KV_W:Q_W + 2 * KV_W]
        qs_buf[...] = z[:, :Q_W] * SM_SCALE
        kn_buf[...] = k
        vn_buf[...] = v
        knt_buf[...] = k.T
        vnt_buf[...] = v.T
        u_ref[...] = z[:, Q_W + 2 * KV_W:]

    lo1 = _lane_lo((1, LANES))
    kj = lax.broadcasted_iota(jnp.int32, (N_HEADS, WINDOW), 1)
    neg_dist = (kj - WINDOW).astype(F32)
    bias = slopecol_ref[...] * neg_dist
    sink = sinkcol_ref[...]
    last_lane = lax.broadcasted_iota(jnp.int32, (KV_W, WINDOW), 1) == WINDOW - 1

    def token(j, carry):
        b = step * n_tok + j
        qrow = qs_buf[pl.ds(b, 1), :]
        kn = kn_buf[pl.ds(b, 1), :]
        vn = vn_buf[pl.ds(b, 1), :]
        kt = ck_ref[j]
        vt = cv_ref[j]
        heads = []
        for h in range(N_HEADS):
            p, e, g = h // 2, h % 2, h // (N_HEADS // N_KV_HEADS)
            blk = qrow[:, p * LANES:(p + 1) * LANES]
            src = blk if e == g else pltpu.roll(blk, HEAD_DIM, axis=1)
            keep = lo1 if g == 0 else jnp.logical_not(lo1)
            heads.append(jnp.where(keep, src, 0.0))
        qf = jnp.concatenate(heads, axis=0).astype(BF16)
        s = _dot(qf, kt.astype(BF16)) + bias
        s_self = jnp.sum(qf.astype(F32) * kn.astype(BF16).astype(F32), axis=-1, keepdims=True)
        m = jnp.maximum(jnp.maximum(jnp.max(s, axis=-1, keepdims=True), s_self), sink)
        pe = jnp.exp(s - m)
        pe_self = jnp.exp(s_self - m)
        den = jnp.sum(pe, axis=-1, keepdims=True) + pe_self + jnp.exp(sink - m)
        o = _dot_nt(pe.astype(BF16), vt.astype(BF16))
        o = o + pe_self.astype(BF16).astype(F32) * vn.astype(BF16).astype(F32)
        o = o / den
        pieces = []
        for p in range(N_HEADS // 2):
            g = (2 * p) // (N_HEADS // N_KV_HEADS)
            a, c = o[2 * p:2 * p + 1, :], o[2 * p + 1:2 * p + 2, :]
            if g == 0:
                pieces.append(jnp.where(lo1, a, pltpu.roll(c, HEAD_DIM, axis=1)))
            else:
                pieces.append(jnp.where(lo1, pltpu.roll(a, HEAD_DIM, axis=1), c))
        attn_ref[pl.ds(b, 1), :] = jnp.concatenate(pieces, axis=1)
        bring = WINDOW - 1 - b
        ko_ref[j] = jnp.where(last_lane, pltpu.roll(knt_buf[...], bring, axis=1),
                              pltpu.roll(kt, WINDOW - 1, axis=1))
        vo_ref[j] = jnp.where(last_lane, pltpu.roll(vnt_buf[...], bring, axis=1),
                              pltpu.roll(vt, WINDOW - 1, axis=1))
        return carry

    lax.fori_loop(0, n_tok, token, 0, unroll=SAMPLE_UNROLL)


def _sample_b_kernel(x_ref, attn_ref, u_ref, sp_ref, sc_ref, w_pool_ref, pscale_ref,
                     g_attn_ref, g_pool_ref, w_o_ref, g_post_mix_ref, g_pre_ffn_ref,
                     g_post_ffn_ref, wup_ref, cw_ref, cb_ref, wdown_ref,
                     y_ref, po_ref, co_ref):
    tc = FF_CHUNK
    d_ff = wdown_ref.shape[0]
    n_chunks = d_ff // tc
    u = u_ref[...]

    outs = []
    for g, w in enumerate(POOL_WINDOWS):
        cs = slice(g * POOL_GW, (g + 1) * POOL_GW)
        cur = u[:, cs]
        acc_u = cur
        for j in range(1, w):
            acc_u = acc_u + sp_ref[POOL_STATE - j, :, cs]
        outs.append(_dot((acc_u / float(w) - cur).astype(BF16), w_pool_ref[g]))
    pool = jnp.concatenate(outs, axis=1) * pscale_ref[...]
    for r in range(POOL_STATE - 1):
        po_ref[r] = sp_ref[r + 1]
    po_ref[POOL_STATE - 1] = u

    x1 = _mix_residual(x_ref[...], attn_ref[...], pool, g_attn_ref[...], g_pool_ref[...],
                       w_o_ref, g_post_mix_ref[...])
    h2 = _rms(x1, g_pre_ffn_ref[...]).astype(BF16)

    co_ref[:, 0, :] = sc_ref[:, 1, :]
    ffn = jnp.zeros(x1.shape, F32)
    for c in range(n_chunks):
        up = _dot(h2, _ff_cols(wup_ref, c, tc))
        gcols = slice(c * tc, (c + 1) * tc)
        vcols = slice(d_ff + c * tc, d_ff + (c + 1) * tc)
        co_ref[:, 1, gcols] = up[:, :tc]
        co_ref[:, 1, vcols] = up[:, tc:]
        old0 = jnp.concatenate([sc_ref[:, 0, gcols], sc_ref[:, 0, vcols]], axis=1)
        old1 = jnp.concatenate([sc_ref[:, 1, gcols], sc_ref[:, 1, vcols]], axis=1)
        cw = _ff_cols(cw_ref, c, tc)
        conv = _ff_cols(cb_ref, c, tc) + old0 * cw[0:1, :]
        conv = conv + old1 * cw[1:2, :]
        conv = conv + up * cw[2:3, :]
        ffn = ffn + _dot(_gated(conv, tc), wdown_ref[c * tc:(c + 1) * tc, :])
    y_ref[...] = x1 + _rms(ffn, g_post_ffn_ref[...])


def _vmem():
    return pl.BlockSpec(memory_space=pltpu.VMEM)


def _smem():
    return pl.BlockSpec(memory_space=pltpu.SMEM)


def _resident(shape):
    nd = len(shape)
    return pl.BlockSpec(shape, lambda *_: (0,) * nd, pipeline_mode=pl.Buffered(1))


def _unchunk_cols(a):
    n_chunks, r, tc2 = a.shape
    a = a.reshape(n_chunks, r, 2, tc2 // 2)
    return jnp.transpose(a, (1, 2, 0, 3)).reshape(r, n_chunks * tc2)


def kernel(x_prompt, x_sample, cache_k, cache_v, state_pool, state_conv, meta, w_in, b_in, sinks,
           w_pool, pool_scale, g_attn_out, g_pool_out, w_o, g_pre_mix, g_post_mix, g_pre_ffn,
           g_post_ffn, w_up, conv_w, conv_b, w_down):
    assert w_in.shape[0] == 1, "single layer"
    batch, seq, d_model = x_prompt.shape
    dec_batch = x_sample.shape[0]
    d_ff = w_down.shape[1]
    tc = FF_CHUNK
    n_chunks = d_ff // tc
    assert n_chunks * tc == d_ff and seq % SEQ_TILE == 0 and dec_batch % SAMPLE_TB == 0
    assert meta.shape[0] == N_META and N_META > POOL_STATE

    row = lambda a: a[0].reshape(1, -1)
    w_in_b = w_in[0].astype(BF16)
    b_in_r = row(b_in)
    w_pool_b = w_pool[0].astype(BF16)
    w_o_b = w_o[0].astype(BF16)
    wup_c = w_up[0].astype(BF16)
    half_value = jnp.concatenate([jnp.ones((d_ff,), F32), jnp.full((d_ff,), 0.5, F32)])
    cw_c = conv_w[0] * half_value
    cb_c = (conv_b[0] * half_value).reshape(1, -1)
    wdown_c = w_down[0].astype(BF16)
    sink_s = sinks[0]
    sink_col = sinks[0].reshape(N_HEADS, 1)
    slope_col = jnp.asarray(np.array(SLOPES, np.float32).reshape(N_HEADS, 1))
    gains = dict(pscale=row(pool_scale), g_attn=row(g_attn_out), g_pool=row(g_pool_out),
                 g_pre_mix=row(g_pre_mix), g_post_mix=row(g_post_mix),
                 g_pre_ffn=row(g_pre_ffn), g_post_ffn=row(g_post_ffn))

    x_meta = jnp.concatenate([jnp.zeros((WINDOW - N_META, d_model), F32), meta.astype(F32)], 0)
    k0, v0, u0, up0 = pl.pallas_call(
        _meta_kernel,
        out_shape=(jax.ShapeDtypeStruct((WINDOW, KV_W), F32),
                   jax.ShapeDtypeStruct((WINDOW, KV_W), F32),
                   jax.ShapeDtypeStruct((U_HIST, POOL_W), F32),
                   jax.ShapeDtypeStruct((n_chunks, CONV_STATE, 2 * tc), F32)),
        in_specs=[_vmem(), _vmem(), _vmem(), _smem()] + [_vmem()] * 9,
        out_specs=(_vmem(),) * 4,
        scratch_shapes=[pltpu.VMEM((U_HIST + WINDOW, POOL_W), F32)],
        compiler_params=pltpu.CompilerParams(vmem_limit_bytes=VMEM_LIMIT),
        name="meta",
    )(x_meta, w_in_b, b_in_r, sink_s, w_pool_b, gains["pscale"], gains["g_attn"],
      gains["g_pool"], w_o_b, gains["g_pre_mix"], gains["g_post_mix"], gains["g_pre_ffn"], wup_c)

    n_tiles = seq // SEQ_TILE
    n_total = batch * n_tiles
    mixer_tile = lambda g: jnp.minimum(g, n_total - 1)
    ffn_tile = lambda g: jnp.maximum(g - 1, 0)
    per_batch = lambda shape, tile: pl.BlockSpec(
        (1,) + shape, lambda g: (tile(g) // n_tiles,) + (0,) * len(shape))
    prompt_inputs = (
        x_prompt, k0, v0, u0, up0, w_in_b, b_in_r, sink_s, w_pool_b, gains["pscale"],
        gains["g_attn"], gains["g_pool"], w_o_b, gains["g_pre_mix"], gains["g_post_mix"],
        gains["g_pre_ffn"], gains["g_post_ffn"], wup_c, cw_c, cb_c, wdown_c)
    in_specs = [pl.BlockSpec((1, SEQ_TILE, d_model),
                             lambda g: (mixer_tile(g) // n_tiles, mixer_tile(g) % n_tiles, 0))]
    in_specs += [_smem() if a is sink_s else _resident(a.shape) for a in prompt_inputs[1:]]
    y_prompt, k_p, v_p, u_p, c_p = pl.pallas_call(
        functools.partial(_prompt_kernel, n_tiles),
        grid=(n_total + 1,),
        out_shape=(jax.ShapeDtypeStruct((batch, seq, d_model), F32),
                   jax.ShapeDtypeStruct((batch, WINDOW, KV_W), F32),
                   jax.ShapeDtypeStruct((batch, WINDOW, KV_W), F32),
                   jax.ShapeDtypeStruct((batch, U_HIST, POOL_W), F32),
                   jax.ShapeDtypeStruct((batch, n_chunks, CONV_STATE, 2 * tc), F32)),
        in_specs=in_specs,
        out_specs=(pl.BlockSpec((1, SEQ_TILE, d_model),
                                lambda g: (ffn_tile(g) // n_tiles, ffn_tile(g) % n_tiles, 0)),
                   per_batch((KV_W, WINDOW), mixer_tile), per_batch((KV_W, WINDOW), mixer_tile),
                   per_batch((U_HIST, POOL_W), mixer_tile),
                   per_batch((n_chunks, CONV_STATE, 2 * tc), ffn_tile)),
        scratch_shapes=[
            pltpu.VMEM((WINDOW + SEQ_TILE, KV_W), F32),
            pltpu.VMEM((WINDOW + SEQ_TILE, KV_W), F32),
            pltpu.VMEM((U_HIST + SEQ_TILE, POOL_W), F32),
            pltpu.VMEM((n_chunks, 2 * SUBLANES, 2 * tc), F32),
            pltpu.VMEM((SEQ_TILE, Q_W), F32),
            pltpu.VMEM((2, d_model // LANES, SEQ_TILE, LANES), F32),
            pltpu.VMEM((d_model // LANES, SEQ_TILE, LANES), F32),
            pltpu.VMEM((SEQ_TILE, Q_W), F32),
        ],
        compiler_params=pltpu.CompilerParams(
            dimension_semantics=("arbitrary",), vmem_limit_bytes=VMEM_LIMIT),
        name="prompt",
    )(*prompt_inputs)

    feat_pos = lambda c: jnp.transpose(c[0].reshape(dec_batch, WINDOW, KV_W), (0, 2, 1))
    ck, cv = feat_pos(cache_k), feat_pos(cache_v)
    xs = x_sample.reshape(dec_batch, d_model)
    tb = SAMPLE_TB
    whole = lambda shape: pl.BlockSpec(shape, lambda i: (0,) * len(shape))
    cache_spec = pl.BlockSpec((tb, KV_W, WINDOW), lambda i: (i, 0, 0))
    attn_s, u_s, k_s, v_s = pl.pallas_call(
        _sample_a_kernel,
        grid=(dec_batch // tb,),
        out_shape=(jax.ShapeDtypeStruct((dec_batch, Q_W), F32),
                   jax.ShapeDtypeStruct((dec_batch, POOL_W), F32),
                   jax.ShapeDtypeStruct((dec_batch, KV_W, WINDOW), F32),
                   jax.ShapeDtypeStruct((dec_batch, KV_W, WINDOW), F32)),
        in_specs=[whole(xs.shape), cache_spec, cache_spec, _resident(w_in_b.shape),
                  _resident(b_in_r.shape), _resident(gains["g_pre_mix"].shape),
                  _resident(sink_col.shape), _resident(slope_col.shape)],
        out_specs=(whole((dec_batch, Q_W)), whole((dec_batch, POOL_W)), cache_spec, cache_spec),
        scratch_shapes=[pltpu.VMEM((dec_batch, Q_W), F32), pltpu.VMEM((dec_batch, KV_W), F32),
                        pltpu.VMEM((dec_batch, KV_W), F32), pltpu.VMEM((KV_W, dec_batch), F32),
                        pltpu.VMEM((KV_W, dec_batch), F32)],
        compiler_params=pltpu.CompilerParams(
            dimension_semantics=("arbitrary",), vmem_limit_bytes=VMEM_LIMIT),
        name="sample_a",
    )(xs, ck, cv, w_in_b, b_in_r, gains["g_pre_mix"], sink_col, slope_col)

    sp = jnp.transpose(state_pool[0], (1, 0, 2))
    sc = state_conv[0]
    y_s, pool_s, conv_s = pl.pallas_call(
        _sample_b_kernel,
        out_shape=(jax.ShapeDtypeStruct((dec_batch, d_model), F32),
                   jax.ShapeDtypeStruct(sp.shape, F32),
                   jax.ShapeDtypeStruct(sc.shape, F32)),
        in_specs=[_vmem()] * 17,
        out_specs=(_vmem(),) * 3,
        compiler_params=pltpu.CompilerParams(vmem_limit_bytes=VMEM_LIMIT),
        name="sample_b",
    )(xs, attn_s, u_s, sp, sc, w_pool_b, gains["pscale"], gains["g_attn"], gains["g_pool"],
      w_o_b, gains["g_post_mix"], gains["g_pre_ffn"], gains["g_post_ffn"], wup_c, cw_c, cb_c,
      wdown_c)

    def pos_feat(c):
        n = c.shape[0]
        return jnp.transpose(c, (0, 2, 1)).reshape(1, n, WINDOW, N_KV_HEADS, HEAD_DIM)

    conv_p = jax.vmap(_unchunk_cols)(c_p)
    return (y_prompt,
            y_s.reshape(dec_batch, 1, d_model),
            pos_feat(k_p), pos_feat(v_p),
            u_p[:, U_HIST - POOL_STATE:, :][None],
            conv_p[None],
            pos_feat(k_s), pos_feat(v_s),
            jnp.transpose(pool_s, (1, 0, 2))[None],
            conv_s[None])
```

```python
import functools

import numpy as np
import jax
import jax.numpy as jnp
from jax import lax
from jax.experimental import pallas as pl
from jax.experimental.pallas import tpu as pltpu

F32 = jnp.float32
BF16 = jnp.bfloat16

N_META = 16
HEAD_DIM = 64
N_HEADS = 8
N_KV_HEADS = 2
WINDOW = 128
POOL_WINDOWS = (2, 4, 8, 16)
POOL_STATE = 15
CONV_STATE = 2
RMS_EPS = 1e-6
SM_SCALE = HEAD_DIM ** -0.5
SLOPES = tuple(2.0 ** (-(h + 1) * (8.0 / N_HEADS)) for h in range(N_HEADS))

LANES = 128
SUBLANES = 8
KV_W = N_KV_HEADS * HEAD_DIM
Q_W = N_HEADS * HEAD_DIM
POOL_W = 512
POOL_GW = POOL_W // len(POOL_WINDOWS)
U_HIST = 16

SEQ_TILE = 512
FF_CHUNK = 256
SAMPLE_TB = 32
SAMPLE_UNROLL = 8
VMEM_LIMIT = 56 * 1024 * 1024
TAIL_SLABS = 4
GELU_C1 = float(np.sqrt(2.0 / np.pi))
GELU_C2 = GELU_C1 * 0.044715


def _rms(x, g):
    ms = jnp.mean(x * x, axis=-1, keepdims=True)
    return x * lax.rsqrt(ms + RMS_EPS) * g


def _dot(a, b):
    return jnp.dot(a, b, preferred_element_type=F32)


def _dot_nt(a, b):
    return lax.dot_general(a, b, (((1,), (1,)), ((), ())), preferred_element_type=F32)


def _lane_lo(shape):
    return lax.broadcasted_iota(jnp.int32, shape, len(shape) - 1) < HEAD_DIM


def _attend_block(qs, k2, v2, pos_start, sink_of):
    nq, nk = qs.shape[0], k2.shape[0]
    qi = lax.broadcasted_iota(jnp.int32, (nq, nk), 0)
    kj = lax.broadcasted_iota(jnp.int32, (nq, nk), 1)
    valid = (kj >= qi) & (kj <= qi + WINDOW) & (kj >= WINDOW - pos_start)
    neg_dist = jnp.where(valid, (kj - qi - WINDOW).astype(F32), -jnp.inf)
    lo = _lane_lo((nq, LANES))
    group = N_HEADS // N_KV_HEADS
    q_heads = []
    for p in range(N_HEADS // 2):
        g = (2 * p) // group
        blk = qs[:, p * LANES:(p + 1) * LANES]
        rolled = pltpu.roll(blk, HEAD_DIM, axis=1)
        keep = lo if g == 0 else jnp.logical_not(lo)
        for e in range(2):
            q_heads.append(jnp.where(keep, blk if e == g else rolled, 0.0).astype(BF16))
    s_all = _dot_nt(jnp.concatenate(q_heads, axis=0), k2)
    probs, dens = [], []
    for h in range(N_HEADS):
        s = s_all[h * nq:(h + 1) * nq, :] + SLOPES[h] * neg_dist
        sink = sink_of(h)
        m = jnp.maximum(jnp.max(s, axis=-1, keepdims=True), sink)
        pe = jnp.exp(s - m)
        dens.append(jnp.sum(pe, axis=-1, keepdims=True) + jnp.exp(sink - m))
        probs.append(pe.astype(BF16))
    o_all = _dot(jnp.concatenate(probs, axis=0), v2)
    outs = [o_all[h * nq:(h + 1) * nq, :] / dens[h] for h in range(N_HEADS)]
    pieces = []
    for p in range(N_HEADS // 2):
        a, b = outs[2 * p], outs[2 * p + 1]
        if (2 * p) // group == 0:
            pieces.append(jnp.where(lo, a, pltpu.roll(b, HEAD_DIM, axis=1)))
        else:
            pieces.append(jnp.where(lo, pltpu.roll(a, HEAD_DIM, axis=1), b))
    return jnp.concatenate(pieces, axis=1)


def _pool_rows(ubuf, rows, w_pool_ref, pool_scale, cnt_of):
    outs = []
    for g, w in enumerate(POOL_WINDOWS):
        cs = slice(g * POOL_GW, (g + 1) * POOL_GW)
        cur = ubuf[U_HIST:U_HIST + rows, cs]
        acc = cur
        for j in range(1, w):
            acc = acc + ubuf[U_HIST - j:U_HIST - j + rows, cs]
        mean = acc / cnt_of(w)
        outs.append(_dot((mean - cur).astype(BF16), w_pool_ref[g]))
    return jnp.concatenate(outs, axis=1) * pool_scale


def _pool_rows_full(ubuf, rows, w_pool_ref, pool_scale):
    outs = []
    for g, w in enumerate(POOL_WINDOWS):
        h = ubuf[:, g * POOL_GW:(g + 1) * POOL_GW]
        s, k = h, 1
        while k < w:
            if k < SUBLANES:
                s = s + pltpu.roll(s, k, axis=0)
            else:
                s = s + jnp.concatenate([s[:k, :], s[:-k, :]], axis=0)
            k *= 2
        cur = h[U_HIST:, :]
        outs.append(_dot((s[U_HIST:, :] * (1.0 / w) - cur).astype(BF16), w_pool_ref[g]))
    return jnp.concatenate(outs, axis=1) * pool_scale


def _mix_residual(x, attn, pool, g_attn, g_pool, w_o_ref, g_post_mix):
    mixin = jnp.concatenate([_rms(attn, g_attn), _rms(pool, g_pool)], axis=1).astype(BF16)
    return x + _rms(_dot(mixin, w_o_ref[...]), g_post_mix)


def _gated(conv, tc):
    g, hv = conv[:, :tc], conv[:, tc:]
    inner = g * (GELU_C1 + GELU_C2 * (g * g))
    return (g * (1.0 + jnp.tanh(inner)) * hv).astype(BF16)


def _ff_cols(ref, c, tc):
    d_ff = ref.shape[-1] // 2
    return jnp.concatenate([ref[:, c * tc:(c + 1) * tc],
                            ref[:, d_ff + c * tc:d_ff + (c + 1) * tc]], axis=1)


def _row_slabs(rows):
    na = rows // SUBLANES - TAIL_SLABS
    assert na % 4 == 0 and (na // 4) % 2 == 1
    slabs = [(v, na) for v in range(na)]
    slabs += [(SUBLANES * na + v, TAIL_SLABS) for v in range(TAIL_SLABS)]
    return slabs, na


def _shifted_rows(ub, prev, na):
    S = SUBLANES
    width = ub.shape[1]
    sub = lax.broadcasted_iota(jnp.int32, (S, width), 0)
    slab = lambda i: ub[i * S:(i + 1) * S, :]

    def wrap(x, y):
        return pltpu.roll(jnp.where(sub == S - 1, y, x), 1, axis=0)

    a1 = wrap(slab(na - 1), prev[S:2 * S, :])
    a2 = wrap(slab(na - 2), prev[0:S, :])
    b1 = wrap(slab(na + 3), slab(na - 1))
    b2 = wrap(slab(na + 2), slab(na - 2))
    s1 = jnp.concatenate([a1, ub[0:(na - 1) * S, :], b1, ub[na * S:(na + 3) * S, :]], axis=0)
    s2 = jnp.concatenate([a2, a1, ub[0:(na - 2) * S, :], b2, b1, ub[na * S:(na + 2) * S, :]],
                         axis=0)
    return s1, s2


def _meta_kernel(x_ref, w_in_ref, b_in_ref, sink_ref, w_pool_ref, pscale_ref, g_attn_ref,
                 g_pool_ref, w_o_ref, g_pre_mix_ref, g_post_mix_ref, g_pre_ffn_ref, wup_ref,
                 k0_ref, v0_ref, u0_ref, up0_ref, ubuf):
    rows = x_ref.shape[0]
    pos0 = N_META - rows
    x = x_ref[...]
    z = _dot(_rms(x, g_pre_mix_ref[...]).astype(BF16), w_in_ref[...]) + b_in_ref[...]
    k = z[:, Q_W:Q_W + KV_W]
    v = z[:, Q_W + KV_W:Q_W + 2 * KV_W]
    u = z[:, Q_W + 2 * KV_W:]
    pos = pos0 + lax.broadcasted_iota(jnp.int32, (rows, 1), 0)
    zeros_kv = jnp.zeros((WINDOW, KV_W), BF16)
    k2 = jnp.concatenate([zeros_kv, k.astype(BF16)], axis=0)
    v2 = jnp.concatenate([zeros_kv, v.astype(BF16)], axis=0)
    attn = _attend_block(z[:, :Q_W] * SM_SCALE, k2, v2, pos0, lambda h: sink_ref[h])

    ubuf[0:U_HIST, :] = jnp.zeros((U_HIST, POOL_W), F32)
    ubuf[U_HIST:, :] = jnp.where(pos >= 0, u, 0.0)
    cnt_of = lambda w: jnp.clip(pos + 1, 1, w).astype(F32)
    pool = _pool_rows(ubuf, rows, w_pool_ref, pscale_ref[...], cnt_of)

    x1 = _mix_residual(x, attn, pool, g_attn_ref[...], g_pool_ref[...], w_o_ref,
                       g_post_mix_ref[...])
    h2 = _rms(x1, g_pre_ffn_ref[...]).astype(BF16)
    k0_ref[...] = k
    v0_ref[...] = v
    u0_ref[...] = u[rows - U_HIST:, :]
    tail = 2 * SUBLANES
    tc = up0_ref.shape[2] // 2
    for c in range(up0_ref.shape[0]):
        up_tail = _dot(h2[rows - tail:, :], _ff_cols(wup_ref, c, tc))
        up0_ref[c] = up_tail[tail - CONV_STATE:, :]


def _prompt_kernel(n_tiles, x_ref, k0_ref, v0_ref, u0_ref, up0_ref, w_in_ref, b_in_ref,
                   sink_ref, w_pool_ref, pscale_ref, g_attn_ref, g_pool_ref, w_o_ref,
                   g_pre_mix_ref, g_post_mix_ref, g_pre_ffn_ref, g_post_ffn_ref, wup_ref, cw_ref,
                   cb_ref, wdown_ref,
                   y_ref, kout_ref, vout_ref, uout_ref, cout_ref,
                   kbuf, vbuf, ubuf, upst, xbuf, obuf):
    g = pl.program_id(0)
    n_total = pl.num_programs(0) - 1
    rows = x_ref.shape[1]
    n_chunks, _, tc2 = upst.shape
    tc = tc2 // 2
    n_col = x_ref.shape[2] // LANES
    S = SUBLANES
    slabs, na = _row_slabs(rows)
    st2_row, st1_row = S - 1, 2 * S - 1

    tf = lax.rem(jnp.minimum(g, n_total - 1), n_tiles)
    tj = lax.rem(jnp.maximum(g - 1, 0), n_tiles)
    wslot = lax.rem(g, 2)
    rslot = 1 - wslot

    @pl.when(tf == 0)
    def _():
        kbuf[...] = k0_ref[...]
        vbuf[...] = v0_ref[...]
        ubuf[...] = u0_ref[...]

    @pl.when(tj == 0)
    def _():
        upst[...] = jnp.zeros(upst.shape, F32)
        upst[:, st2_row:st2_row + 1, :] = up0_ref[:, 0:1, :]
        upst[:, st1_row:st1_row + 1, :] = up0_ref[:, 1:2, :]

    @pl.when(g == 0)
    def _():
        xbuf[1] = jnp.zeros(xbuf.shape[1:], F32)

    mixer_vals = {}

    def mixer_in():
        z = _dot(_rms(x_ref[0], g_pre_mix_ref[...]).astype(BF16), w_in_ref[...]) + b_in_ref[...]
        k_new = z[:, Q_W:Q_W + KV_W]
        v_new = z[:, Q_W + KV_W:Q_W + 2 * KV_W]
        u_new = z[:, Q_W + 2 * KV_W:]
        mixer_vals["k"] = jnp.concatenate([kbuf[...], k_new], axis=0).astype(BF16)
        mixer_vals["v"] = jnp.concatenate([vbuf[...], v_new], axis=0).astype(BF16)
        mixer_vals["u"] = jnp.concatenate([ubuf[...], u_new], axis=0)
        mixer_vals["qs"] = z[:, :Q_W] * SM_SCALE
        kbuf[...] = k_new[rows - WINDOW:, :]
        vbuf[...] = v_new[rows - WINDOW:, :]
        ubuf[...] = u_new[rows - U_HIST:, :]

    def mixer_attend(i):
        r0 = i * WINDOW
        mixer_vals["attn", i] = _attend_block(
            mixer_vals["qs"][r0:r0 + WINDOW, :], mixer_vals["k"][r0:r0 + 2 * WINDOW, :],
            mixer_vals["v"][r0:r0 + 2 * WINDOW, :], N_META + tf * rows + r0,
            lambda h: sink_ref[h])

    def mixer_out():
        pool = _pool_rows_full(mixer_vals["u"], rows, w_pool_ref, pscale_ref[...])
        attn = jnp.concatenate([mixer_vals["attn", i] for i in range(rows // WINDOW)], axis=0)
        x1 = _mix_residual(x_ref[0], attn, pool, g_attn_ref[...], g_pool_ref[...],
                           w_o_ref, g_post_mix_ref[...])
        for j in range(n_col):
            xbuf[wslot, j] = x1[:, j * LANES:(j + 1) * LANES]

    def x1_slab_order():
        return jnp.concatenate(
            [jnp.concatenate([xbuf[rslot, j, pl.ds(start, S, stride=stride), :]
                              for j in range(n_col)], axis=1) for start, stride in slabs], axis=0)

    ffn_vals = {}

    def ffn_in():
        ffn_vals["h2"] = _rms(x1_slab_order(), g_pre_ffn_ref[...]).astype(BF16)

    def up_project(c):
        ffn_vals["up", c] = _dot(ffn_vals["h2"], _ff_cols(wup_ref, c, tc))

    def activate(c):
        ub = ffn_vals.pop(("up", c))
        s1, s2 = _shifted_rows(ub, upst[c], na)
        upst[c] = ub[rows - 2 * S:rows, :]
        cw = _ff_cols(cw_ref, c, tc)
        conv = _ff_cols(cb_ref, c, tc) + s2 * cw[0:1, :]
        conv = conv + s1 * cw[1:2, :]
        conv = conv + ub * cw[2:3, :]
        ffn_vals["a", c] = _gated(conv, tc)

    def down_project(c):
        part = _dot(ffn_vals.pop(("a", c)), wdown_ref[c * tc:(c + 1) * tc, :])
        ffn_vals["acc"] = part if c == 0 else ffn_vals["acc"] + part

    def ffn_stage(c):
        if c + 1 < n_chunks:
            up_project(c + 1)
        if c < n_chunks:
            activate(c)
        if c >= 1:
            down_project(c - 1)

    def ffn_out():
        y = x1_slab_order() + _rms(ffn_vals["acc"], g_post_ffn_ref[...])
        for i, (start, stride) in enumerate(slabs):
            for j in range(n_col):
                obuf[j, pl.ds(start, S, stride=stride), :] = y[i * S:(i + 1) * S,
                                                              j * LANES:(j + 1) * LANES]
        y_ref[0] = jnp.concatenate([obuf[j] for j in range(n_col)], axis=1)

    ffn_stages = [lambda c=c: ffn_stage(c) for c in range(n_chunks + 1)]
    mixer_stages = [mixer_in] + [lambda i=i: mixer_attend(i) for i in range(rows // WINDOW)]
    mixer_stages.append(mixer_out)

    ffn_in()
    up_project(0)
    emitted = 0
    for c, stage in enumerate(ffn_stages):
        due = ((c + 1) * len(mixer_stages)) // len(ffn_stages)
        while emitted < due:
            mixer_stages[emitted]()
            emitted += 1
        stage()
    ffn_out()

    @pl.when((tf == n_tiles - 1) & (g < n_total))
    def _():
        kout_ref[0] = kbuf[...].T
        vout_ref[0] = vbuf[...].T
        uout_ref[0] = ubuf[...]

    @pl.when((tj == n_tiles - 1) & (g > 0))
    def _():
        cout_ref[0, :, 0:1, :] = upst[:, st2_row:st2_row + 1, :]
        cout_ref[0, :, 1:2, :] = upst[:, st1_row:st1_row + 1, :]


def _sample_a_kernel(x_ref, ck_ref, cv_ref, w_in_ref, b_in_ref, g_pre_mix_ref, sinkcol_ref,
                     slopecol_ref,
                     attn_ref, u_ref, ko_ref, vo_ref,
                     qs_buf, kn_buf, vn_buf, knt_buf, vnt_buf):
    step = pl.program_id(0)
    n_tok = ck_ref.shape[0]

    @pl.when(step == 0)
    def _():
        z = _dot(_rms(x_ref[:, 0, :], g_pre_mix_ref[...]).astype(BF16), w_in_ref[...])
        z = z + b_in_ref[...]
        k = z[:, Q_W:Q_W + KV_W]
        v = z[:, Q_W + KV_W:Q_W + 2 * KV_W]
        qs_buf[...] = z[:, :Q_W] * SM_SCALE
        kn_buf[...] = k
        vn_buf[...] = v
        knt_buf[...] = k.T
        vnt_buf[...] = v.T
        u_ref[...] = z[:, Q_W + 2 * KV_W:]

    lo1 = _lane_lo((1, LANES))
    kj = lax.broadcasted_iota(jnp.int32, (N_HEADS, WINDOW), 1)
    neg_dist = (kj - WINDOW).astype(F32)
    bias = slopecol_ref[...] * neg_dist
    sink = sinkcol_ref[...]
    last_lane = lax.broadcasted_iota(jnp.int32, (KV_W, WINDOW), 1) == WINDOW - 1

    def token(j, carry):
        b = step * n_tok + j
        qrow = qs_buf[pl.ds(b, 1), :]
        kn = kn_buf[pl.ds(b, 1), :]
        vn = vn_buf[pl.ds(b, 1), :]
        kt = ck_ref[j]
        vt = cv_ref[j]
        heads = []
        for h in range(N_HEADS):
            p, e, g = h // 2, h % 2, h // (N_HEADS // N_KV_HEADS)
            blk = qrow[:, p * LANES:(p + 1) * LANES]
            src = blk if e == g else pltpu.roll(blk, HEAD_DIM, axis=1)
            keep = lo1 if g == 0 else jnp.logical_not(lo1)
            heads.append(jnp.where(keep, src, 0.0))
        qf = jnp.concatenate(heads, axis=0).astype(BF16)
        s = _dot(qf, kt.astype(BF16)) + bias
        s_self = jnp.sum(qf.astype(F32) * kn.astype(BF16).astype(F32), axis=-1, keepdims=True)
        m = jnp.maximum(jnp.maximum(jnp.max(s, axis=-1, keepdims=True), s_self), sink)
        pe = jnp.exp(s - m)
        pe_self = jnp.exp(s_self - m)
        den = jnp.sum(pe, axis=-1, keepdims=True) + pe_self + jnp.exp(sink - m)
        o = _dot_nt(pe.astype(BF16), vt.astype(BF16))
        o = o + pe_self.astype(BF16).astype(F32) * vn.astype(BF16).astype(F32)
        o = o / den
        pieces = []
        for p in range(N_HEADS // 2):
            g = (2 * p) // (N_HEADS // N_KV_HEADS)
            a, c = o[2 * p:2 * p + 1, :], o[2 * p + 1:2 * p + 2, :]
            if g == 0:
                pieces.append(jnp.where(lo1, a, pltpu.roll(c, HEAD_DIM, axis=1)))
            else:
                pieces.append(jnp.where(lo1, pltpu.roll(a, HEAD_DIM, axis=1), c))
        attn_ref[pl.ds(b, 1), :] = jnp.concatenate(pieces, axis=1)
        bring = WINDOW - 1 - b
        ko_ref[j] = jnp.where(last_lane, pltpu.roll(knt_buf[...], bring, axis=1),
                              pltpu.roll(kt, WINDOW - 1, axis=1))
        vo_ref[j] = jnp.where(last_lane, pltpu.roll(vnt_buf[...], bring, axis=1),
                              pltpu.roll(vt, WINDOW - 1, axis=1))
        return carry

    lax.fori_loop(0, n_tok, token, 0, unroll=SAMPLE_UNROLL)


def _sample_b_kernel(x_ref, attn_ref, u_ref, sp_ref, sc_ref, w_pool_ref, pscale_ref,
                     g_attn_ref, g_pool_ref, w_o_ref, g_post_mix_ref, g_pre_ffn_ref,
                     g_post_ffn_ref, wup_ref, cw_ref, cb_ref, wdown_ref,
                     y_ref, po_ref, co_ref):
    tc = FF_CHUNK
    d_ff = wdown_ref.shape[0]
    n_chunks = d_ff // tc
    u = u_ref[...]

    outs = []
    for g, w in enumerate(POOL_WINDOWS):
        cs = slice(g * POOL_GW, (g + 1) * POOL_GW)
        cur = u[:, cs]
        acc_u = cur
        for j in range(1, w):
            acc_u = acc_u + sp_ref[POOL_STATE - j, :, cs]
        outs.append(_dot((acc_u / float(w) - cur).astype(BF16), w_pool_ref[g]))
    pool = jnp.concatenate(outs, axis=1) * pscale_ref[...]
    for r in range(POOL_STATE - 1):
        po_ref[r] = sp_ref[r + 1]
    po_ref[POOL_STATE - 1] = u

    x1 = _mix_residual(x_ref[:, 0, :], attn_ref[...], pool, g_attn_ref[...], g_pool_ref[...],
                       w_o_ref, g_post_mix_ref[...])
    h2 = _rms(x1, g_pre_ffn_ref[...]).astype(BF16)

    co_ref[:, 0, :] = sc_ref[:, 1, :]
    ffn = jnp.zeros(x1.shape, F32)
    for c in range(n_chunks):
        up = _dot(h2, _ff_cols(wup_ref, c, tc))
        gcols = slice(c * tc, (c + 1) * tc)
        vcols = slice(d_ff + c * tc, d_ff + (c + 1) * tc)
        co_ref[:, 1, gcols] = up[:, :tc]
        co_ref[:, 1, vcols] = up[:, tc:]
        old0 = jnp.concatenate([sc_ref[:, 0, gcols], sc_ref[:, 0, vcols]], axis=1)
        old1 = jnp.concatenate([sc_ref[:, 1, gcols], sc_ref[:, 1, vcols]], axis=1)
        cw = _ff_cols(cw_ref, c, tc)
        conv = _ff_cols(cb_ref, c, tc) + old0 * cw[0:1, :]
        conv = conv + old1 * cw[1:2, :]
        conv = conv + up * cw[2:3, :]
        ffn = ffn + _dot(_gated(conv, tc), wdown_ref[c * tc:(c + 1) * tc, :])
    y_ref[:, 0, :] = x1 + _rms(ffn, g_post_ffn_ref[...])


def _vmem():
    return pl.BlockSpec(memory_space=pltpu.VMEM)


def _smem():
    return pl.BlockSpec(memory_space=pltpu.SMEM)


def _resident(shape):
    nd = len(shape)
    return pl.BlockSpec(shape, lambda *_: (0,) * nd, pipeline_mode=pl.Buffered(1))


def _unchunk_cols(a):
    n_chunks, r, tc2 = a.shape
    a = a.reshape(n_chunks, r, 2, tc2 // 2)
    return jnp.transpose(a, (1, 2, 0, 3)).reshape(r, n_chunks * tc2)


def kernel(x_prompt, x_sample, cache_k, cache_v, state_pool, state_conv, meta, w_in, b_in, sinks,
           w_pool, pool_scale, g_attn_out, g_pool_out, w_o, g_pre_mix, g_post_mix, g_pre_ffn,
           g_post_ffn, w_up, conv_w, conv_b, w_down):
    assert w_in.shape[0] == 1, "single layer"
    batch, seq, d_model = x_prompt.shape
    dec_batch = x_sample.shape[0]
    d_ff = w_down.shape[1]
    tc = FF_CHUNK
    n_chunks = d_ff // tc
    assert n_chunks * tc == d_ff and seq % SEQ_TILE == 0 and dec_batch % SAMPLE_TB == 0
    assert meta.shape[0] == N_META and N_META > POOL_STATE

    row = lambda a: a[0].reshape(1, -1)
    w_in_b = w_in[0].astype(BF16)
    b_in_r = row(b_in)
    w_pool_b = w_pool[0].astype(BF16)
    w_o_b = w_o[0].astype(BF16)
    wup_c = w_up[0].astype(BF16)
    half_value = jnp.concatenate([jnp.ones((d_ff,), F32), jnp.full((d_ff,), 0.5, F32)])
    cw_c = conv_w[0] * half_value
    cb_c = (conv_b[0] * half_value).reshape(1, -1)
    wdown_c = w_down[0].astype(BF16)
    sink_s = sinks[0]
    sink_col = sinks[0].reshape(N_HEADS, 1)
    slope_col = jnp.asarray(np.array(SLOPES, np.float32).reshape(N_HEADS, 1))
    gains = dict(pscale=row(pool_scale), g_attn=row(g_attn_out), g_pool=row(g_pool_out),
                 g_pre_mix=row(g_pre_mix), g_post_mix=row(g_post_mix),
                 g_pre_ffn=row(g_pre_ffn), g_post_ffn=row(g_post_ffn))

    x_meta = jnp.concatenate([jnp.zeros((WINDOW - N_META, d_model), F32), meta.astype(F32)], 0)
    k0, v0, u0, up0 = pl.pallas_call(
        _meta_kernel,
        out_shape=(jax.ShapeDtypeStruct((WINDOW, KV_W), F32),
                   jax.ShapeDtypeStruct((WINDOW, KV_W), F32),
                   jax.ShapeDtypeStruct((U_HIST, POOL_W), F32),
                   jax.ShapeDtypeStruct((n_chunks, CONV_STATE, 2 * tc), F32)),
        in_specs=[_vmem(), _vmem(), _vmem(), _smem()] + [_vmem()] * 9,
        out_specs=(_vmem(),) * 4,
        scratch_shapes=[pltpu.VMEM((U_HIST + WINDOW, POOL_W), F32)],
        compiler_params=pltpu.CompilerParams(vmem_limit_bytes=VMEM_LIMIT),
        name="meta",
    )(x_meta, w_in_b, b_in_r, sink_s, w_pool_b, gains["pscale"], gains["g_attn"],
      gains["g_pool"], w_o_b, gains["g_pre_mix"], gains["g_post_mix"], gains["g_pre_ffn"], wup_c)

    n_tiles = seq // SEQ_TILE
    n_total = batch * n_tiles
    mixer_tile = lambda g: jnp.minimum(g, n_total - 1)
    ffn_tile = lambda g: jnp.maximum(g - 1, 0)
    per_batch = lambda shape, tile: pl.BlockSpec(
        (1,) + shape, lambda g: (tile(g) // n_tiles,) + (0,) * len(shape))
    prompt_inputs = (
        x_prompt, k0, v0, u0, up0, w_in_b, b_in_r, sink_s, w_pool_b, gains["pscale"],
        gains["g_attn"], gains["g_pool"], w_o_b, gains["g_pre_mix"], gains["g_post_mix"],
        gains["g_pre_ffn"], gains["g_post_ffn"], wup_c, cw_c, cb_c, wdown_c)
    in_specs = [pl.BlockSpec((1, SEQ_TILE, d_model),
                             lambda g: (mixer_tile(g) // n_tiles, mixer_tile(g) % n_tiles, 0))]
    in_specs += [_smem() if a is sink_s else _resident(a.shape) for a in prompt_inputs[1:]]
    y_prompt, k_p, v_p, u_p, c_p = pl.pallas_call(
        functools.partial(_prompt_kernel, n_tiles),
        grid=(n_total + 1,),
        out_shape=(jax.ShapeDtypeStruct((batch, seq, d_model), F32),
                   jax.ShapeDtypeStruct((batch, WINDOW, KV_W), F32),
                   jax.ShapeDtypeStruct((batch, WINDOW, KV_W), F32),
                   jax.ShapeDtypeStruct((batch, U_HIST, POOL_W), F32),
                   jax.ShapeDtypeStruct((batch, n_chunks, CONV_STATE, 2 * tc), F32)),
        in_specs=in_specs,
        out_specs=(pl.BlockSpec((1, SEQ_TILE, d_model),
                                lambda g: (ffn_tile(g) // n_tiles, ffn_tile(g) % n_tiles, 0)),
                   per_batch((KV_W, WINDOW), mixer_tile), per_batch((KV_W, WINDOW), mixer_tile),
                   per_batch((U_HIST, POOL_W), mixer_tile),
                   per_batch((n_chunks, CONV_STATE, 2 * tc), ffn_tile)),
        scratch_shapes=[
            pltpu.VMEM((WINDOW, KV_W), F32),
            pltpu.VMEM((WINDOW, KV_W), F32),
            pltpu.VMEM((U_HIST, POOL_W), F32),
            pltpu.VMEM((n_chunks, 2 * SUBLANES, 2 * tc), F32),
            pltpu.VMEM((2, d_model // LANES, SEQ_TILE, LANES), F32),
            pltpu.VMEM((d_model // LANES, SEQ_TILE, LANES), F32),
        ],
        compiler_params=pltpu.CompilerParams(
            dimension_semantics=("arbitrary",), vmem_limit_bytes=VMEM_LIMIT),
        name="prompt",
    )(*prompt_inputs)

    feat_pos = lambda c: jnp.transpose(c[0].reshape(dec_batch, WINDOW, KV_W), (0, 2, 1))
    ck, cv = feat_pos(cache_k), feat_pos(cache_v)
    xs = x_sample
    tb = SAMPLE_TB
    whole = lambda shape: pl.BlockSpec(shape, lambda i: (0,) * len(shape))
    cache_spec = pl.BlockSpec((tb, KV_W, WINDOW), lambda i: (i, 0, 0))
    attn_s, u_s, k_s, v_s = pl.pallas_call(
        _sample_a_kernel,
        grid=(dec_batch // tb,),
        out_shape=(jax.ShapeDtypeStruct((dec_batch, Q_W), F32),
                   jax.ShapeDtypeStruct((dec_batch, POOL_W), F32),
                   jax.ShapeDtypeStruct((dec_batch, KV_W, WINDOW), F32),
                   jax.ShapeDtypeStruct((dec_batch, KV_W, WINDOW), F32)),
        in_specs=[whole(xs.shape), cache_spec, cache_spec, _resident(w_in_b.shape),
                  _resident(b_in_r.shape), _resident(gains["g_pre_mix"].shape),
                  _resident(sink_col.shape), _resident(slope_col.shape)],
        out_specs=(whole((dec_batch, Q_W)), whole((dec_batch, POOL_W)), cache_spec, cache_spec),
        scratch_shapes=[pltpu.VMEM((dec_batch, Q_W), F32), pltpu.VMEM((dec_batch, KV_W), F32),
                        pltpu.VMEM((dec_batch, KV_W), F32), pltpu.VMEM((KV_W, dec_batch), F32),
                        pltpu.VMEM((KV_W, dec_batch), F32)],
        compiler_params=pltpu.CompilerParams(
            dimension_semantics=("arbitrary",), vmem_limit_bytes=VMEM_LIMIT),
        name="sample_a",
    )(xs, ck, cv, w_in_b, b_in_r, gains["g_pre_mix"], sink_col, slope_col)

    sp = jnp.transpose(state_pool[0], (1, 0, 2))
    sc = state_conv[0]
    y_s, pool_s, conv_s = pl.pallas_call(
        _sample_b_kernel,
        out_shape=(jax.ShapeDtypeStruct(xs.shape, F32),
                   jax.ShapeDtypeStruct(sp.shape, F32),
                   jax.ShapeDtypeStruct(sc.shape, F32)),
        in_specs=[_vmem()] * 17,
        out_specs=(_vmem(),) * 3,
        compiler_params=pltpu.CompilerParams(vmem_limit_bytes=VMEM_LIMIT),
        name="sample_b",
    )(xs, attn_s, u_s, sp, sc, w_pool_b, gains["pscale"], gains["g_attn"], gains["g_pool"],
      w_o_b, gains["g_post_mix"], gains["g_pre_ffn"], gains["g_post_ffn"], wup_c, cw_c, cb_c,
      wdown_c)

    def pos_feat(c):
        n = c.shape[0]
        return jnp.transpose(c, (0, 2, 1)).reshape(1, n, WINDOW, N_KV_HEADS, HEAD_DIM)

    conv_p = jax.vmap(_unchunk_cols)(c_p)
    return (y_prompt,
            y_s,
            pos_feat(k_p), pos_feat(v_p),
            u_p[:, U_HIST - POOL_STATE:, :][None],
            conv_p[None],
            pos_feat(k_s), pos_feat(v_s),
            jnp.transpose(pool_s, (1, 0, 2))[None],
            conv_s[None])
```

```python
import functools

import numpy as np
import jax
import jax.numpy as jnp
from jax import lax
from jax.experimental import pallas as pl
from jax.experimental.pallas import tpu as pltpu

F32 = jnp.float32
BF16 = jnp.bfloat16

N_META = 16
HEAD_DIM = 64
N_HEADS = 8
N_KV_HEADS = 2
WINDOW = 128
POOL_WINDOWS = (2, 4, 8, 16)
POOL_STATE = 15
CONV_STATE = 2
RMS_EPS = 1e-6
SM_SCALE = HEAD_DIM ** -0.5
SLOPES = tuple(2.0 ** (-(h + 1) * (8.0 / N_HEADS)) for h in range(N_HEADS))

LANES = 128
SUBLANES = 8
KV_W = N_KV_HEADS * HEAD_DIM
Q_W = N_HEADS * HEAD_DIM
POOL_W = 512
POOL_GW = POOL_W // len(POOL_WINDOWS)
U_HIST = 16

SEQ_TILE = 512
FF_CHUNK = 256
SAMPLE_TB = 32
SAMPLE_UNROLL = 8
VMEM_LIMIT = 56 * 1024 * 1024
TAIL_SLABS = 4
GELU_C1 = float(np.sqrt(2.0 / np.pi))
GELU_C2 = GELU_C1 * 0.044715


def _rms(x, g):
    ms = jnp.mean(x * x, axis=-1, keepdims=True)
    return x * lax.rsqrt(ms + RMS_EPS) * g


def _dot(a, b):
    return jnp.dot(a, b, preferred_element_type=F32)


def _dot_nt(a, b):
    return lax.dot_general(a, b, (((1,), (1,)), ((), ())), preferred_element_type=F32)


def _lane_lo(shape):
    return lax.broadcasted_iota(jnp.int32, shape, len(shape) - 1) < HEAD_DIM


def _attend_block(qs, k2, v2, pos_start, sink_of):
    nq, nk = qs.shape[0], k2.shape[0]
    qi = lax.broadcasted_iota(jnp.int32, (nq, nk), 0)
    kj = lax.broadcasted_iota(jnp.int32, (nq, nk), 1)
    valid = (kj >= qi) & (kj <= qi + WINDOW) & (kj >= WINDOW - pos_start)
    neg_dist = jnp.where(valid, (kj - qi - WINDOW).astype(F32), -jnp.inf)
    lo = _lane_lo((nq, LANES))
    group = N_HEADS // N_KV_HEADS
    q_heads = []
    for p in range(N_HEADS // 2):
        g = (2 * p) // group
        blk = qs[:, p * LANES:(p + 1) * LANES]
        rolled = pltpu.roll(blk, HEAD_DIM, axis=1)
        keep = lo if g == 0 else jnp.logical_not(lo)
        for e in range(2):
            q_heads.append(jnp.where(keep, blk if e == g else rolled, 0.0).astype(BF16))
    s_all = _dot_nt(jnp.concatenate(q_heads, axis=0), k2)
    probs, dens = [], []
    for h in range(N_HEADS):
        s = s_all[h * nq:(h + 1) * nq, :] + SLOPES[h] * neg_dist
        sink = sink_of(h)
        m = jnp.maximum(jnp.max(s, axis=-1, keepdims=True), sink)
        pe = jnp.exp(s - m)
        dens.append(jnp.sum(pe, axis=-1, keepdims=True) + jnp.exp(sink - m))
        probs.append(pe.astype(BF16))
    o_all = _dot(jnp.concatenate(probs, axis=0), v2)
    outs = [o_all[h * nq:(h + 1) * nq, :] / dens[h] for h in range(N_HEADS)]
    pieces = []
    for p in range(N_HEADS // 2):
        a, b = outs[2 * p], outs[2 * p + 1]
        if (2 * p) // group == 0:
            pieces.append(jnp.where(lo, a, pltpu.roll(b, HEAD_DIM, axis=1)))
        else:
            pieces.append(jnp.where(lo, pltpu.roll(a, HEAD_DIM, axis=1), b))
    return jnp.concatenate(pieces, axis=1)


def _pool_rows(ubuf, rows, w_pool_ref, pool_scale, cnt_of):
    outs = []
    for g, w in enumerate(POOL_WINDOWS):
        cs = slice(g * POOL_GW, (g + 1) * POOL_GW)
        cur = ubuf[U_HIST:U_HIST + rows, cs]
        acc = cur
        for j in range(1, w):
            acc = acc + ubuf[U_HIST - j:U_HIST - j + rows, cs]
        mean = acc / cnt_of(w)
        outs.append(_dot((mean - cur).astype(BF16), w_pool_ref[g]))
    return jnp.concatenate(outs, axis=1) * pool_scale


def _pool_rows_full(ubuf, rows, w_pool_ref, pool_scale):
    outs = []
    for g, w in enumerate(POOL_WINDOWS):
        h = ubuf[:, g * POOL_GW:(g + 1) * POOL_GW]
        s, k = h, 1
        while k < w:
            if k < SUBLANES:
                s = s + pltpu.roll(s, k, axis=0)
            else:
                s = s + jnp.concatenate([s[:k, :], s[:-k, :]], axis=0)
            k *= 2
        cur = h[U_HIST:, :]
        outs.append(_dot((s[U_HIST:, :] * (1.0 / w) - cur).astype(BF16), w_pool_ref[g]))
    return jnp.concatenate(outs, axis=1) * pool_scale


def _mix_residual(x, attn, pool, g_attn, g_pool, w_o_ref, g_post_mix):
    mixin = jnp.concatenate([_rms(attn, g_attn), _rms(pool, g_pool)], axis=1).astype(BF16)
    return x + _rms(_dot(mixin, w_o_ref[...]), g_post_mix)


def _gated(conv, tc):
    g, hv = conv[:, :tc], conv[:, tc:]
    inner = g * (GELU_C1 + GELU_C2 * (g * g))
    return (g * (1.0 + jnp.tanh(inner)) * hv).astype(BF16)


def _ff_cols(ref, c, tc):
    d_ff = ref.shape[-1] // 2
    return jnp.concatenate([ref[:, c * tc:(c + 1) * tc],
                            ref[:, d_ff + c * tc:d_ff + (c + 1) * tc]], axis=1)


def _row_slabs(rows):
    na = rows // SUBLANES - TAIL_SLABS
    assert na % 4 == 0 and (na // 4) % 2 == 1
    slabs = [(v, na) for v in range(na)]
    slabs += [(SUBLANES * na + v, TAIL_SLABS) for v in range(TAIL_SLABS)]
    return slabs, na


def _shifted_rows(ub, prev, na):
    S = SUBLANES
    width = ub.shape[1]
    sub = lax.broadcasted_iota(jnp.int32, (S, width), 0)
    slab = lambda i: ub[i * S:(i + 1) * S, :]

    def wrap(x, y):
        return pltpu.roll(jnp.where(sub == S - 1, y, x), 1, axis=0)

    a1 = wrap(slab(na - 1), prev[S:2 * S, :])
    a2 = wrap(slab(na - 2), prev[0:S, :])
    b1 = wrap(slab(na + 3), slab(na - 1))
    b2 = wrap(slab(na + 2), slab(na - 2))
    s1 = jnp.concatenate([a1, ub[0:(na - 1) * S, :], b1, ub[na * S:(na + 3) * S, :]], axis=0)
    s2 = jnp.concatenate([a2, a1, ub[0:(na - 2) * S, :], b2, b1, ub[na * S:(na + 2) * S, :]],
                         axis=0)
    return s1, s2


def _meta_kernel(x_ref, w_in_ref, b_in_ref, sink_ref, w_pool_ref, pscale_ref, g_attn_ref,
                 g_pool_ref, w_o_ref, g_pre_mix_ref, g_post_mix_ref, g_pre_ffn_ref, wup_ref,
                 k0_ref, v0_ref, u0_ref, up0_ref, ubuf):
    rows = x_ref.shape[0]
    pos0 = N_META - rows
    x = x_ref[...]
    z = _dot(_rms(x, g_pre_mix_ref[...]).astype(BF16), w_in_ref[...]) + b_in_ref[...]
    k = z[:, Q_W:Q_W + KV_W]
    v = z[:, Q_W + KV_W:Q_W + 2 * KV_W]
    u = z[:, Q_W + 2 * KV_W:]
    pos = pos0 + lax.broadcasted_iota(jnp.int32, (rows, 1), 0)
    zeros_kv = jnp.zeros((WINDOW, KV_W), BF16)
    k2 = jnp.concatenate([zeros_kv, k.astype(BF16)], axis=0)
    v2 = jnp.concatenate([zeros_kv, v.astype(BF16)], axis=0)
    attn = _attend_block(z[:, :Q_W] * SM_SCALE, k2, v2, pos0, lambda h: sink_ref[h])

    ubuf[0:U_HIST, :] = jnp.zeros((U_HIST, POOL_W), F32)
    ubuf[U_HIST:, :] = jnp.where(pos >= 0, u, 0.0)
    cnt_of = lambda w: jnp.clip(pos + 1, 1, w).astype(F32)
    pool = _pool_rows(ubuf, rows, w_pool_ref, pscale_ref[...], cnt_of)

    x1 = _mix_residual(x, attn, pool, g_attn_ref[...], g_pool_ref[...], w_o_ref,
                       g_post_mix_ref[...])
    h2 = _rms(x1, g_pre_ffn_ref[...]).astype(BF16)
    k0_ref[...] = k
    v0_ref[...] = v
    u0_ref[...] = u[rows - U_HIST:, :]
    tail = 2 * SUBLANES
    tc = up0_ref.shape[2] // 2
    for c in range(up0_ref.shape[0]):
        up_tail = _dot(h2[rows - tail:, :], _ff_cols(wup_ref, c, tc))
        up0_ref[c] = up_tail[tail - CONV_STATE:, :]


def _prompt_kernel(n_tiles, x_ref, k0_ref, v0_ref, u0_ref, up0_ref, w_in_ref, b_in_ref,
                   sink_ref, w_pool_ref, pscale_ref, g_attn_ref, g_pool_ref, w_o_ref,
                   g_pre_mix_ref, g_post_mix_ref, g_pre_ffn_ref, g_post_ffn_ref, wup_ref, cw_ref,
                   cb_ref, wdown_ref,
                   y_ref, kout_ref, vout_ref, uout_ref, cout_ref,
                   kbuf, vbuf, ubuf, upst, xbuf, obuf):
    g = pl.program_id(0)
    n_total = pl.num_programs(0) - 1
    rows = x_ref.shape[1]
    n_chunks, _, tc2 = upst.shape
    tc = tc2 // 2
    n_col = x_ref.shape[2] // LANES
    S = SUBLANES
    slabs, na = _row_slabs(rows)
    st2_row, st1_row = S - 1, 2 * S - 1

    tf = lax.rem(jnp.minimum(g, n_total - 1), n_tiles)
    tj = lax.rem(jnp.maximum(g - 1, 0), n_tiles)
    wslot = lax.rem(g, 2)
    rslot = 1 - wslot

    @pl.when(tf == 0)
    def _():
        kbuf[...] = k0_ref[...]
        vbuf[...] = v0_ref[...]
        ubuf[...] = u0_ref[...]

    @pl.when(tj == 0)
    def _():
        upst[...] = jnp.zeros(upst.shape, F32)
        upst[:, st2_row:st2_row + 1, :] = up0_ref[:, 0:1, :]
        upst[:, st1_row:st1_row + 1, :] = up0_ref[:, 1:2, :]

    @pl.when(g == 0)
    def _():
        xbuf[1] = jnp.zeros(xbuf.shape[1:], F32)

    mixer_vals = {}

    def mixer_in():
        z = _dot(_rms(x_ref[0], g_pre_mix_ref[...]).astype(BF16), w_in_ref[...]) + b_in_ref[...]
        k_new = z[:, Q_W:Q_W + KV_W]
        v_new = z[:, Q_W + KV_W:Q_W + 2 * KV_W]
        u_new = z[:, Q_W + 2 * KV_W:]
        mixer_vals["k"] = jnp.concatenate([kbuf[...], k_new], axis=0).astype(BF16)
        mixer_vals["v"] = jnp.concatenate([vbuf[...], v_new], axis=0).astype(BF16)
        mixer_vals["u"] = jnp.concatenate([ubuf[...], u_new], axis=0)
        mixer_vals["qs"] = z[:, :Q_W] * SM_SCALE
        kbuf[...] = k_new[rows - WINDOW:, :]
        vbuf[...] = v_new[rows - WINDOW:, :]
        ubuf[...] = u_new[rows - U_HIST:, :]

    def mixer_attend(i):
        r0 = i * WINDOW
        mixer_vals["attn", i] = _attend_block(
            mixer_vals["qs"][r0:r0 + WINDOW, :], mixer_vals["k"][r0:r0 + 2 * WINDOW, :],
            mixer_vals["v"][r0:r0 + 2 * WINDOW, :], N_META + tf * rows + r0,
            lambda h: sink_ref[h])

    def mixer_out():
        pool = _pool_rows_full(mixer_vals["u"], rows, w_pool_ref, pscale_ref[...])
        attn = jnp.concatenate([mixer_vals["attn", i] for i in range(rows // WINDOW)], axis=0)
        x1 = _mix_residual(x_ref[0], attn, pool, g_attn_ref[...], g_pool_ref[...],
                           w_o_ref, g_post_mix_ref[...])
        for j in range(n_col):
            xbuf[wslot, j] = x1[:, j * LANES:(j + 1) * LANES]

    def x1_slab_order():
        return jnp.concatenate(
            [jnp.concatenate([xbuf[rslot, j, pl.ds(start, S, stride=stride), :]
                              for j in range(n_col)], axis=1) for start, stride in slabs], axis=0)

    ffn_vals = {}

    def ffn_in():
        ffn_vals["h2"] = _rms(x1_slab_order(), g_pre_ffn_ref[...]).astype(BF16)

    def up_project(c):
        ffn_vals["up", c] = _dot(ffn_vals["h2"], _ff_cols(wup_ref, c, tc))

    def activate(c):
        ub = ffn_vals.pop(("up", c))
        s1, s2 = _shifted_rows(ub, upst[c], na)
        upst[c] = ub[rows - 2 * S:rows, :]
        cw = _ff_cols(cw_ref, c, tc)
        conv = _ff_cols(cb_ref, c, tc) + s2 * cw[0:1, :]
        conv = conv + s1 * cw[1:2, :]
        conv = conv + ub * cw[2:3, :]
        ffn_vals["a", c] = _gated(conv, tc)

    def down_project(c):
        part = _dot(ffn_vals.pop(("a", c)), wdown_ref[c * tc:(c + 1) * tc, :])
        ffn_vals["acc"] = part if c == 0 else ffn_vals["acc"] + part

    def ffn_stage(c):
        if c + 1 < n_chunks:
            up_project(c + 1)
        if c < n_chunks:
            activate(c)
        if c >= 1:
            down_project(c - 1)

    def ffn_out():
        y = x1_slab_order() + _rms(ffn_vals["acc"], g_post_ffn_ref[...])
        for i, (start, stride) in enumerate(slabs):
            for j in range(n_col):
                obuf[j, pl.ds(start, S, stride=stride), :] = y[i * S:(i + 1) * S,
                                                              j * LANES:(j + 1) * LANES]
        y_ref[0] = jnp.concatenate([obuf[j] for j in range(n_col)], axis=1)

    ffn_stages = [lambda c=c: ffn_stage(c) for c in range(n_chunks + 1)]
    mixer_stages = [mixer_in] + [lambda i=i: mixer_attend(i) for i in range(rows // WINDOW)]
    mixer_stages.append(mixer_out)

    ffn_in()
    up_project(0)
    emitted = 0
    for c, stage in enumerate(ffn_stages):
        due = ((c + 1) * len(mixer_stages)) // len(ffn_stages)
        while emitted < due:
            mixer_stages[emitted]()
            emitted += 1
        stage()
    ffn_out()

    @pl.when((tf == n_tiles - 1) & (g < n_total))
    def _():
        kout_ref[0] = kbuf[...].T
        vout_ref[0] = vbuf[...].T
        uout_ref[0] = ubuf[...]

    @pl.when((tj == n_tiles - 1) & (g > 0))
    def _():
        cout_ref[0, :, 0:1, :] = upst[:, st2_row:st2_row + 1, :]
        cout_ref[0, :, 1:2, :] = upst[:, st1_row:st1_row + 1, :]


def _sample_a_kernel(x_ref, ck_ref, cv_ref, w_in_ref, b_in_ref, g_pre_mix_ref, sinkcol_ref,
                     slopecol_ref,
                     attn_ref, u_ref, ko_ref, vo_ref,
                     qs_buf, kn_buf, vn_buf, knt_buf, vnt_buf):
    step = pl.program_id(0)
    n_tok = ck_ref.shape[0]

    @pl.when(step == 0)
    def _():
        z = _dot(_rms(x_ref[:, 0, :], g_pre_mix_ref[...]).astype(BF16), w_in_ref[...])
        z = z + b_in_ref[...]
        k = z[:, Q_W:Q_W + KV_W]
        v = z[:, Q_W + KV_W:Q_W + 2 * KV_W]
        qs_buf[...] = z[:, :Q_W] * SM_SCALE
        kn_buf[...] = k
        vn_buf[...] = v
        knt_buf[...] = k.T
        vnt_buf[...] = v.T
        u_ref[...] = z[:, Q_W + 2 * KV_W:]

    lo1 = _lane_lo((1, LANES))
    kj = lax.broadcasted_iota(jnp.int32, (N_HEADS, WINDOW), 1)
    neg_dist = (kj - WINDOW).astype(F32)
    bias = slopecol_ref[...] * neg_dist
    sink = sinkcol_ref[...]
    last_lane = lax.broadcasted_iota(jnp.int32, (KV_W, WINDOW), 1) == WINDOW - 1

    def token(j, carry):
        b = step * n_tok + j
        qrow = qs_buf[pl.ds(b, 1), :]
        kn = kn_buf[pl.ds(b, 1), :]
        vn = vn_buf[pl.ds(b, 1), :]
        kt = ck_ref[j]
        vt = cv_ref[j]
        heads = []
        for h in range(N_HEADS):
            p, e, g = h // 2, h % 2, h // (N_HEADS // N_KV_HEADS)
            blk = qrow[:, p * LANES:(p + 1) * LANES]
            src = blk if e == g else pltpu.roll(blk, HEAD_DIM, axis=1)
            keep = lo1 if g == 0 else jnp.logical_not(lo1)
            heads.append(jnp.where(keep, src, 0.0))
        qf = jnp.concatenate(heads, axis=0).astype(BF16)
        s = _dot(qf, kt.astype(BF16)) + bias
        s_self = jnp.sum(qf.astype(F32) * kn.astype(BF16).astype(F32), axis=-1, keepdims=True)
        m = jnp.maximum(jnp.maximum(jnp.max(s, axis=-1, keepdims=True), s_self), sink)
        pe = jnp.exp(s - m)
        pe_self = jnp.exp(s_self - m)
        den = jnp.sum(pe, axis=-1, keepdims=True) + pe_self + jnp.exp(sink - m)
        o = _dot_nt(pe.astype(BF16), vt.astype(BF16))
        o = o + pe_self.astype(BF16).astype(F32) * vn.astype(BF16).astype(F32)
        o = o / den
        pieces = []
        for p in range(N_HEADS // 2):
            g = (2 * p) // (N_HEADS // N_KV_HEADS)
            a, c = o[2 * p:2 * p + 1, :], o[2 * p + 1:2 * p + 2, :]
            if g == 0:
                pieces.append(jnp.where(lo1, a, pltpu.roll(c, HEAD_DIM, axis=1)))
            else:
                pieces.append(jnp.where(lo1, pltpu.roll(a, HEAD_DIM, axis=1), c))
        attn_ref[pl.ds(b, 1), :] = jnp.concatenate(pieces, axis=1)
        bring = WINDOW - 1 - b
        ko_ref[j] = jnp.where(last_lane, pltpu.roll(knt_buf[...], bring, axis=1),
                              pltpu.roll(kt, WINDOW - 1, axis=1))
        vo_ref[j] = jnp.where(last_lane, pltpu.roll(vnt_buf[...], bring, axis=1),
                              pltpu.roll(vt, WINDOW - 1, axis=1))
        return carry

    lax.fori_loop(0, n_tok, token, 0, unroll=SAMPLE_UNROLL)


def _sample_b_kernel(x_ref, attn_ref, u_ref, sp_ref, sc_ref, w_pool_ref, pscale_ref,
                     g_attn_ref, g_pool_ref, w_o_ref, g_post_mix_ref, g_pre_ffn_ref,
                     g_post_ffn_ref, wup_ref, cw_ref, cb_ref, wdown_ref,
                     y_ref, po_ref, co_ref):
    tc = FF_CHUNK
    d_ff = wdown_ref.shape[0]
    n_chunks = d_ff // tc
    u = u_ref[...]

    outs = []
    for g, w in enumerate(POOL_WINDOWS):
        cs = slice(g * POOL_GW, (g + 1) * POOL_GW)
        cur = u[:, cs]
        acc_u = cur
        for j in range(1, w):
            acc_u = acc_u + sp_ref[POOL_STATE - j, :, cs]
        outs.append(_dot((acc_u / float(w) - cur).astype(BF16), w_pool_ref[g]))
    pool = jnp.concatenate(outs, axis=1) * pscale_ref[...]
    for r in range(POOL_STATE - 1):
        po_ref[r] = sp_ref[r + 1]
    po_ref[POOL_STATE - 1] = u

    x1 = _mix_residual(x_ref[:, 0, :], attn_ref[...], pool, g_attn_ref[...], g_pool_ref[...],
                       w_o_ref, g_post_mix_ref[...])
    h2 = _rms(x1, g_pre_ffn_ref[...]).astype(BF16)

    co_ref[:, 0, :] = sc_ref[:, 1, :]
    ffn = jnp.zeros(x1.shape, F32)
    for c in range(n_chunks):
        up = _dot(h2, _ff_cols(wup_ref, c, tc))
        gcols = slice(c * tc, (c + 1) * tc)
        vcols = slice(d_ff + c * tc, d_ff + (c + 1) * tc)
        co_ref[:, 1, gcols] = up[:, :tc]
        co_ref[:, 1, vcols] = up[:, tc:]
        old0 = jnp.concatenate([sc_ref[:, 0, gcols], sc_ref[:, 0, vcols]], axis=1)
        old1 = jnp.concatenate([sc_ref[:, 1, gcols], sc_ref[:, 1, vcols]], axis=1)
        cw = _ff_cols(cw_ref, c, tc)
        conv = _ff_cols(cb_ref, c, tc) + old0 * cw[0:1, :]
        conv = conv + old1 * cw[1:2, :]
        conv = conv + up * cw[2:3, :]
        ffn = ffn + _dot(_gated(conv, tc), wdown_ref[c * tc:(c + 1) * tc, :])
    y_ref[:, 0, :] = x1 + _rms(ffn, g_post_ffn_ref[...])


def _meta_sample_b_kernel(xm_ref, w_in_ref, b_in_ref, sink_ref, g_pre_mix_ref,
                          x_ref, attn_ref, u_ref, sp_ref, sc_ref, w_pool_ref, pscale_ref,
                          g_attn_ref, g_pool_ref, w_o_ref, g_post_mix_ref, g_pre_ffn_ref,
                          g_post_ffn_ref, wup_ref, cw_ref, cb_ref, wdown_ref,
                          k0_ref, v0_ref, u0_ref, up0_ref, y_ref, po_ref, co_ref, ubuf):
    _meta_kernel(xm_ref, w_in_ref, b_in_ref, sink_ref, w_pool_ref, pscale_ref, g_attn_ref,
                 g_pool_ref, w_o_ref, g_pre_mix_ref, g_post_mix_ref, g_pre_ffn_ref, wup_ref,
                 k0_ref, v0_ref, u0_ref, up0_ref, ubuf)
    _sample_b_kernel(x_ref, attn_ref, u_ref, sp_ref, sc_ref, w_pool_ref, pscale_ref,
                     g_attn_ref, g_pool_ref, w_o_ref, g_post_mix_ref, g_pre_ffn_ref,
                     g_post_ffn_ref, wup_ref, cw_ref, cb_ref, wdown_ref,
                     y_ref, po_ref, co_ref)


def _vmem():
    return pl.BlockSpec(memory_space=pltpu.VMEM)


def _smem():
    return pl.BlockSpec(memory_space=pltpu.SMEM)


def _resident(shape):
    nd = len(shape)
    return pl.BlockSpec(shape, lambda *_: (0,) * nd, pipeline_mode=pl.Buffered(1))


def _unchunk_cols(a):
    n_chunks, r, tc2 = a.shape
    a = a.reshape(n_chunks, r, 2, tc2 // 2)
    return jnp.transpose(a, (1, 2, 0, 3)).reshape(r, n_chunks * tc2)


def kernel(x_prompt, x_sample, cache_k, cache_v, state_pool, state_conv, meta, w_in, b_in, sinks,
           w_pool, pool_scale, g_attn_out, g_pool_out, w_o, g_pre_mix, g_post_mix, g_pre_ffn,
           g_post_ffn, w_up, conv_w, conv_b, w_down):
    assert w_in.shape[0] == 1, "single layer"
    batch, seq, d_model = x_prompt.shape
    dec_batch = x_sample.shape[0]
    d_ff = w_down.shape[1]
    tc = FF_CHUNK
    n_chunks = d_ff // tc
    assert n_chunks * tc == d_ff and seq % SEQ_TILE == 0 and dec_batch % SAMPLE_TB == 0
    assert meta.shape[0] == N_META and N_META > POOL_STATE

    row = lambda a: a[0].reshape(1, -1)
    w_in_b = w_in[0].astype(BF16)
    b_in_r = row(b_in)
    w_pool_b = w_pool[0].astype(BF16)
    w_o_b = w_o[0].astype(BF16)
    wup_c = w_up[0].astype(BF16)
    half_value = jnp.concatenate([jnp.ones((d_ff,), F32), jnp.full((d_ff,), 0.5, F32)])
    cw_c = conv_w[0] * half_value
    cb_c = (conv_b[0] * half_value).reshape(1, -1)
    wdown_c = w_down[0].astype(BF16)
    sink_s = sinks[0]
    sink_col = sinks[0].reshape(N_HEADS, 1)
    slope_col = jnp.asarray(np.array(SLOPES, np.float32).reshape(N_HEADS, 1))
    gains = dict(pscale=row(pool_scale), g_attn=row(g_attn_out), g_pool=row(g_pool_out),
                 g_pre_mix=row(g_pre_mix), g_post_mix=row(g_post_mix),
                 g_pre_ffn=row(g_pre_ffn), g_post_ffn=row(g_post_ffn))

    feat_pos = lambda c: jnp.transpose(c[0].reshape(dec_batch, WINDOW, KV_W), (0, 2, 1))
    ck, cv = feat_pos(cache_k), feat_pos(cache_v)
    xs = x_sample
    tb = SAMPLE_TB
    whole = lambda shape: pl.BlockSpec(shape, lambda i: (0,) * len(shape))
    cache_spec = pl.BlockSpec((tb, KV_W, WINDOW), lambda i: (i, 0, 0))
    attn_s, u_s, k_s, v_s = pl.pallas_call(
        _sample_a_kernel,
        grid=(dec_batch // tb,),
        out_shape=(jax.ShapeDtypeStruct((dec_batch, Q_W), F32),
                   jax.ShapeDtypeStruct((dec_batch, POOL_W), F32),
                   jax.ShapeDtypeStruct((dec_batch, KV_W, WINDOW), F32),
                   jax.ShapeDtypeStruct((dec_batch, KV_W, WINDOW), F32)),
        in_specs=[whole(xs.shape), cache_spec, cache_spec, _resident(w_in_b.shape),
                  _resident(b_in_r.shape), _resident(gains["g_pre_mix"].shape),
                  _resident(sink_col.shape), _resident(slope_col.shape)],
        out_specs=(whole((dec_batch, Q_W)), whole((dec_batch, POOL_W)), cache_spec, cache_spec),
        scratch_shapes=[pltpu.VMEM((dec_batch, Q_W), F32), pltpu.VMEM((dec_batch, KV_W), F32),
                        pltpu.VMEM((dec_batch, KV_W), F32), pltpu.VMEM((KV_W, dec_batch), F32),
                        pltpu.VMEM((KV_W, dec_batch), F32)],
        compiler_params=pltpu.CompilerParams(
            dimension_semantics=("arbitrary",), vmem_limit_bytes=VMEM_LIMIT),
        name="sample_a",
    )(xs, ck, cv, w_in_b, b_in_r, gains["g_pre_mix"], sink_col, slope_col)

    x_meta = jnp.concatenate([jnp.zeros((WINDOW - N_META, d_model), F32), meta.astype(F32)], 0)
    sp = jnp.transpose(state_pool[0], (1, 0, 2))
    sc = state_conv[0]
    k0, v0, u0, up0, y_s, pool_s, conv_s = pl.pallas_call(
        _meta_sample_b_kernel,
        out_shape=(jax.ShapeDtypeStruct((WINDOW, KV_W), F32),
                   jax.ShapeDtypeStruct((WINDOW, KV_W), F32),
                   jax.ShapeDtypeStruct((U_HIST, POOL_W), F32),
                   jax.ShapeDtypeStruct((n_chunks, CONV_STATE, 2 * tc), F32),
                   jax.ShapeDtypeStruct(xs.shape, F32),
                   jax.ShapeDtypeStruct(sp.shape, F32),
                   jax.ShapeDtypeStruct(sc.shape, F32)),
        in_specs=[_vmem(), _vmem(), _vmem(), _smem(), _vmem()] + [_vmem()] * 17,
        out_specs=(_vmem(),) * 7,
        scratch_shapes=[pltpu.VMEM((U_HIST + WINDOW, POOL_W), F32)],
        compiler_params=pltpu.CompilerParams(vmem_limit_bytes=VMEM_LIMIT),
        name="meta_sample_b",
    )(x_meta, w_in_b, b_in_r, sink_s, gains["g_pre_mix"],
      xs, attn_s, u_s, sp, sc, w_pool_b, gains["pscale"], gains["g_attn"], gains["g_pool"],
      w_o_b, gains["g_post_mix"], gains["g_pre_ffn"], gains["g_post_ffn"], wup_c, cw_c, cb_c,
      wdown_c)

    n_tiles = seq // SEQ_TILE
    n_total = batch * n_tiles
    mixer_tile = lambda g: jnp.minimum(g, n_total - 1)
    ffn_tile = lambda g: jnp.maximum(g - 1, 0)
    per_batch = lambda shape, tile: pl.BlockSpec(
        (1,) + shape, lambda g: (tile(g) // n_tiles,) + (0,) * len(shape))
    prompt_inputs = (
        x_prompt, k0, v0, u0, up0, w_in_b, b_in_r, sink_s, w_pool_b, gains["pscale"],
        gains["g_attn"], gains["g_pool"], w_o_b, gains["g_pre_mix"], gains["g_post_mix"],
        gains["g_pre_ffn"], gains["g_post_ffn"], wup_c, cw_c, cb_c, wdown_c)
    in_specs = [pl.BlockSpec((1, SEQ_TILE, d_model),
                             lambda g: (mixer_tile(g) // n_tiles, mixer_tile(g) % n_tiles, 0))]
    in_specs += [_smem() if a is sink_s else _resident(a.shape) for a in prompt_inputs[1:]]
    y_prompt, k_p, v_p, u_p, c_p = pl.pallas_call(
        functools.partial(_prompt_kernel, n_tiles),
        grid=(n_total + 1,),
        out_shape=(jax.ShapeDtypeStruct((batch, seq, d_model), F32),
                   jax.ShapeDtypeStruct((batch, WINDOW, KV_W), F32),
                   jax.ShapeDtypeStruct((batch, WINDOW, KV_W), F32),
                   jax.ShapeDtypeStruct((batch, U_HIST, POOL_W), F32),
                   jax.ShapeDtypeStruct((batch, n_chunks, CONV_STATE, 2 * tc), F32)),
        in_specs=in_specs,
        out_specs=(pl.BlockSpec((1, SEQ_TILE, d_model),
                                lambda g: (ffn_tile(g) // n_tiles, ffn_tile(g) % n_tiles, 0)),
                   per_batch((KV_W, WINDOW), mixer_tile), per_batch((KV_W, WINDOW), mixer_tile),
                   per_batch((U_HIST, POOL_W), mixer_tile),
                   per_batch((n_chunks, CONV_STATE, 2 * tc), ffn_tile)),
        scratch_shapes=[
            pltpu.VMEM((WINDOW, KV_W), F32),
            pltpu.VMEM((WINDOW, KV_W), F32),
            pltpu.VMEM((U_HIST, POOL_W), F32),
            pltpu.VMEM((n_chunks, 2 * SUBLANES, 2 * tc), F32),
            pltpu.VMEM((2, d_model // LANES, SEQ_TILE, LANES), F32),
            pltpu.VMEM((d_model // LANES, SEQ_TILE, LANES), F32),
        ],
        compiler_params=pltpu.CompilerParams(
            dimension_semantics=("arbitrary",), vmem_limit_bytes=VMEM_LIMIT),
        name="prompt",
    )(*prompt_inputs)

    def pos_feat(c):
        n = c.shape[0]
        return jnp.transpose(c, (0, 2, 1)).reshape(1, n, WINDOW, N_KV_HEADS, HEAD_DIM)

    conv_p = jax.vmap(_unchunk_cols)(c_p)
    return (y_prompt,
            y_s,
            pos_feat(k_p), pos_feat(v_p),
            u_p[:, U_HIST - POOL_STATE:, :][None],
            conv_p[None],
            pos_feat(k_s), pos_feat(v_s),
            jnp.transpose(pool_s, (1, 0, 2))[None],
            conv_s[None])
```

```python
import functools

import numpy as np
import jax
import jax.numpy as jnp
from jax import lax
from jax.experimental import pallas as pl
from jax.experimental.pallas import tpu as pltpu

F32 = jnp.float32
BF16 = jnp.bfloat16

N_META = 16
HEAD_DIM = 64
N_HEADS = 8
N_KV_HEADS = 2
WINDOW = 128
POOL_WINDOWS = (2, 4, 8, 16)
POOL_STATE = 15
CONV_STATE = 2
RMS_EPS = 1e-6
SM_SCALE = HEAD_DIM ** -0.5
SLOPES = tuple(2.0 ** (-(h + 1) * (8.0 / N_HEADS)) for h in range(N_HEADS))

LANES = 128
SUBLANES = 8
KV_W = N_KV_HEADS * HEAD_DIM
Q_W = N_HEADS * HEAD_DIM
POOL_W = 512
POOL_GW = POOL_W // len(POOL_WINDOWS)
U_HIST = 16

SEQ_TILE = 512
FF_CHUNK = 256
SAMPLE_TB = 32
SAMPLE_UNROLL = 8
VMEM_LIMIT = 56 * 1024 * 1024
TAIL_SLABS = 4
GELU_C1 = float(np.sqrt(2.0 / np.pi))
GELU_C2 = GELU_C1 * 0.044715


def _rms(x, g):
    ms = jnp.mean(x * x, axis=-1, keepdims=True)
    return x * lax.rsqrt(ms + RMS_EPS) * g


def _dot(a, b):
    return jnp.dot(a, b, preferred_element_type=F32)


def _dot_nt(a, b):
    return lax.dot_general(a, b, (((1,), (1,)), ((), ())), preferred_element_type=F32)


def _lane_lo(shape):
    return lax.broadcasted_iota(jnp.int32, shape, len(shape) - 1) < HEAD_DIM


def _attend_block(qs, k2, v2, pos_start, sink_of):
    nq, nk = qs.shape[0], k2.shape[0]
    qi = lax.broadcasted_iota(jnp.int32, (nq, nk), 0)
    kj = lax.broadcasted_iota(jnp.int32, (nq, nk), 1)
    valid = (kj >= qi) & (kj <= qi + WINDOW) & (kj >= WINDOW - pos_start)
    neg_dist = jnp.where(valid, (kj - qi - WINDOW).astype(F32), -jnp.inf)
    lo = _lane_lo((nq, LANES))
    group = N_HEADS // N_KV_HEADS
    q_heads = []
    for p in range(N_HEADS // 2):
        g = (2 * p) // group
        blk = qs[:, p * LANES:(p + 1) * LANES]
        rolled = pltpu.roll(blk, HEAD_DIM, axis=1)
        keep = lo if g == 0 else jnp.logical_not(lo)
        for e in range(2):
            q_heads.append(jnp.where(keep, blk if e == g else rolled, 0.0).astype(BF16))
    s_all = _dot_nt(jnp.concatenate(q_heads, axis=0), k2)
    probs, dens = [], []
    for h in range(N_HEADS):
        s = s_all[h * nq:(h + 1) * nq, :] + SLOPES[h] * neg_dist
        sink = sink_of(h)
        m = jnp.maximum(jnp.max(s, axis=-1, keepdims=True), sink)
        pe = jnp.exp(s - m)
        dens.append(jnp.sum(pe, axis=-1, keepdims=True) + jnp.exp(sink - m))
        probs.append(pe.astype(BF16))
    o_all = _dot(jnp.concatenate(probs, axis=0), v2)
    outs = [o_all[h * nq:(h + 1) * nq, :] / dens[h] for h in range(N_HEADS)]
    pieces = []
    for p in range(N_HEADS // 2):
        a, b = outs[2 * p], outs[2 * p + 1]
        if (2 * p) // group == 0:
            pieces.append(jnp.where(lo, a, pltpu.roll(b, HEAD_DIM, axis=1)))
        else:
            pieces.append(jnp.where(lo, pltpu.roll(a, HEAD_DIM, axis=1), b))
    return jnp.concatenate(pieces, axis=1)


def _pool_rows(ubuf, rows, w_pool_ref, pool_scale, cnt_of):
    outs = []
    for g, w in enumerate(POOL_WINDOWS):
        cs = slice(g * POOL_GW, (g + 1) * POOL_GW)
        cur = ubuf[U_HIST:U_HIST + rows, cs]
        acc = cur
        for j in range(1, w):
            acc = acc + ubuf[U_HIST - j:U_HIST - j + rows, cs]
        mean = acc / cnt_of(w)
        outs.append(_dot((mean - cur).astype(BF16), w_pool_ref[g]))
    return jnp.concatenate(outs, axis=1) * pool_scale


def _pool_rows_full(ubuf, rows, w_pool_ref, pool_scale):
    outs = []
    for g, w in enumerate(POOL_WINDOWS):
        h = ubuf[:, g * POOL_GW:(g + 1) * POOL_GW]
        s, k = h, 1
        while k < w:
            if k < SUBLANES:
                s = s + pltpu.roll(s, k, axis=0)
            else:
                s = s + jnp.concatenate([s[:k, :], s[:-k, :]], axis=0)
            k *= 2
        cur = h[U_HIST:, :]
        outs.append(_dot((s[U_HIST:, :] * (1.0 / w) - cur).astype(BF16), w_pool_ref[g]))
    return jnp.concatenate(outs, axis=1) * pool_scale


def _mix_residual(x, attn, pool, g_attn, g_pool, w_o_ref, g_post_mix):
    mixin = jnp.concatenate([_rms(attn, g_attn), _rms(pool, g_pool)], axis=1).astype(BF16)
    return x + _rms(_dot(mixin, w_o_ref[...]), g_post_mix)


def _gated(conv, tc):
    g, hv = conv[:, :tc], conv[:, tc:]
    inner = g * (GELU_C1 + GELU_C2 * (g * g))
    return (g * (1.0 + jnp.tanh(inner)) * hv).astype(BF16)


def _ff_cols(ref, c, tc):
    d_ff = ref.shape[-1] // 2
    return jnp.concatenate([ref[:, c * tc:(c + 1) * tc],
                            ref[:, d_ff + c * tc:d_ff + (c + 1) * tc]], axis=1)


def _row_slabs(rows):
    na = rows // SUBLANES - TAIL_SLABS
    assert na % 4 == 0 and (na // 4) % 2 == 1
    slabs = [(v, na) for v in range(na)]
    slabs += [(SUBLANES * na + v, TAIL_SLABS) for v in range(TAIL_SLABS)]
    return slabs, na


def _shifted_rows(ub, prev, na):
    S = SUBLANES
    width = ub.shape[1]
    sub = lax.broadcasted_iota(jnp.int32, (S, width), 0)
    slab = lambda i: ub[i * S:(i + 1) * S, :]

    def wrap(x, y):
        return pltpu.roll(jnp.where(sub == S - 1, y, x), 1, axis=0)

    a1 = wrap(slab(na - 1), prev[S:2 * S, :])
    a2 = wrap(slab(na - 2), prev[0:S, :])
    b1 = wrap(slab(na + 3), slab(na - 1))
    b2 = wrap(slab(na + 2), slab(na - 2))
    s1 = jnp.concatenate([a1, ub[0:(na - 1) * S, :], b1, ub[na * S:(na + 3) * S, :]], axis=0)
    s2 = jnp.concatenate([a2, a1, ub[0:(na - 2) * S, :], b2, b1, ub[na * S:(na + 2) * S, :]],
                         axis=0)
    return s1, s2


def _meta_kernel(x_ref, w_in_ref, b_in_ref, sink_ref, w_pool_ref, pscale_ref, g_attn_ref,
                 g_pool_ref, w_o_ref, g_pre_mix_ref, g_post_mix_ref, g_pre_ffn_ref, wup_ref,
                 k0_ref, v0_ref, u0_ref, up0_ref, ubuf):
    rows = WINDOW
    pos0 = N_META - rows
    x = jnp.concatenate([jnp.zeros((rows - N_META, x_ref.shape[1]), F32), x_ref[...]], axis=0)
    z = _dot(_rms(x, g_pre_mix_ref[...]).astype(BF16), w_in_ref[...]) + b_in_ref[...]
    k = z[:, Q_W:Q_W + KV_W]
    v = z[:, Q_W + KV_W:Q_W + 2 * KV_W]
    u = z[:, Q_W + 2 * KV_W:]
    pos = pos0 + lax.broadcasted_iota(jnp.int32, (rows, 1), 0)
    zeros_kv = jnp.zeros((WINDOW, KV_W), BF16)
    k2 = jnp.concatenate([zeros_kv, k.astype(BF16)], axis=0)
    v2 = jnp.concatenate([zeros_kv, v.astype(BF16)], axis=0)
    attn = _attend_block(z[:, :Q_W] * SM_SCALE, k2, v2, pos0, lambda h: sink_ref[h])

    ubuf[0:U_HIST, :] = jnp.zeros((U_HIST, POOL_W), F32)
    ubuf[U_HIST:, :] = jnp.where(pos >= 0, u, 0.0)
    cnt_of = lambda w: jnp.clip(pos + 1, 1, w).astype(F32)
    pool = _pool_rows(ubuf, rows, w_pool_ref, pscale_ref[...], cnt_of)

    x1 = _mix_residual(x, attn, pool, g_attn_ref[...], g_pool_ref[...], w_o_ref,
                       g_post_mix_ref[...])
    h2 = _rms(x1, g_pre_ffn_ref[...]).astype(BF16)
    k0_ref[...] = k
    v0_ref[...] = v
    u0_ref[...] = u[rows - U_HIST:, :]
    tail = 2 * SUBLANES
    tc = up0_ref.shape[2] // 2
    for c in range(up0_ref.shape[0]):
        up_tail = _dot(h2[rows - tail:, :], _ff_cols(wup_ref, c, tc))
        up0_ref[c] = up_tail[tail - CONV_STATE:, :]


def _prompt_kernel(n_tiles, x_ref, k0_ref, v0_ref, u0_ref, up0_ref, w_in_ref, b_in_ref,
                   sink_ref, w_pool_ref, pscale_ref, g_attn_ref, g_pool_ref, w_o_ref,
                   g_pre_mix_ref, g_post_mix_ref, g_pre_ffn_ref, g_post_ffn_ref, wup_ref, cw_ref,
                   cb_ref, wdown_ref,
                   y_ref, kout_ref, vout_ref, uout_ref, cout_ref,
                   kbuf, vbuf, ubuf, upst, xbuf, obuf):
    g = pl.program_id(0)
    n_total = pl.num_programs(0) - 1
    rows = x_ref.shape[1]
    n_chunks, _, tc2 = upst.shape
    tc = tc2 // 2
    n_col = x_ref.shape[2] // LANES
    S = SUBLANES
    slabs, na = _row_slabs(rows)
    st2_row, st1_row = S - 1, 2 * S - 1

    tf = lax.rem(jnp.minimum(g, n_total - 1), n_tiles)
    tj = lax.rem(jnp.maximum(g - 1, 0), n_tiles)
    wslot = lax.rem(g, 2)
    rslot = 1 - wslot

    @pl.when(tf == 0)
    def _():
        kbuf[...] = k0_ref[...]
        vbuf[...] = v0_ref[...]
        ubuf[...] = u0_ref[...]

    @pl.when(tj == 0)
    def _():
        upst[...] = jnp.zeros(upst.shape, F32)
        upst[:, st2_row:st2_row + 1, :] = up0_ref[:, 0:1, :]
        upst[:, st1_row:st1_row + 1, :] = up0_ref[:, 1:2, :]

    @pl.when(g == 0)
    def _():
        xbuf[1] = jnp.zeros(xbuf.shape[1:], F32)

    mixer_vals = {}

    def mixer_in():
        z = _dot(_rms(x_ref[0], g_pre_mix_ref[...]).astype(BF16), w_in_ref[...]) + b_in_ref[...]
        k_new = z[:, Q_W:Q_W + KV_W]
        v_new = z[:, Q_W + KV_W:Q_W + 2 * KV_W]
        u_new = z[:, Q_W + 2 * KV_W:]
        mixer_vals["k"] = jnp.concatenate([kbuf[...], k_new], axis=0).astype(BF16)
        mixer_vals["v"] = jnp.concatenate([vbuf[...], v_new], axis=0).astype(BF16)
        mixer_vals["u"] = jnp.concatenate([ubuf[...], u_new], axis=0)
        mixer_vals["qs"] = z[:, :Q_W] * SM_SCALE
        kbuf[...] = k_new[rows - WINDOW:, :]
        vbuf[...] = v_new[rows - WINDOW:, :]
        ubuf[...] = u_new[rows - U_HIST:, :]

    def mixer_attend(i):
        r0 = i * WINDOW
        mixer_vals["attn", i] = _attend_block(
            mixer_vals["qs"][r0:r0 + WINDOW, :], mixer_vals["k"][r0:r0 + 2 * WINDOW, :],
            mixer_vals["v"][r0:r0 + 2 * WINDOW, :], N_META + tf * rows + r0,
            lambda h: sink_ref[h])

    def mixer_out():
        pool = _pool_rows_full(mixer_vals["u"], rows, w_pool_ref, pscale_ref[...])
        attn = jnp.concatenate([mixer_vals["attn", i] for i in range(rows // WINDOW)], axis=0)
        x1 = _mix_residual(x_ref[0], attn, pool, g_attn_ref[...], g_pool_ref[...],
                           w_o_ref, g_post_mix_ref[...])
        for j in range(n_col):
            xbuf[wslot, j] = x1[:, j * LANES:(j + 1) * LANES]

    def x1_slab_order():
        return jnp.concatenate(
            [jnp.concatenate([xbuf[rslot, j, pl.ds(start, S, stride=stride), :]
                              for j in range(n_col)], axis=1) for start, stride in slabs], axis=0)

    ffn_vals = {}

    def ffn_in():
        ffn_vals["h2"] = _rms(x1_slab_order(), g_pre_ffn_ref[...]).astype(BF16)

    def up_project(c):
        ffn_vals["up", c] = _dot(ffn_vals["h2"], _ff_cols(wup_ref, c, tc))

    def activate(c):
        ub = ffn_vals.pop(("up", c))
        s1, s2 = _shifted_rows(ub, upst[c], na)
        upst[c] = ub[rows - 2 * S:rows, :]
        cw = _ff_cols(cw_ref, c, tc)
        conv = _ff_cols(cb_ref, c, tc) + s2 * cw[0:1, :]
        conv = conv + s1 * cw[1:2, :]
        conv = conv + ub * cw[2:3, :]
        ffn_vals["a", c] = _gated(conv, tc)

    def down_project(c):
        part = _dot(ffn_vals.pop(("a", c)), wdown_ref[c * tc:(c + 1) * tc, :])
        ffn_vals["acc"] = part if c == 0 else ffn_vals["acc"] + part

    def ffn_stage(c):
        if c + 1 < n_chunks:
            up_project(c + 1)
        if c < n_chunks:
            activate(c)
        if c >= 1:
            down_project(c - 1)

    def ffn_out():
        y = x1_slab_order() + _rms(ffn_vals["acc"], g_post_ffn_ref[...])
        for i, (start, stride) in enumerate(slabs):
            for j in range(n_col):
                obuf[j, pl.ds(start, S, stride=stride), :] = y[i * S:(i + 1) * S,
                                                              j * LANES:(j + 1) * LANES]
        y_ref[0] = jnp.concatenate([obuf[j] for j in range(n_col)], axis=1)

    ffn_stages = [lambda c=c: ffn_stage(c) for c in range(n_chunks + 1)]
    mixer_stages = [mixer_in] + [lambda i=i: mixer_attend(i) for i in range(rows // WINDOW)]
    mixer_stages.append(mixer_out)

    ffn_in()
    up_project(0)
    emitted = 0
    for c, stage in enumerate(ffn_stages):
        due = ((c + 1) * len(mixer_stages)) // len(ffn_stages)
        while emitted < due:
            mixer_stages[emitted]()
            emitted += 1
        stage()
    ffn_out()

    @pl.when((tf == n_tiles - 1) & (g < n_total))
    def _():
        kout_ref[0] = kbuf[...].T
        vout_ref[0] = vbuf[...].T
        uout_ref[0] = ubuf[...]

    @pl.when((tj == n_tiles - 1) & (g > 0))
    def _():
        cout_ref[0, :, 0:1, :] = upst[:, st2_row:st2_row + 1, :]
        cout_ref[0, :, 1:2, :] = upst[:, st1_row:st1_row + 1, :]


def _sample_a_kernel(x_ref, ck_ref, cv_ref, w_in_ref, b_in_ref, g_pre_mix_ref, sinkcol_ref,
                     slopecol_ref,
                     attn_ref, u_ref, ko_ref, vo_ref,
                     qs_buf, kn_buf, vn_buf, knt_buf, vnt_buf):
    step = pl.program_id(0)
    n_tok = ck_ref.shape[0]

    @pl.when(step == 0)
    def _():
        z = _dot(_rms(x_ref[:, 0, :], g_pre_mix_ref[...]).astype(BF16), w_in_ref[...])
        z = z + b_in_ref[...]
        k = z[:, Q_W:Q_W + KV_W]
        v = z[:, Q_W + KV_W:Q_W + 2 * KV_W]
        qs_buf[...] = z[:, :Q_W] * SM_SCALE
        kn_buf[...] = k
        vn_buf[...] = v
        knt_buf[...] = k.T
        vnt_buf[...] = v.T
        u_ref[...] = z[:, Q_W + 2 * KV_W:]

    lo1 = _lane_lo((1, LANES))
    kj = lax.broadcasted_iota(jnp.int32, (N_HEADS, WINDOW), 1)
    neg_dist = (kj - WINDOW).astype(F32)
    bias = slopecol_ref[...] * neg_dist
    sink = sinkcol_ref[...]
    last_lane = lax.broadcasted_iota(jnp.int32, (KV_W, WINDOW), 1) == WINDOW - 1

    def token(j, carry):
        b = step * n_tok + j
        qrow = qs_buf[pl.ds(b, 1), :]
        kn = kn_buf[pl.ds(b, 1), :]
        vn = vn_buf[pl.ds(b, 1), :]
        kt = ck_ref[j]
        vt = cv_ref[j]
        heads = []
        for h in range(N_HEADS):
            p, e, g = h // 2, h % 2, h // (N_HEADS // N_KV_HEADS)
            blk = qrow[:, p * LANES:(p + 1) * LANES]
            src = blk if e == g else pltpu.roll(blk, HEAD_DIM, axis=1)
            keep = lo1 if g == 0 else jnp.logical_not(lo1)
            heads.append(jnp.where(keep, src, 0.0))
        qf = jnp.concatenate(heads, axis=0).astype(BF16)
        s = _dot(qf, kt.astype(BF16)) + bias
        s_self = jnp.sum(qf.astype(F32) * kn.astype(BF16).astype(F32), axis=-1, keepdims=True)
        m = jnp.maximum(jnp.maximum(jnp.max(s, axis=-1, keepdims=True), s_self), sink)
        pe = jnp.exp(s - m)
        pe_self = jnp.exp(s_self - m)
        den = jnp.sum(pe, axis=-1, keepdims=True) + pe_self + jnp.exp(sink - m)
        o = _dot_nt(pe.astype(BF16), vt.astype(BF16))
        o = o + pe_self.astype(BF16).astype(F32) * vn.astype(BF16).astype(F32)
        o = o / den
        pieces = []
        for p in range(N_HEADS // 2):
            g = (2 * p) // (N_HEADS // N_KV_HEADS)
            a, c = o[2 * p:2 * p + 1, :], o[2 * p + 1:2 * p + 2, :]
            if g == 0:
                pieces.append(jnp.where(lo1, a, pltpu.roll(c, HEAD_DIM, axis=1)))
            else:
                pieces.append(jnp.where(lo1, pltpu.roll(a, HEAD_DIM, axis=1), c))
        attn_ref[pl.ds(b, 1), :] = jnp.concatenate(pieces, axis=1)
        bring = WINDOW - 1 - b
        ko_ref[j] = jnp.where(last_lane, pltpu.roll(knt_buf[...], bring, axis=1),
                              pltpu.roll(kt, WINDOW - 1, axis=1))
        vo_ref[j] = jnp.where(last_lane, pltpu.roll(vnt_buf[...], bring, axis=1),
                              pltpu.roll(vt, WINDOW - 1, axis=1))
        return carry

    lax.fori_loop(0, n_tok, token, 0, unroll=SAMPLE_UNROLL)


def _sample_b_kernel(x_ref, attn_ref, u_ref, sp_ref, sc_ref, w_pool_ref, pscale_ref,
                     g_attn_ref, g_pool_ref, w_o_ref, g_post_mix_ref, g_pre_ffn_ref,
                     g_post_ffn_ref, wup_ref, cw_ref, cb_ref, wdown_ref,
                     y_ref, po_ref, co_ref):
    tc = FF_CHUNK
    d_ff = wdown_ref.shape[0]
    n_chunks = d_ff // tc
    u = u_ref[...]

    outs = []
    for g, w in enumerate(POOL_WINDOWS):
        cs = slice(g * POOL_GW, (g + 1) * POOL_GW)
        cur = u[:, cs]
        acc_u = cur
        for j in range(1, w):
            acc_u = acc_u + sp_ref[POOL_STATE - j, :, cs]
        outs.append(_dot((acc_u / float(w) - cur).astype(BF16), w_pool_ref[g]))
    pool = jnp.concatenate(outs, axis=1) * pscale_ref[...]
    for r in range(POOL_STATE - 1):
        po_ref[r] = sp_ref[r + 1]
    po_ref[POOL_STATE - 1] = u

    x1 = _mix_residual(x_ref[:, 0, :], attn_ref[...], pool, g_attn_ref[...], g_pool_ref[...],
                       w_o_ref, g_post_mix_ref[...])
    h2 = _rms(x1, g_pre_ffn_ref[...]).astype(BF16)

    co_ref[:, 0, :] = sc_ref[:, 1, :]
    ffn = jnp.zeros(x1.shape, F32)
    for c in range(n_chunks):
        up = _dot(h2, _ff_cols(wup_ref, c, tc))
        gcols = slice(c * tc, (c + 1) * tc)
        vcols = slice(d_ff + c * tc, d_ff + (c + 1) * tc)
        co_ref[:, 1, gcols] = up[:, :tc]
        co_ref[:, 1, vcols] = up[:, tc:]
        old0 = jnp.concatenate([sc_ref[:, 0, gcols], sc_ref[:, 0, vcols]], axis=1)
        old1 = jnp.concatenate([sc_ref[:, 1, gcols], sc_ref[:, 1, vcols]], axis=1)
        cw = _ff_cols(cw_ref, c, tc)
        conv = _ff_cols(cb_ref, c, tc) + old0 * cw[0:1, :]
        conv = conv + old1 * cw[1:2, :]
        conv = conv + up * cw[2:3, :]
        ffn = ffn + _dot(_gated(conv, tc), wdown_ref[c * tc:(c + 1) * tc, :])
    y_ref[:, 0, :] = x1 + _rms(ffn, g_post_ffn_ref[...])


def _meta_sample_b_kernel(xm_ref, w_in_ref, b_in_ref, sink_ref, g_pre_mix_ref,
                          x_ref, attn_ref, u_ref, sp_ref, sc_ref, w_pool_ref, pscale_ref,
                          g_attn_ref, g_pool_ref, w_o_ref, g_post_mix_ref, g_pre_ffn_ref,
                          g_post_ffn_ref, wup_ref, cw_ref, cb_ref, wdown_ref,
                          k0_ref, v0_ref, u0_ref, up0_ref, y_ref, po_ref, co_ref, ubuf):
    _meta_kernel(xm_ref, w_in_ref, b_in_ref, sink_ref, w_pool_ref, pscale_ref, g_attn_ref,
                 g_pool_ref, w_o_ref, g_pre_mix_ref, g_post_mix_ref, g_pre_ffn_ref, wup_ref,
                 k0_ref, v0_ref, u0_ref, up0_ref, ubuf)
    _sample_b_kernel(x_ref, attn_ref, u_ref, sp_ref, sc_ref, w_pool_ref, pscale_ref,
                     g_attn_ref, g_pool_ref, w_o_ref, g_post_mix_ref, g_pre_ffn_ref,
                     g_post_ffn_ref, wup_ref, cw_ref, cb_ref, wdown_ref,
                     y_ref, po_ref, co_ref)


def _vmem():
    return pl.BlockSpec(memory_space=pltpu.VMEM)


def _smem():
    return pl.BlockSpec(memory_space=pltpu.SMEM)


def _resident(shape):
    nd = len(shape)
    return pl.BlockSpec(shape, lambda *_: (0,) * nd, pipeline_mode=pl.Buffered(1))


def _unchunk_cols(a):
    n_chunks, r, tc2 = a.shape
    a = a.reshape(n_chunks, r, 2, tc2 // 2)
    return jnp.transpose(a, (1, 2, 0, 3)).reshape(r, n_chunks * tc2)


def kernel(x_prompt, x_sample, cache_k, cache_v, state_pool, state_conv, meta, w_in, b_in, sinks,
           w_pool, pool_scale, g_attn_out, g_pool_out, w_o, g_pre_mix, g_post_mix, g_pre_ffn,
           g_post_ffn, w_up, conv_w, conv_b, w_down):
    assert w_in.shape[0] == 1, "single layer"
    batch, seq, d_model = x_prompt.shape
    dec_batch = x_sample.shape[0]
    d_ff = w_down.shape[1]
    tc = FF_CHUNK
    n_chunks = d_ff // tc
    assert n_chunks * tc == d_ff and seq % SEQ_TILE == 0 and dec_batch % SAMPLE_TB == 0
    assert meta.shape[0] == N_META and N_META > POOL_STATE

    row = lambda a: a[0].reshape(1, -1)
    w_in_b = w_in[0].astype(BF16)
    b_in_r = row(b_in)
    w_pool_b = w_pool[0].astype(BF16)
    w_o_b = w_o[0].astype(BF16)
    wup_c = w_up[0].astype(BF16)
    half_value = jnp.concatenate([jnp.ones((d_ff,), F32), jnp.full((d_ff,), 0.5, F32)])
    cw_c = conv_w[0] * half_value
    cb_c = (conv_b[0] * half_value).reshape(1, -1)
    wdown_c = w_down[0].astype(BF16)
    sink_s = sinks[0]
    sink_col = sinks[0].reshape(N_HEADS, 1)
    slope_col = jnp.asarray(np.array(SLOPES, np.float32).reshape(N_HEADS, 1))
    gains = dict(pscale=row(pool_scale), g_attn=row(g_attn_out), g_pool=row(g_pool_out),
                 g_pre_mix=row(g_pre_mix), g_post_mix=row(g_post_mix),
                 g_pre_ffn=row(g_pre_ffn), g_post_ffn=row(g_post_ffn))

    feat_pos = lambda c: jnp.transpose(c[0].reshape(dec_batch, WINDOW, KV_W), (0, 2, 1))
    ck, cv = feat_pos(cache_k), feat_pos(cache_v)
    xs = x_sample
    tb = SAMPLE_TB
    whole = lambda shape: pl.BlockSpec(shape, lambda i: (0,) * len(shape))
    cache_spec = pl.BlockSpec((tb, KV_W, WINDOW), lambda i: (i, 0, 0))
    attn_s, u_s, k_s, v_s = pl.pallas_call(
        _sample_a_kernel,
        grid=(dec_batch // tb,),
        out_shape=(jax.ShapeDtypeStruct((dec_batch, Q_W), F32),
                   jax.ShapeDtypeStruct((dec_batch, POOL_W), F32),
                   jax.ShapeDtypeStruct((dec_batch, KV_W, WINDOW), F32),
                   jax.ShapeDtypeStruct((dec_batch, KV_W, WINDOW), F32)),
        in_specs=[whole(xs.shape), cache_spec, cache_spec, _resident(w_in_b.shape),
                  _resident(b_in_r.shape), _resident(gains["g_pre_mix"].shape),
                  _resident(sink_col.shape), _resident(slope_col.shape)],
        out_specs=(whole((dec_batch, Q_W)), whole((dec_batch, POOL_W)), cache_spec, cache_spec),
        scratch_shapes=[pltpu.VMEM((dec_batch, Q_W), F32), pltpu.VMEM((dec_batch, KV_W), F32),
                        pltpu.VMEM((dec_batch, KV_W), F32), pltpu.VMEM((KV_W, dec_batch), F32),
                        pltpu.VMEM((KV_W, dec_batch), F32)],
        compiler_params=pltpu.CompilerParams(
            dimension_semantics=("arbitrary",), vmem_limit_bytes=VMEM_LIMIT),
        name="sample_a",
    )(xs, ck, cv, w_in_b, b_in_r, gains["g_pre_mix"], sink_col, slope_col)

    x_meta = meta.astype(F32)
    sp = jnp.transpose(state_pool[0], (1, 0, 2))
    sc = state_conv[0]
    k0, v0, u0, up0, y_s, pool_s, conv_s = pl.pallas_call(
        _meta_sample_b_kernel,
        out_shape=(jax.ShapeDtypeStruct((WINDOW, KV_W), F32),
                   jax.ShapeDtypeStruct((WINDOW, KV_W), F32),
                   jax.ShapeDtypeStruct((U_HIST, POOL_W), F32),
                   jax.ShapeDtypeStruct((n_chunks, CONV_STATE, 2 * tc), F32),
                   jax.ShapeDtypeStruct(xs.shape, F32),
                   jax.ShapeDtypeStruct(sp.shape, F32),
                   jax.ShapeDtypeStruct(sc.shape, F32)),
        in_specs=[_vmem(), _vmem(), _vmem(), _smem(), _vmem()] + [_vmem()] * 17,
        out_specs=(_vmem(),) * 7,
        scratch_shapes=[pltpu.VMEM((U_HIST + WINDOW, POOL_W), F32)],
        compiler_params=pltpu.CompilerParams(vmem_limit_bytes=VMEM_LIMIT),
        name="meta_sample_b",
    )(x_meta, w_in_b, b_in_r, sink_s, gains["g_pre_mix"],
      xs, attn_s, u_s, sp, sc, w_pool_b, gains["pscale"], gains["g_attn"], gains["g_pool"],
      w_o_b, gains["g_post_mix"], gains["g_pre_ffn"], gains["g_post_ffn"], wup_c, cw_c, cb_c,
      wdown_c)

    n_tiles = seq // SEQ_TILE
    n_total = batch * n_tiles
    mixer_tile = lambda g: jnp.minimum(g, n_total - 1)
    ffn_tile = lambda g: jnp.maximum(g - 1, 0)
    per_batch = lambda shape, tile: pl.BlockSpec(
        (1,) + shape, lambda g: (tile(g) // n_tiles,) + (0,) * len(shape))
    prompt_inputs = (
        x_prompt, k0, v0, u0, up0, w_in_b, b_in_r, sink_s, w_pool_b, gains["pscale"],
        gains["g_attn"], gains["g_pool"], w_o_b, gains["g_pre_mix"], gains["g_post_mix"],
        gains["g_pre_ffn"], gains["g_post_ffn"], wup_c, cw_c, cb_c, wdown_c)
    in_specs = [pl.BlockSpec((1, SEQ_TILE, d_model),
                             lambda g: (mixer_tile(g) // n_tiles, mixer_tile(g) % n_tiles, 0))]
    in_specs += [_smem() if a is sink_s else _resident(a.shape) for a in prompt_inputs[1:]]
    y_prompt, k_p, v_p, u_p, c_p = pl.pallas_call(
        functools.partial(_prompt_kernel, n_tiles),
        grid=(n_total + 1,),
        out_shape=(jax.ShapeDtypeStruct((batch, seq, d_model), F32),
                   jax.ShapeDtypeStruct((batch, WINDOW, KV_W), F32),
                   jax.ShapeDtypeStruct((batch, WINDOW, KV_W), F32),
                   jax.ShapeDtypeStruct((batch, U_HIST, POOL_W), F32),
                   jax.ShapeDtypeStruct((batch, n_chunks, CONV_STATE, 2 * tc), F32)),
        in_specs=in_specs,
        out_specs=(pl.BlockSpec((1, SEQ_TILE, d_model),
                                lambda g: (ffn_tile(g) // n_tiles, ffn_tile(g) % n_tiles, 0)),
                   per_batch((KV_W, WINDOW), mixer_tile), per_batch((KV_W, WINDOW), mixer_tile),
                   per_batch((U_HIST, POOL_W), mixer_tile),
                   per_batch((n_chunks, CONV_STATE, 2 * tc), ffn_tile)),
        scratch_shapes=[
            pltpu.VMEM((WINDOW, KV_W), F32),
            pltpu.VMEM((WINDOW, KV_W), F32),
            pltpu.VMEM((U_HIST, POOL_W), F32),
            pltpu.VMEM((n_chunks, 2 * SUBLANES, 2 * tc), F32),
            pltpu.VMEM((2, d_model // LANES, SEQ_TILE, LANES), F32),
            pltpu.VMEM((d_model // LANES, SEQ_TILE, LANES), F32),
        ],
        compiler_params=pltpu.CompilerParams(
            dimension_semantics=("arbitrary",), vmem_limit_bytes=VMEM_LIMIT),
        name="prompt",
    )(*prompt_inputs)

    def pos_feat(c):
        n = c.shape[0]
        return jnp.transpose(c, (0, 2, 1)).reshape(1, n, WINDOW, N_KV_HEADS, HEAD_DIM)

    conv_p = jax.vmap(_unchunk_cols)(c_p)
    return (y_prompt,
            y_s,
            pos_feat(k_p), pos_feat(v_p),
            u_p[:, U_HIST - POOL_STATE:, :][None],
            conv_p[None],
            pos_feat(k_s), pos_feat(v_s),
            jnp.transpose(pool_s, (1, 0, 2))[None],
            conv_s[None])
```
